```python
import math
import jax, jax.numpy as jnp
from jax import lax
import numpy as np

D_MODEL = 1024
BATCH = 16
SEQ = 2048
DEPTH = 2
DEC_BATCH = 8
DEC_SEQ = 8192
PAST_LEN = 128

GRID_W = 64
N_EVEN = (DEPTH + 1) // 2
N_ODD = DEPTH // 2
CONV_CH = D_MODEL // 2
CONV_WIDTH = 31
B_PATTERNS = ((128, 1), (512, 4), (2048, 16))
B_GROUPS = len(B_PATTERNS)
B_HEADS = 4
B_HD = 128
B_WIDTH = B_HEADS * B_HD
B_HALF = 64
IN_EVEN = 2 * CONV_CH + 3 * B_GROUPS * B_WIDTH
OUT_EVEN = CONV_CH + B_WIDTH
C_HEADS = 8
C_KV_HEADS = 2
C_GROUP = C_HEADS // C_KV_HEADS
C_HD = 128
IN_ODD = (C_HEADS + 2 * C_KV_HEADS) * C_HD
ROPE_THETA = 10000.0
Q_BLOCK = 128
MEM_LEN = 256
X_HEADS = 4
X_HD = D_MODEL // X_HEADS
FF_DENSE = 2816
N_EXPERTS = 8
TOP_K = 2
FF_EXPERT = 3584
DN_ALPHA = (2 * DEPTH) ** 0.25
DN_BETA = (8 * DEPTH) ** -0.25
LN_EPS = 1e-5
RMS_EPS = 1e-6
NEG = -1e30

kernel_name = "hybrid_conformer_dilated_axial_encoder"


def layer_norm(x, g, b):
    xf = x.astype(jnp.float32)
    mu = jnp.mean(xf, -1, keepdims=True)
    var = jnp.mean(jnp.square(xf - mu), -1, keepdims=True)
    return ((xf - mu) * lax.rsqrt(var + LN_EPS) * g + b).astype(x.dtype)


def rms_norm(x, g):
    xf = x.astype(jnp.float32)
    return (xf * lax.rsqrt(jnp.mean(jnp.square(xf), -1, keepdims=True) + RMS_EPS) * g).astype(x.dtype)


def alibi_slopes(n):
    return jnp.asarray(np.array([2.0 ** (-8.0 * (i + 1) / n) for i in range(n)], np.float32))


def conv_module(u, dw_w, dw_b, cn_g, cn_b):
    a, gate = jnp.split(u, 2, axis=-1)
    h = a * jax.nn.sigmoid(gate)
    h = lax.conv_general_dilated(h, dw_w[:, None, :], window_strides=(1,),
                                 padding=((CONV_WIDTH // 2, CONV_WIDTH // 2),),
                                 dimension_numbers=('NWC', 'WIO', 'NWC'),
                                 feature_group_count=CONV_CH) + dw_b
    return jax.nn.silu(layer_norm(h, cn_g, cn_b))


def dilated_band_attention(q, k, v, dilation, slopes):
    B, S, H, hd = q.shape
    L = S // dilation
    W = B_HALF
    nb = -(-L // W)
    Lp = nb * W

    def to_sub(t):
        return t.reshape(B, L, dilation, H, hd).transpose(0, 2, 3, 1, 4)

    qb = jnp.pad(to_sub(q), ((0, 0), (0, 0), (0, 0), (0, Lp - L), (0, 0))).reshape(B, dilation, H, nb, W, hd)

    def band(t):
        tb = jnp.pad(to_sub(t), ((0, 0), (0, 0), (0, 0), (W, Lp - L + W), (0, 0))).reshape(B, dilation, H, nb + 2, W, hd)
        return jnp.concatenate([tb[:, :, :, :-2], tb[:, :, :, 1:-1], tb[:, :, :, 2:]], axis=-2)

    kb, vb = band(k), band(v)
    s = jnp.einsum('brhnqc,brhnkc->brhnqk', qb, kb).astype(jnp.float32) / math.sqrt(hd)
    qi = jnp.arange(nb)[:, None, None] * W + jnp.arange(W)[None, :, None]
    ki = jnp.arange(nb)[:, None, None] * W - W + jnp.arange(3 * W)[None, None, :]
    rel = ki - qi
    valid = (jnp.abs(rel) <= W) & (ki >= 0) & (ki < L)
    dist = (jnp.abs(rel) * dilation).astype(jnp.float32)
    s = jnp.where(valid, s - slopes[:, None, None, None] * dist, NEG)
    m = jnp.max(s, -1)
    p = jnp.exp(s - m[..., None])
    den = jnp.sum(p, -1)
    num = jnp.einsum('brhnqk,brhnkc->brhnqc', p.astype(vb.dtype), vb).astype(jnp.float32)

    def from_sub(t):
        t = t.reshape(B, dilation, H, Lp, *t.shape[5:])[:, :, :, :L]
        return jnp.moveaxis(t, 3, 1).reshape(B, S, H, *t.shape[4:])

    return from_sub(num), from_sub(m), from_sub(den)


def mixture_of_dilations(q, k, v):
    slopes = alibi_slopes(B_GROUPS * B_HEADS).reshape(B_GROUPS, B_HEADS)
    nums, ms, dens = [], [], []
    for g, (_, d) in enumerate(B_PATTERNS):
        n_g, m_g, s_g = dilated_band_attention(q[:, :, g], k[:, :, g], v[:, :, g], d, slopes[g])
        nums.append(n_g)
        ms.append(m_g)
        dens.append(s_g)
    nums, ms, dens = jnp.stack(nums), jnp.stack(ms), jnp.stack(dens)
    w = jnp.exp(ms - jnp.max(ms, 0))
    return jnp.sum(w[..., None] * nums, 0) / jnp.sum(w * dens, 0)[..., None]


def axial_rope_tables(seq_len):
    rows = seq_len // GRID_W
    row = jnp.broadcast_to(jnp.arange(rows, dtype=jnp.float32)[:, None], (rows, GRID_W)).reshape(-1)
    col = jnp.broadcast_to(jnp.arange(GRID_W, dtype=jnp.float32)[None, :], (rows, GRID_W)).reshape(-1)
    axis_dim = C_HD // 2
    freqs = ROPE_THETA ** (-jnp.arange(0, axis_dim, 2, dtype=jnp.float32) / axis_dim)
    ang_r = row[:, None] * freqs[None, :]
    ang_c = col[:, None] * freqs[None, :]
    ang = jnp.concatenate([ang_r, ang_r, ang_c, ang_c], -1)
    return jnp.cos(ang), jnp.sin(ang)


def apply_axial_rope(x, cos, sin):
    xf = x.astype(jnp.float32)
    half = C_HD // 2

    def rotate_half(u):
        u1, u2 = jnp.split(u, 2, axis=-1)
        return jnp.concatenate([-u2, u1], -1)

    rot = jnp.concatenate([rotate_half(xf[..., :half]), rotate_half(xf[..., half:])], -1)
    return (xf * cos[:, None, :] + rot * sin[:, None, :]).astype(x.dtype)


def blocked_gqa(q, k, v):
    B, S, _, hd = q.shape
    nq = S // Q_BLOCK
    qb = q.reshape(B, nq, Q_BLOCK, C_KV_HEADS, C_GROUP, hd).transpose(1, 0, 3, 4, 2, 5)
    kt = k.transpose(0, 2, 1, 3)
    vt = v.transpose(0, 2, 1, 3)
    scale = 1.0 / math.sqrt(hd)

    def one_block(q_blk):
        s = jnp.einsum('bkgqc,bksc->bkgqs', q_blk, kt).astype(jnp.float32) * scale
        p = jax.nn.softmax(s, axis=-1)
        return jnp.einsum('bkgqs,bksc->bkgqc', p.astype(vt.dtype), vt)

    o = lax.map(one_block, qb)
    return o.transpose(1, 0, 4, 2, 3, 5).reshape(B, S, C_HEADS * hd)


def memory_cross_attention(x, mem, wq, wkv, wo):
    B, S, D = x.shape
    q = (x @ wq).reshape(B, S, X_HEADS, X_HD)
    k, v = jnp.split(mem @ wkv, 2, axis=-1)
    k = k.reshape(B, -1, X_HEADS, X_HD)
    v = v.reshape(B, -1, X_HEADS, X_HD)
    s = jnp.einsum('bshc,bmhc->bhsm', q, k).astype(jnp.float32) / math.sqrt(X_HD)
    p = jax.nn.softmax(s, axis=-1)
    o = jnp.einsum('bhsm,bmhc->bshc', p.astype(v.dtype), v).reshape(B, S, D)
    return o @ wo


def swiglu(x, w13, w2):
    a, b = jnp.split(x @ w13, 2, axis=-1)
    return (jax.nn.silu(a) * b) @ w2


def moe_swiglu(x, router, w13, w2):
    B, S, D = x.shape
    xt = x.reshape(-1, D)
    logits = (xt @ router).astype(jnp.float32)
    top_v, top_i = lax.top_k(logits, TOP_K)
    gates = jax.nn.softmax(top_v, axis=-1)
    combine = jnp.sum(jax.nn.one_hot(top_i, N_EXPERTS, dtype=jnp.float32) * gates[..., None], axis=1)
    y = jnp.zeros(xt.shape, jnp.float32)
    for e in range(N_EXPERTS):
        y = y + combine[:, e:e + 1] * swiglu(xt, w13[e], w2[e]).astype(jnp.float32)
    return y.astype(x.dtype).reshape(B, S, D)


def encoder_trunk(x, mem, e_w_in, e_dw_w, e_dw_b, e_cn_g, e_cn_b, e_w_out, f_w13, f_w2,
                  o_w_in, o_q_norm, o_k_norm, o_w_out, m_router, m_w13, m_w2,
                  x_wq, x_wkv, x_wo, ln_g, ln_b):
    B, S, _ = x.shape
    for i in range(DEPTH):
        j = i // 2
        if i % 2 == 0:
            h = x @ e_w_in[j]
            a_out = conv_module(h[..., :2 * CONV_CH], e_dw_w[j], e_dw_b[j], e_cn_g[j], e_cn_b[j])
            qkv = h[..., 2 * CONV_CH:].reshape(B, S, 3, B_GROUPS, B_HEADS, B_HD)
            b_out = mixture_of_dilations(qkv[:, :, 0], qkv[:, :, 1], qkv[:, :, 2]).reshape(B, S, B_WIDTH)
            mix = jnp.concatenate([a_out, b_out.astype(x.dtype)], -1) @ e_w_out[j]
        else:
            h = x @ o_w_in[j]
            nq, nk = C_HEADS * C_HD, C_KV_HEADS * C_HD
            q = h[..., :nq].reshape(B, S, C_HEADS, C_HD)
            k = h[..., nq:nq + nk].reshape(B, S, C_KV_HEADS, C_HD)
            v = h[..., nq + nk:].reshape(B, S, C_KV_HEADS, C_HD)
            cos, sin = axial_rope_tables(S)
            q = apply_axial_rope(rms_norm(q, o_q_norm[j]), cos, sin)
            k = apply_axial_rope(rms_norm(k, o_k_norm[j]), cos, sin)
            mix = blocked_gqa(q, k, v) @ o_w_out[j]
        x = layer_norm(DN_ALPHA * x + mix, ln_g[i, 0], ln_b[i, 0])
        x = layer_norm(DN_ALPHA * x + memory_cross_attention(x, mem, x_wq[i], x_wkv[i], x_wo[i]),
                       ln_g[i, 1], ln_b[i, 1])
        if i % 2 == 0:
            ffn = swiglu(x, f_w13[j], f_w2[j])
        else:
            ffn = moe_swiglu(x, m_router[j], m_w13[j], m_w2[j])
        x = layer_norm(DN_ALPHA * x + ffn, ln_g[i, 2], ln_b[i, 2])
    return x


def setup_inputs(seed: int = 0) -> dict:
    key = jax.random.key(seed)
    ks = jax.random.split(key, 32)
    f32 = jnp.float32

    def nrm(k, shape, scale):
        return jax.random.normal(k, shape, f32) * scale

    D = D_MODEL
    return {
        'x_prompt': nrm(ks[0], (BATCH, SEQ, D), 1.0),
        'x_sample': nrm(ks[1], (DEC_BATCH, DEC_SEQ, D), 1.0),
        'mem_prompt': nrm(ks[2], (BATCH, MEM_LEN, D), 1.0),
        'mem_sample': nrm(ks[3], (DEC_BATCH, MEM_LEN, D), 1.0),
        'e_w_in': nrm(ks[4], (N_EVEN, D, IN_EVEN), D ** -0.5),
        'e_dw_w': nrm(ks[5], (N_EVEN, CONV_WIDTH, CONV_CH), CONV_WIDTH ** -0.5),
        'e_dw_b': nrm(ks[6], (N_EVEN, CONV_CH), 0.01),
        'e_cn_g': 1.0 + nrm(ks[7], (N_EVEN, CONV_CH), 0.01),
        'e_cn_b': nrm(ks[8], (N_EVEN, CONV_CH), 0.01),
        'e_w_out': nrm(ks[9], (N_EVEN, OUT_EVEN, D), OUT_EVEN ** -0.5 * DN_BETA),
        'f_w13': nrm(ks[10], (N_EVEN, D, 2 * FF_DENSE), D ** -0.5),
        'f_w2': nrm(ks[11], (N_EVEN, FF_DENSE, D), FF_DENSE ** -0.5 * DN_BETA),
        'o_w_in': nrm(ks[12], (N_ODD, D, IN_ODD), D ** -0.5),
        'o_q_norm': 1.0 + nrm(ks[13], (N_ODD, C_HD), 0.01),
        'o_k_norm': 1.0 + nrm(ks[14], (N_ODD, C_HD), 0.01),
        'o_w_out': nrm(ks[15], (N_ODD, C_HEADS * C_HD, D), (C_HEADS * C_HD) ** -0.5 * DN_BETA),
        'm_router': nrm(ks[16], (N_ODD, D, N_EXPERTS), D ** -0.5),
        'm_w13': nrm(ks[17], (N_ODD, N_EXPERTS, D, 2 * FF_EXPERT), D ** -0.5),
        'm_w2': nrm(ks[18], (N_ODD, N_EXPERTS, FF_EXPERT, D), FF_EXPERT ** -0.5 * DN_BETA),
        'x_wq': nrm(ks[19], (DEPTH, D, D), D ** -0.5),
        'x_wkv': nrm(ks[20], (DEPTH, D, 2 * D), D ** -0.5),
        'x_wo': nrm(ks[21], (DEPTH, D, D), D ** -0.5 * DN_BETA),
        'ln_g': 1.0 + nrm(ks[22], (DEPTH, 3, D), 0.01),
        'ln_b': nrm(ks[23], (DEPTH, 3, D), 0.01),
    }


def reference(x_prompt, x_sample, mem_prompt, mem_sample, e_w_in, e_dw_w, e_dw_b, e_cn_g, e_cn_b,
              e_w_out, f_w13, f_w2, o_w_in, o_q_norm, o_k_norm, o_w_out, m_router, m_w13, m_w2,
              x_wq, x_wkv, x_wo, ln_g, ln_b):
    weights = (e_w_in, e_dw_w, e_dw_b, e_cn_g, e_cn_b, e_w_out, f_w13, f_w2,
               o_w_in, o_q_norm, o_k_norm, o_w_out, m_router, m_w13, m_w2,
               x_wq, x_wkv, x_wo, ln_g, ln_b)
    y_prompt = encoder_trunk(x_prompt, mem_prompt, *weights)
    y_sample = encoder_trunk(x_sample, mem_sample, *weights)
    return (y_prompt, y_sample)
```

```python
import functools
import math

import numpy as np
import jax
import jax.numpy as jnp
from jax import lax
from jax.experimental import pallas as pl
from jax.experimental.pallas import tpu as pltpu

F32 = jnp.float32
BF16 = jnp.bfloat16

D_MODEL = 1024
DEPTH = 2
GRID_W = 64
CONV_CH = D_MODEL // 2
CONV_WIDTH = 31
B_DILATIONS = (1, 4, 16)
B_GROUPS = len(B_DILATIONS)
B_HEADS = 4
B_HD = 128
B_WIDTH = B_HEADS * B_HD
B_HALF = 64
QKV_EVEN = 3 * B_GROUPS * B_WIDTH
C_HEADS = 8
C_KV_HEADS = 2
C_GROUP = C_HEADS // C_KV_HEADS
C_HD = 128
ROPE_THETA = 10000.0
X_HEADS = 4
X_HD = D_MODEL // X_HEADS
FF_DENSE = 2816
N_EXPERTS = 8
FF_EXPERT = 3584
DN_ALPHA = (2 * DEPTH) ** 0.25
LN_EPS = 1e-5
RMS_EPS = 1e-6
NEG = -1e30

LANES = 128
V7X_VMEM_LIMIT = 56 * 1024 * 1024


def _params(*semantics):
    return pltpu.CompilerParams(dimension_semantics=semantics,
                                vmem_limit_bytes=V7X_VMEM_LIMIT)


def _layer_norm(y, g, b):
    mu = jnp.mean(y, -1, keepdims=True)
    yc = y - mu
    var = jnp.mean(yc * yc, -1, keepdims=True)
    return yc * lax.rsqrt(var + LN_EPS) * g + b


def _dot(a, b):
    return jnp.dot(a, b, preferred_element_type=F32)


def _dot_nt(a, b):
    return lax.dot_general(a, b, (((1,), (1,)), ((), ())), preferred_element_type=F32)


def _matmul_kernel(x_ref, w_ref, *o_refs, splits, chunk):
    xb = x_ref[...].astype(BF16)
    col = 0
    for o_ref, width in zip(o_refs, splits):
        for c in range(0, width, chunk):
            o_ref[:, c:c + chunk] = _dot(xb, w_ref[:, col + c:col + c + chunk]).astype(o_ref.dtype)
        col += width


def _matmul(x, w, splits, dtypes, tm=512, chunk=512):
    n, k = x.shape
    return pl.pallas_call(
        functools.partial(_matmul_kernel, splits=splits, chunk=chunk),
        grid=(n // tm,),
        in_specs=[pl.BlockSpec((tm, k), lambda i: (i, 0)),
                  pl.BlockSpec(w.shape, lambda i: (0, 0))],
        out_specs=[pl.BlockSpec((tm, s), lambda i: (i, 0)) for s in splits],
        out_shape=[jax.ShapeDtypeStruct((n, s), dt) for s, dt in zip(splits, dtypes)],
        compiler_params=_params("parallel"),
    )(x, w)


CONV_HALO = 16
CONV_ROWS = 64


def _conv_kernel(prev_ref, cur_ref, next_ref, w_ref, b_ref, g_ref, be_ref, o_ref, hwin, *, ts):
    s = pl.program_id(1)
    last = pl.num_programs(1) - 1

    def glu(u):
        return u[:, :CONV_CH] * jax.nn.sigmoid(u[:, CONV_CH:])

    hwin[0:CONV_HALO, :] = jnp.where(s > 0, glu(prev_ref[...]), 0.0)
    hwin[CONV_HALO:CONV_HALO + ts, :] = glu(cur_ref[...])
    hwin[CONV_HALO + ts:, :] = jnp.where(s < last, glu(next_ref[...]), 0.0)
    pad = CONV_WIDTH // 2
    for c in range(0, ts, CONV_ROWS):
        acc = jnp.broadcast_to(b_ref[...], (CONV_ROWS, CONV_CH))
        for j in range(CONV_WIDTH):
            start = CONV_HALO + c + j - pad
            acc = acc + w_ref[j:j + 1, :] * hwin[start:start + CONV_ROWS, :]
        y = _layer_norm(acc, g_ref[...], be_ref[...])
        o_ref[c:c + CONV_ROWS, :] = (y * jax.nn.sigmoid(y)).astype(o_ref.dtype)


def _conv_module(u, dw_w, dw_b, cn_g, cn_b, ts=512):
    b, s, _ = u.shape
    hb = ts // CONV_HALO
    nhalo = s // CONV_HALO
    row = lambda a: a.reshape(1, CONV_CH)
    return pl.pallas_call(
        functools.partial(_conv_kernel, ts=ts),
        grid=(b, s // ts),
        in_specs=[
            pl.BlockSpec((None, CONV_HALO, 2 * CONV_CH),
                         lambda i, j: (i, jnp.maximum(j * hb - 1, 0), 0)),
            pl.BlockSpec((None, ts, 2 * CONV_CH), lambda i, j: (i, j, 0)),
            pl.BlockSpec((None, CONV_HALO, 2 * CONV_CH),
                         lambda i, j: (i, jnp.minimum((j + 1) * hb, nhalo - 1), 0)),
            pl.BlockSpec((CONV_WIDTH, CONV_CH), lambda i, j: (0, 0)),
            pl.BlockSpec((1, CONV_CH), lambda i, j: (0, 0)),
            pl.BlockSpec((1, CONV_CH), lambda i, j: (0, 0)),
            pl.BlockSpec((1, CONV_CH), lambda i, j: (0, 0)),
        ],
        out_specs=pl.BlockSpec((None, ts, CONV_CH), lambda i, j: (i, j, 0)),
        out_shape=jax.ShapeDtypeStruct((b, s, CONV_CH), BF16),
        scratch_shapes=[pltpu.VMEM((ts + 2 * CONV_HALO, CONV_CH), F32)],
        compiler_params=_params("parallel", "parallel"),
    )(u, u, u, dw_w, row(dw_b), row(cn_g), row(cn_b))


BAND_Q = 128


def _dilated_kernel(slope_ref, q_ref, k_ref, v_ref, *rest, length, dilation, group, has_prev, final):
    if has_prev:
        po_ref, plse_ref, *outs = rest
    else:
        outs = rest
    o_ref = outs[0]
    head = pl.program_id(2)
    slope = slope_ref[group * B_HEADS + head] * float(dilation)
    kw = min(2 * BAND_Q, length)
    scale = 1.0 / math.sqrt(B_HD)

    def body(qi, carry):
        q0 = pl.multiple_of(qi * BAND_Q, BAND_Q)
        ws = pl.multiple_of(jnp.clip(q0 - B_HALF, 0, length - kw), B_HALF)
        q = q_ref[pl.ds(q0, BAND_Q), :]
        k = k_ref[pl.ds(ws, kw), :]
        v = v_ref[pl.ds(ws, kw), :]
        s = _dot_nt(q, k) * scale
        row = lax.broadcasted_iota(jnp.int32, (BAND_Q, kw), 0)
        col = lax.broadcasted_iota(jnp.int32, (BAND_Q, kw), 1)
        dist = jnp.abs(col - row + (ws - q0))
        s = jnp.where(dist <= B_HALF, s - slope * dist.astype(F32), NEG)
        m = jnp.max(s, -1, keepdims=True)
        p = jnp.exp(s - m)
        l = jnp.sum(p, -1, keepdims=True)
        o = _dot(p.astype(BF16), v) / l
        lse = m + jnp.log(l)
        if has_prev:
            po = po_ref[pl.ds(q0, BAND_Q), :]
            plse = plse_ref[pl.ds(q0, BAND_Q), :]
            top = jnp.maximum(plse, lse)
            wp = jnp.exp(plse - top)
            wc = jnp.exp(lse - top)
            tot = wp + wc
            o = (wp * po + wc * o) / tot
            lse = top + jnp.log(tot)
        o_ref[pl.ds(q0, BAND_Q), :] = o.astype(o_ref.dtype)
        if not final:
            outs[1][pl.ds(q0, BAND_Q), :] = jnp.broadcast_to(lse, (BAND_Q, B_HD))
        return carry

    lax.fori_loop(0, length // BAND_Q, body, 0)


def _dilated_group(qkv, slopes, group, state, final):
    b, s, _ = qkv.shape
    d = B_DILATIONS[group]
    length = s // d
    ncol = QKV_EVEN // B_HD
    qkv_v = qkv.reshape(b, length, d * QKV_EVEN)

    def col_spec(which):
        off = (which * B_GROUPS + group) * B_HEADS
        return pl.BlockSpec((None, length, B_HD), lambda i, r, h: (i, 0, r * ncol + off + h))

    state_spec = pl.BlockSpec((None, length, B_HD), lambda i, r, h: (i, 0, r * B_HEADS + h))
    in_specs = [pl.BlockSpec(memory_space=pltpu.SMEM), col_spec(0), col_spec(1), col_spec(2)]
    args = [slopes, qkv_v, qkv_v, qkv_v]
    if state is not None:
        in_specs += [state_spec, state_spec]
        args += [t.reshape(b, length, d * B_WIDTH) for t in state]
    if final:
        out_specs = [state_spec]
        out_shape = [jax.ShapeDtypeStruct((b, length, d * B_WIDTH), BF16)]
    else:
        out_specs = [state_spec, state_spec]
        out_shape = [jax.ShapeDtypeStruct((b, length, d * B_WIDTH), F32)] * 2
    outs = pl.pallas_call(
        functools.partial(_dilated_kernel, length=length, dilation=d, group=group,
                          has_prev=state is not None, final=final),
        grid=(b, d, B_HEADS),
        in_specs=in_specs, out_specs=out_specs, out_shape=out_shape,
        compiler_params=_params("parallel", "parallel", "parallel"),
    )(*args)
    return [t.reshape(b, s, B_WIDTH) for t in outs]


def _mixture_of_dilations(qkv):
    n = B_GROUPS * B_HEADS
    slopes = jnp.asarray(np.array([2.0 ** (-8.0 * (i + 1) / n) for i in range(n)], np.float32))
    state = None
    for group in reversed(range(B_GROUPS)):
        state = _dilated_group(qkv, slopes, group, state, final=(group == 0))
    return state[0]


def _proj_ln_kernel(*refs, n_in):
    ins = refs[:n_in]
    ws = refs[n_in:2 * n_in]
    x_ref, g_ref, b_ref, o_ref = refs[2 * n_in:]
    y = DN_ALPHA * x_ref[...]
    for a_ref, w_ref in zip(ins, ws):
        y = y + _dot(a_ref[...], w_ref[...])
    o_ref[...] = _layer_norm(y, g_ref[...], b_ref[...])


def _proj_ln(ins, ws, x, g, b, tm=512):
    n, dm = x.shape
    n_in = len(ins)
    in_specs = ([pl.BlockSpec((tm, a.shape[1]), lambda i: (i, 0)) for a in ins]
                + [pl.BlockSpec(w.shape, lambda i: (0, 0)) for w in ws]
                + [pl.BlockSpec((tm, dm), lambda i: (i, 0)),
                   pl.BlockSpec((1, dm), lambda i: (0, 0)),
                   pl.BlockSpec((1, dm), lambda i: (0, 0))])
    return pl.pallas_call(
        functools.partial(_proj_ln_kernel, n_in=n_in),
        grid=(n // tm,),
        in_specs=in_specs,
        out_specs=pl.BlockSpec((tm, dm), lambda i: (i, 0)),
        out_shape=jax.ShapeDtypeStruct((n, dm), F32),
        compiler_params=_params("parallel"),
    )(*ins, *ws, x, g.reshape(1, dm), b.reshape(1, dm))


def _xattn_kernel(x_ref, k_ref, v_ref, wq_ref, wo_ref, g_ref, b_ref, o_ref):
    x = x_ref[...]
    q = (_dot(x.astype(BF16), wq_ref[...]) * (1.0 / math.sqrt(X_HD))).astype(BF16)
    heads = []
    for h in range(X_HEADS):
        sl = slice(h * X_HD, (h + 1) * X_HD)
        s = _dot_nt(q[:, sl], k_ref[:, sl])
        m = jnp.max(s, -1, keepdims=True)
        p = jnp.exp(s - m)
        l = jnp.sum(p, -1, keepdims=True)
        heads.append((_dot(p.astype(BF16), v_ref[:, sl]) / l).astype(BF16))
    o = jnp.concatenate(heads, -1)
    y = DN_ALPHA * x + _dot(o, wo_ref[...])
    o_ref[...] = _layer_norm(y, g_ref[...], b_ref[...])


def _xattn(x, kv, wq, wo, g, b, tm=512):
    bsz, s, dm = x.shape
    m = kv.shape[1]
    return pl.pallas_call(
        _xattn_kernel,
        grid=(bsz, s // tm),
        in_specs=[pl.BlockSpec((None, tm, dm), lambda i, j: (i, j, 0)),
                  pl.BlockSpec((None, m, dm), lambda i, j: (i, 0, 0)),
                  pl.BlockSpec((None, m, dm), lambda i, j: (i, 0, 1)),
                  pl.BlockSpec((dm, dm), lambda i, j: (0, 0)),
                  pl.BlockSpec((dm, dm), lambda i, j: (0, 0)),
                  pl.BlockSpec((1, dm), lambda i, j: (0, 0)),
                  pl.BlockSpec((1, dm), lambda i, j: (0, 0))],
        out_specs=pl.BlockSpec((None, tm, dm), lambda i, j: (i, j, 0)),
        out_shape=jax.ShapeDtypeStruct((bsz, s, dm), F32),
        compiler_params=_params("parallel", "parallel"),
    )(x, kv, kv, wq, wo, g.reshape(1, dm), b.reshape(1, dm))


def _ffn_kernel(x_ref, w1_ref, w3_ref, w2_ref, g_ref, b_ref, o_ref, acc):
    j = pl.program_id(1)

    @pl.when(j == 0)
    def _():
        acc[...] = DN_ALPHA * x_ref[...]

    xb = x_ref[...].astype(BF16)
    a = _dot(xb, w1_ref[...])
    gate = _dot(xb, w3_ref[...])
    h = (a * jax.nn.sigmoid(a) * gate).astype(BF16)
    acc[...] += _dot(h, w2_ref[...])

    @pl.when(j == pl.num_programs(1) - 1)
    def _():
        o_ref[...] = _layer_norm(acc[...], g_ref[...], b_ref[...])


def _ffn(x, w13, w2, g, b, tm=512, tf=1408):
    n, dm = x.shape
    ff = w2.shape[0]
    nf = ff // tf
    return pl.pallas_call(
        _ffn_kernel,
        grid=(n // tm, nf),
        in_specs=[pl.BlockSpec((tm, dm), lambda i, j: (i, 0)),
                  pl.BlockSpec((dm, tf), lambda i, j: (0, j)),
                  pl.BlockSpec((dm, tf), lambda i, j: (0, nf + j)),
                  pl.BlockSpec((tf, dm), lambda i, j: (j, 0)),
                  pl.BlockSpec((1, dm), lambda i, j: (0, 0)),
                  pl.BlockSpec((1, dm), lambda i, j: (0, 0))],
        out_specs=pl.BlockSpec((tm, dm), lambda i, j: (i, 0)),
        out_shape=jax.ShapeDtypeStruct((n, dm), F32),
        scratch_shapes=[pltpu.VMEM((tm, dm), F32)],
        compiler_params=_params("parallel", "arbitrary"),
    )(x, w13, w13, w2, g.reshape(1, dm), b.reshape(1, dm))


def _rope_tables(seq_len):
    rows = seq_len // GRID_W
    row = jnp.broadcast_to(jnp.arange(rows, dtype=F32)[:, None], (rows, GRID_W)).reshape(-1)
    col = jnp.broadcast_to(jnp.arange(GRID_W, dtype=F32)[None, :], (rows, GRID_W)).reshape(-1)
    axis_dim = C_HD // 2
    freqs = ROPE_THETA ** (-jnp.arange(0, axis_dim, 2, dtype=F32) / axis_dim)
    ang_r = row[:, None] * freqs[None, :]
    ang_c = col[:, None] * freqs[None, :]
    ang = jnp.concatenate([ang_r, ang_r, ang_c, ang_c], -1)
    cos, sin = jnp.cos(ang), jnp.sin(ang)
    first = (jnp.arange(C_HD) % axis_dim) < axis_dim // 2
    sin_up = jnp.where(first[None, :], -sin, 0.0)
    sin_dn = jnp.where(first[None, :], 0.0, sin)
    return cos, sin_up, sin_dn


def _odd_in_kernel(x_ref, w_ref, qn_ref, kn_ref, cos_ref, su_ref, sd_ref, q_ref, k_ref, v_ref):
    xb = x_ref[...].astype(BF16)
    cos, su, sd = cos_ref[...], su_ref[...], sd_ref[...]
    quarter = C_HD // 4
    q_scale = math.log2(math.e) / math.sqrt(C_HD)

    def norm_rope(h, gain):
        h = h * lax.rsqrt(jnp.mean(h * h, -1, keepdims=True) + RMS_EPS) * gain
        up = pltpu.roll(h, C_HD - quarter, 1)
        dn = pltpu.roll(h, quarter, 1)
        return h * cos + up * su + dn * sd

    nq, nk = C_HEADS * C_HD, C_KV_HEADS * C_HD
    for i in range(C_HEADS):
        h = _dot(xb, w_ref[:, i * C_HD:(i + 1) * C_HD])
        q_ref[:, i * C_HD:(i + 1) * C_HD] = (norm_rope(h, qn_ref[...]) * q_scale).astype(BF16)
    for i in range(C_KV_HEADS):
        h = _dot(xb, w_ref[:, nq + i * C_HD:nq + (i + 1) * C_HD])
        k_ref[:, i * C_HD:(i + 1) * C_HD] = norm_rope(h, kn_ref[...]).astype(BF16)
    v_ref[...] = _dot(xb, w_ref[:, nq + nk:]).astype(BF16)


def _odd_in_proj(x, w, qn, kn, tm=512):
    b, s, dm = x.shape
    nq, nk = C_HEADS * C_HD, C_KV_HEADS * C_HD
    cos, su, sd = _rope_tables(s)
    tab = pl.BlockSpec((tm, C_HD), lambda i, j: (j, 0))
    vec = pl.BlockSpec((1, C_HD), lambda i, j: (0, 0))
    return pl.pallas_call(
        _odd_in_kernel,
        grid=(b, s // tm),
        in_specs=[pl.BlockSpec((None, tm, dm), lambda i, j: (i, j, 0)),
                  pl.BlockSpec(w.shape, lambda i, j: (0, 0)),
                  vec, vec, tab, tab, tab],
        out_specs=[pl.BlockSpec((None, tm, nq), lambda i, j: (i, j, 0)),
                   pl.BlockSpec((None, tm, nk), lambda i, j: (i, j, 0)),
                   pl.BlockSpec((None, tm, nk), lambda i, j: (i, j, 0))],
        out_shape=[jax.ShapeDtypeStruct((b, s, nq), BF16),
                   jax.ShapeDtypeStruct((b, s, nk), BF16),
                   jax.ShapeDtypeStruct((b, s, nk), BF16)],
        compiler_params=_params("parallel", "parallel"),
    )(x, w, qn.reshape(1, C_HD), kn.reshape(1, C_HD), cos, su, sd)


def _gqa_kernel(q_ref, k_ref, v_ref, o_ref, m_sc, l_sc, acc_sc, *, seq, tq, tk):
    q = jnp.concatenate([q_ref[:, h * C_HD:(h + 1) * C_HD] for h in range(C_GROUP)], 0)
    m_sc[...] = jnp.full(m_sc.shape, -jnp.inf, F32)
    l_sc[...] = jnp.zeros(l_sc.shape, F32)
    acc_sc[...] = jnp.zeros(acc_sc.shape, F32)
    reps = tk // LANES

    def body(kc, carry):
        k0 = pl.multiple_of(kc * tk, tk)
        k = k_ref[pl.ds(k0, tk), :]
        v = v_ref[pl.ds(k0, tk), :]
        s = _dot_nt(q, k)
        m_old = m_sc[...]
        m_new = jnp.maximum(m_old, jnp.max(s, -1, keepdims=True))
        alpha = jnp.exp2(m_old - m_new)
        p = jnp.exp2(s - jnp.concatenate([m_new] * reps, 1))
        l_sc[...] = alpha * l_sc[...] + jnp.sum(p, -1, keepdims=True)
        acc_sc[...] = alpha * acc_sc[...] + _dot(p.astype(BF16), v)
        m_sc[...] = m_new
        return carry

    lax.fori_loop(0, seq // tk, body, 0)
    out = acc_sc[...] / l_sc[...]
    for h in range(C_GROUP):
        o_ref[:, h * C_HD:(h + 1) * C_HD] = out[h * tq:(h + 1) * tq].astype(o_ref.dtype)


def _gqa(q, k, v, tq=256, tk=512):
    b, s, nq = q.shape
    gw = C_GROUP * C_HD
    rows = C_GROUP * tq
    return pl.pallas_call(
        functools.partial(_gqa_kernel, seq=s, tq=tq, tk=tk),
        grid=(b, C_KV_HEADS, s // tq),
        in_specs=[pl.BlockSpec((None, tq, gw), lambda i, h, j: (i, j, h)),
                  pl.BlockSpec((None, s, C_HD), lambda i, h, j: (i, 0, h)),
                  pl.BlockSpec((None, s, C_HD), lambda i, h, j: (i, 0, h))],
        out_specs=pl.BlockSpec((None, tq, gw), lambda i, h, j: (i, j, h)),
        out_shape=jax.ShapeDtypeStruct((b, s, nq), BF16),
        scratch_shapes=[pltpu.VMEM((rows, C_HD), F32)] * 3,
        compiler_params=_params("parallel", "parallel", "parallel"),
    )(q, k, v)


def _router_kernel(x_ref, w_ref, c_ref):
    logits = jnp.dot(x_ref[...], w_ref[...], preferred_element_type=F32,
                     precision=lax.Precision.HIGHEST)
    lane = lax.broadcasted_iota(jnp.int32, logits.shape, 1)
    logits = jnp.where(lane < N_EXPERTS, logits, -jnp.inf)
    m1 = jnp.max(logits, -1, keepdims=True)
    i1 = jnp.min(jnp.where(logits == m1, lane, LANES), -1, keepdims=True)
    rest = jnp.where(lane == i1, -jnp.inf, logits)
    m2 = jnp.max(rest, -1, keepdims=True)
    i2 = jnp.min(jnp.where(rest == m2, lane, LANES), -1, keepdims=True)
    e = jnp.exp(m2 - m1)
    g1 = 1.0 / (1.0 + e)
    g2 = e / (1.0 + e)
    c_ref[...] = jnp.where(lane == i1, g1, 0.0) + jnp.where(lane == i2, g2, 0.0)


def _router(x, w, tm=512):
    n, dm = x.shape
    wp = jnp.zeros((dm, LANES), F32).at[:, :N_EXPERTS].set(w)
    return pl.pallas_call(
        _router_kernel,
        grid=(n // tm,),
        in_specs=[pl.BlockSpec((tm, dm), lambda i: (i, 0)),
                  pl.BlockSpec((dm, LANES), lambda i: (0, 0))],
        out_specs=pl.BlockSpec((tm, LANES), lambda i: (i, 0)),
        out_shape=jax.ShapeDtypeStruct((n, LANES), F32),
        compiler_params=_params("parallel"),
    )(x, wp)


def _moe_kernel(x_ref, c_ref, w1_ref, w3_ref, w2_ref, g_ref, b_ref, o_ref, acc):
    e = pl.program_id(1)
    j = pl.program_id(2)

    @pl.when((e == 0) & (j == 0))
    def _():
        acc[...] = DN_ALPHA * x_ref[...]

    xb = x_ref[...].astype(BF16)
    a = _dot(xb, w1_ref[...])
    gate = _dot(xb, w3_ref[...])
    c = c_ref[...]
    lane = lax.broadcasted_iota(jnp.int32, c.shape, 1)
    ce = jnp.sum(jnp.where(lane == e, c, 0.0), -1, keepdims=True)
    h = (a * jax.nn.sigmoid(a) * gate * ce).astype(BF16)
    acc[...] += _dot(h, w2_ref[...])

    @pl.when((e == pl.num_programs(1) - 1) & (j == pl.num_programs(2) - 1))
    def _():
        o_ref[...] = _layer_norm(acc[...], g_ref[...], b_ref[...])


def _moe(x, combine, w13, w2, g, b, tm=512, tf=896):
    n, dm = x.shape
    ne, ff, _ = w2.shape
    nf = ff // tf
    return pl.pallas_call(
        _moe_kernel,
        grid=(n // tm, ne, nf),
        in_specs=[pl.BlockSpec((tm, dm), lambda i, e, j: (i, 0)),
                  pl.BlockSpec((tm, LANES), lambda i, e, j: (i, 0)),
                  pl.BlockSpec((None, dm, tf), lambda i, e, j: (e, 0, j)),
                  pl.BlockSpec((None, dm, tf), lambda i, e, j: (e, 0, nf + j)),
                  pl.BlockSpec((None, tf, dm), lambda i, e, j: (e, j, 0)),
                  pl.BlockSpec((1, dm), lambda i, e, j: (0, 0)),
                  pl.BlockSpec((1, dm), lambda i, e, j: (0, 0))],
        out_specs=pl.BlockSpec((tm, dm), lambda i, e, j: (i, 0)),
        out_shape=jax.ShapeDtypeStruct((n, dm), F32),
        scratch_shapes=[pltpu.VMEM((tm, dm), F32)],
        compiler_params=_params("parallel", "arbitrary", "arbitrary"),
    )(x, combine, w13, w13, w2, g.reshape(1, dm), b.reshape(1, dm))


def _trunk(x, mem, w):
    b, s, dm = x.shape
    n = b * s
    mem2 = mem.reshape(-1, dm)
    for i in range(DEPTH):
        j = i // 2
        x2 = x.reshape(n, dm)
        if i % 2 == 0:
            u, qkv = _matmul(x2, w['e_w_in'][j], (2 * CONV_CH, QKV_EVEN), (F32, BF16))
            a_out = _conv_module(u.reshape(b, s, 2 * CONV_CH), w['e_dw_w'][j], w['e_dw_b'][j],
                                 w['e_cn_g'][j], w['e_cn_b'][j])
            b_out = _mixture_of_dilations(qkv.reshape(b, s, QKV_EVEN))
            w_out = w['e_w_out'][j]
            x2 = _proj_ln([a_out.reshape(n, CONV_CH), b_out.reshape(n, B_WIDTH)],
                          [w_out[:CONV_CH], w_out[CONV_CH:]], x2, w['ln_g'][i, 0], w['ln_b'][i, 0])
        else:
            q, k, v = _odd_in_proj(x, w['o_w_in'][j], w['o_q_norm'][j], w['o_k_norm'][j])
            attn = _gqa(q, k, v)
            x2 = _proj_ln([attn.reshape(n, C_HEADS * C_HD)], [w['o_w_out'][j]], x2,
                          w['ln_g'][i, 0], w['ln_b'][i, 0])
        (kv,) = _matmul(mem2, w['x_wkv'][i], (2 * dm,), (BF16,), tm=256)
        x = _xattn(x2.reshape(b, s, dm), kv.reshape(b, -1, 2 * dm), w['x_wq'][i], w['x_wo'][i],
                   w['ln_g'][i, 1], w['ln_b'][i, 1])
        x2 = x.reshape(n, dm)
        if i % 2 == 0:
            x2 = _ffn(x2, w['f_w13'][j], w['f_w2'][j], w['ln_g'][i, 2], w['ln_b'][i, 2])
        else:
            combine = _router(x2, w['m_router'][j])
            x2 = _moe(x2, combine, w['m_w13'][j], w['m_w2'][j], w['ln_g'][i, 2], w['ln_b'][i, 2])
        x = x2.reshape(b, s, dm)
    return x


_MXU_WEIGHTS = ('e_w_in', 'e_w_out', 'f_w13', 'f_w2', 'o_w_in', 'o_w_out', 'm_w13', 'm_w2',
                'x_wq', 'x_wkv', 'x_wo')


def kernel(x_prompt, x_sample, mem_prompt, mem_sample, e_w_in, e_dw_w, e_dw_b, e_cn_g, e_cn_b,
           e_w_out, f_w13, f_w2, o_w_in, o_q_norm, o_k_norm, o_w_out, m_router, m_w13, m_w2,
           x_wq, x_wkv, x_wo, ln_g, ln_b):
    w = dict(e_w_in=e_w_in, e_dw_w=e_dw_w, e_dw_b=e_dw_b, e_cn_g=e_cn_g, e_cn_b=e_cn_b,
             e_w_out=e_w_out, f_w13=f_w13, f_w2=f_w2, o_w_in=o_w_in, o_q_norm=o_q_norm,
             o_k_norm=o_k_norm, o_w_out=o_w_out, m_router=m_router, m_w13=m_w13, m_w2=m_w2,
             x_wq=x_wq, x_wkv=x_wkv, x_wo=x_wo, ln_g=ln_g, ln_b=ln_b)
    for name in _MXU_WEIGHTS:
        w[name] = w[name].astype(BF16)
    return (_trunk(x_prompt, mem_prompt, w), _trunk(x_sample, mem_sample, w))
```

```python
import functools
import math

import numpy as np
import jax
import jax.numpy as jnp
from jax import lax
from jax.experimental import pallas as pl
from jax.experimental.pallas import tpu as pltpu

F32 = jnp.float32
BF16 = jnp.bfloat16

D_MODEL = 1024
DEPTH = 2
GRID_W = 64
CONV_CH = D_MODEL // 2
CONV_WIDTH = 31
B_DILATIONS = (1, 4, 16)
B_GROUPS = len(B_DILATIONS)
B_HEADS = 4
B_HD = 128
B_WIDTH = B_HEADS * B_HD
B_HALF = 64
QKV_EVEN = 3 * B_GROUPS * B_WIDTH
C_HEADS = 8
C_KV_HEADS = 2
C_GROUP = C_HEADS // C_KV_HEADS
C_HD = 128
ROPE_THETA = 10000.0
X_HEADS = 4
X_HD = D_MODEL // X_HEADS
FF_DENSE = 2816
N_EXPERTS = 8
FF_EXPERT = 3584
DN_ALPHA = (2 * DEPTH) ** 0.25
LN_EPS = 1e-5
RMS_EPS = 1e-6
NEG = -1e30

LANES = 128
SUBLANES = 8
V7X_VMEM_LIMIT = 56 * 1024 * 1024


def _params(*semantics):
    return pltpu.CompilerParams(dimension_semantics=semantics,
                                vmem_limit_bytes=V7X_VMEM_LIMIT)


def _layer_norm(y, g, b):
    mu = jnp.mean(y, -1, keepdims=True)
    yc = y - mu
    var = jnp.mean(yc * yc, -1, keepdims=True)
    return yc * lax.rsqrt(var + LN_EPS) * g + b


def _dot(a, b):
    return jnp.dot(a, b, preferred_element_type=F32)


def _dot_nt(a, b):
    return lax.dot_general(a, b, (((1,), (1,)), ((), ())), preferred_element_type=F32)


def _matmul_kernel(x_ref, w_ref, *o_refs, splits, chunk):
    xb = x_ref[...].astype(BF16)
    col = 0
    for o_ref, width in zip(o_refs, splits):
        for c in range(0, width, chunk):
            o_ref[:, c:c + chunk] = _dot(xb, w_ref[:, col + c:col + c + chunk]).astype(o_ref.dtype)
        col += width


def _matmul(x, w, splits, dtypes, name, tm=512, chunk=512):
    n, k = x.shape
    return pl.pallas_call(
        functools.partial(_matmul_kernel, splits=splits, chunk=chunk),
        grid=(n // tm,),
        in_specs=[pl.BlockSpec((tm, k), lambda i: (i, 0)),
                  pl.BlockSpec(w.shape, lambda i: (0, 0))],
        out_specs=[pl.BlockSpec((tm, s), lambda i: (i, 0)) for s in splits],
        out_shape=[jax.ShapeDtypeStruct((n, s), dt) for s, dt in zip(splits, dtypes)],
        compiler_params=_params("parallel"),
        name=name,
    )(x, w)


CONV_HALO = 16
CONV_ROWS = 64


def _conv_kernel(prev_ref, cur_ref, next_ref, w_ref, b_ref, g_ref, be_ref, o_ref, hwin, *, ts):
    s = pl.program_id(1)
    last = pl.num_programs(1) - 1

    def glu(u):
        return u[:, :CONV_CH] * jax.nn.sigmoid(u[:, CONV_CH:])

    hwin[0:CONV_HALO, :] = jnp.where(s > 0, glu(prev_ref[...]), 0.0)
    hwin[CONV_HALO:CONV_HALO + ts, :] = glu(cur_ref[...])
    hwin[CONV_HALO + ts:, :] = jnp.where(s < last, glu(next_ref[...]), 0.0)
    pad = CONV_WIDTH // 2
    for c in range(0, ts, CONV_ROWS):
        acc = jnp.broadcast_to(b_ref[...], (CONV_ROWS, CONV_CH))
        for j in range(CONV_WIDTH):
            start = CONV_HALO + c + j - pad
            acc = acc + w_ref[j:j + 1, :] * hwin[start:start + CONV_ROWS, :]
        y = _layer_norm(acc, g_ref[...], be_ref[...])
        o_ref[c:c + CONV_ROWS, :] = (y * jax.nn.sigmoid(y)).astype(o_ref.dtype)


def _conv_module(u, dw_w, dw_b, cn_g, cn_b, ts=512):
    b, s, _ = u.shape
    hb = ts // CONV_HALO
    nhalo = s // CONV_HALO
    row = lambda a: a.reshape(1, CONV_CH)
    return pl.pallas_call(
        functools.partial(_conv_kernel, ts=ts),
        grid=(b, s // ts),
        in_specs=[
            pl.BlockSpec((None, CONV_HALO, 2 * CONV_CH),
                         lambda i, j: (i, jnp.maximum(j * hb - 1, 0), 0)),
            pl.BlockSpec((None, ts, 2 * CONV_CH), lambda i, j: (i, j, 0)),
            pl.BlockSpec((None, CONV_HALO, 2 * CONV_CH),
                         lambda i, j: (i, jnp.minimum((j + 1) * hb, nhalo - 1), 0)),
            pl.BlockSpec((CONV_WIDTH, CONV_CH), lambda i, j: (0, 0)),
            pl.BlockSpec((1, CONV_CH), lambda i, j: (0, 0)),
            pl.BlockSpec((1, CONV_CH), lambda i, j: (0, 0)),
            pl.BlockSpec((1, CONV_CH), lambda i, j: (0, 0)),
        ],
        out_specs=pl.BlockSpec((None, ts, CONV_CH), lambda i, j: (i, j, 0)),
        out_shape=jax.ShapeDtypeStruct((b, s, CONV_CH), BF16),
        scratch_shapes=[pltpu.VMEM((ts + 2 * CONV_HALO, CONV_CH), F32)],
        compiler_params=_params("parallel", "parallel"),
        name="conv_module",
    )(u, u, u, dw_w, row(dw_b), row(cn_g), row(cn_b))


BAND_Q = 128
DIL_TILE = 2048


def _band_block(q, k, v, slope, band_bias, key_token0, key_step, seq):
    s = _dot_nt(q, k) * (1.0 / math.sqrt(B_HD)) + slope * band_bias
    col = lax.broadcasted_iota(jnp.int32, (1, 2 * BAND_Q), 1)
    token = key_token0 + col * key_step
    in_seq = token.astype(jnp.uint32) < jnp.uint32(seq)
    s = jnp.where(in_seq, s, NEG)
    m = jnp.max(s, -1, keepdims=True)
    p = jnp.exp(s - m)
    l = jnp.sum(p, -1, keepdims=True)
    return _dot(p.astype(BF16), v) / l, m + jnp.log(l)


def _dilated_kernel(slope_ref, *refs, seq):
    q_refs = refs[0:3]
    kv_refs = refs[3:21]
    o_ref = refs[21]
    kw0, vw0, qf1, kw1, vw1, qf2, kw2, vw2, og0, og1, og2, lg0, lg1, lg2 = refs[22:]
    head = pl.program_id(1)
    t0 = pl.program_id(2) * DIL_TILE

    row = lax.broadcasted_iota(jnp.int32, (BAND_Q, 2 * BAND_Q), 0)
    col = lax.broadcasted_iota(jnp.int32, (BAND_Q, 2 * BAND_Q), 1)
    dist = jnp.abs(col - row - B_HALF)
    band_bias = jnp.where(dist <= B_HALF, -dist.astype(F32), NEG)

    def stage(win, group, which, dtype):
        halo = B_HALF * B_DILATIONS[group]
        prev_ref, cur_ref, next_ref = kv_refs[group * 6 + which * 3:group * 6 + which * 3 + 3]
        win[0:halo, :] = prev_ref[...].astype(dtype)
        win[halo:halo + DIL_TILE, :] = cur_ref[...].astype(dtype)
        win[halo + DIL_TILE:, :] = next_ref[...].astype(dtype)

    def slope_of(group):
        return slope_ref[group * B_HEADS + head] * float(B_DILATIONS[group])

    stage(kw0, 0, 0, BF16)
    stage(vw0, 0, 1, BF16)
    slope0 = slope_of(0)

    def body0(i, carry):
        i0 = pl.multiple_of(i * BAND_Q, BAND_Q)
        o, lse = _band_block(q_refs[0][pl.ds(i0, BAND_Q), :], kw0[pl.ds(i0, 2 * BAND_Q), :],
                             vw0[pl.ds(i0, 2 * BAND_Q), :], slope0, band_bias,
                             t0 + i0 - B_HALF, 1, seq)
        og0[pl.ds(i0, BAND_Q), :] = o
        lg0[pl.ds(i0, BAND_Q), :] = jnp.broadcast_to(lse, (BAND_Q, B_HD))
        return carry

    lax.fori_loop(0, DIL_TILE // BAND_Q, body0, 0)

    for group, qf, kw, vw, og, lg in ((1, qf1, kw1, vw1, og1, lg1), (2, qf2, kw2, vw2, og2, lg2)):
        d = B_DILATIONS[group]
        nq = DIL_TILE // d
        qf[...] = q_refs[group][...].astype(F32)
        stage(kw, group, 0, F32)
        stage(vw, group, 1, F32)
        slope = slope_of(group)

        def body(r, carry, d=d, nq=nq, qf=qf, kw=kw, vw=vw, og=og, lg=lg, slope=slope):
            qr = qf[pl.ds(r, nq, stride=d), :].astype(BF16)
            kr = kw[pl.ds(r, nq + 2 * B_HALF, stride=d), :].astype(BF16)
            vr = vw[pl.ds(r, nq + 2 * B_HALF, stride=d), :].astype(BF16)
            for i0 in range(0, nq, BAND_Q):
                o, lse = _band_block(qr[i0:i0 + BAND_Q], kr[i0:i0 + 2 * BAND_Q],
                                     vr[i0:i0 + 2 * BAND_Q], slope, band_bias,
                                     t0 + r + (i0 - B_HALF) * d, d, seq)
                og[pl.ds(r + i0 * d, BAND_Q, stride=d), :] = o
                lg[pl.ds(r + i0 * d, BAND_Q, stride=d), :] = jnp.broadcast_to(lse, (BAND_Q, B_HD))
            return carry

        lax.fori_loop(0, d, body, 0)

    l0, l1, l2 = lg0[...], lg1[...], lg2[...]
    top = jnp.maximum(jnp.maximum(l0, l1), l2)
    w0, w1, w2 = jnp.exp(l0 - top), jnp.exp(l1 - top), jnp.exp(l2 - top)
    mix = (w0 * og0[...] + w1 * og1[...] + w2 * og2[...]) / (w0 + w1 + w2)
    o_ref[...] = mix.astype(o_ref.dtype)


def _mixture_of_dilations(qkv):
    b, s, _ = qkv.shape
    n = B_GROUPS * B_HEADS
    slopes = jnp.asarray(np.array([2.0 ** (-8.0 * (i + 1) / n) for i in range(n)], np.float32))
    n_tiles = s // DIL_TILE

    def column(which, group):
        return (which * B_GROUPS + group) * B_HEADS

    def tile_spec(col):
        return pl.BlockSpec((None, DIL_TILE, B_HD), lambda i, h, t: (i, t, col + h))

    def halo_specs(col, group):
        halo = B_HALF * B_DILATIONS[group]
        per_tile = DIL_TILE // halo
        last = s // halo - 1
        return [pl.BlockSpec((None, halo, B_HD),
                             lambda i, h, t: (i, jnp.maximum(t * per_tile - 1, 0), col + h)),
                tile_spec(col),
                pl.BlockSpec((None, halo, B_HD),
                             lambda i, h, t: (i, jnp.minimum((t + 1) * per_tile, last), col + h))]

    in_specs = [pl.BlockSpec(memory_space=pltpu.SMEM)]
    in_specs += [tile_spec(column(0, g)) for g in range(B_GROUPS)]
    for g in range(B_GROUPS):
        in_specs += halo_specs(column(1, g), g) + halo_specs(column(2, g), g)
    scratch = []
    for g, d in enumerate(B_DILATIONS):
        win = (DIL_TILE + 2 * B_HALF * d, B_HD)
        if g == 0:
            scratch += [pltpu.VMEM(win, BF16)] * 2
        else:
            scratch += [pltpu.VMEM((DIL_TILE, B_HD), F32), pltpu.VMEM(win, F32), pltpu.VMEM(win, F32)]
    scratch += [pltpu.VMEM((DIL_TILE, B_HD), F32)] * (2 * B_GROUPS)
    return pl.pallas_call(
        functools.partial(_dilated_kernel, seq=s),
        grid=(b, B_HEADS, n_tiles),
        in_specs=in_specs,
        out_specs=pl.BlockSpec((None, DIL_TILE, B_HD), lambda i, h, t: (i, t, h)),
        out_shape=jax.ShapeDtypeStruct((b, s, B_WIDTH), BF16),
        scratch_shapes=scratch,
        compiler_params=_params("parallel", "parallel", "parallel"),
        name="dilated_mixture_attn",
    )(slopes, *([qkv] * (1 + 3 + 6 * B_GROUPS - 1)))


def _proj_ln_kernel(*refs, n_in):
    ins = refs[:n_in]
    ws = refs[n_in:2 * n_in]
    x_ref, g_ref, b_ref, o_ref = refs[2 * n_in:]
    y = DN_ALPHA * x_ref[...]
    for a_ref, w_ref in zip(ins, ws):
        y = y + _dot(a_ref[...], w_ref[...])
    o_ref[...] = _layer_norm(y, g_ref[...], b_ref[...])


def _proj_ln(ins, ws, x, g, b, tm=512):
    n, dm = x.shape
    n_in = len(ins)
    in_specs = ([pl.BlockSpec((tm, a.shape[1]), lambda i: (i, 0)) for a in ins]
                + [pl.BlockSpec(w.shape, lambda i: (0, 0)) for w in ws]
                + [pl.BlockSpec((tm, dm), lambda i: (i, 0)),
                   pl.BlockSpec((1, dm), lambda i: (0, 0)),
                   pl.BlockSpec((1, dm), lambda i: (0, 0))])
    return pl.pallas_call(
        functools.partial(_proj_ln_kernel, n_in=n_in),
        grid=(n // tm,),
        in_specs=in_specs,
        out_specs=pl.BlockSpec((tm, dm), lambda i: (i, 0)),
        out_shape=jax.ShapeDtypeStruct((n, dm), F32),
        compiler_params=_params("parallel"),
        name="mixer_out_proj_ln",
    )(*ins, *ws, x, g.reshape(1, dm), b.reshape(1, dm))


def _xattn_kernel(x_ref, k_ref, v_ref, wq_ref, wo_ref, g_ref, b_ref, o_ref, *maybe_ob_ref):
    x = x_ref[...]
    q = (_dot(x.astype(BF16), wq_ref[...]) * (1.0 / math.sqrt(X_HD))).astype(BF16)
    heads = []
    for h in range(X_HEADS):
        sl = slice(h * X_HD, (h + 1) * X_HD)
        s = _dot_nt(q[:, sl], k_ref[:, sl])
        m = jnp.max(s, -1, keepdims=True)
        p = jnp.exp(s - m)
        l = jnp.sum(p, -1, keepdims=True)
        heads.append((_dot(p.astype(BF16), v_ref[:, sl]) / l).astype(BF16))
    o = jnp.concatenate(heads, -1)
    y = _layer_norm(DN_ALPHA * x + _dot(o, wo_ref[...]), g_ref[...], b_ref[...])
    o_ref[...] = y
    for ob_ref in maybe_ob_ref:
        ob_ref[...] = y.astype(BF16)


def _xattn(x, kv, wq, wo, g, b, with_bf16_copy, tm=512):
    bsz, s, dm = x.shape
    m = kv.shape[1]
    row_spec = pl.BlockSpec((None, tm, dm), lambda i, j: (i, j, 0))
    out_specs = [row_spec]
    out_shape = [jax.ShapeDtypeStruct((bsz, s, dm), F32)]
    if with_bf16_copy:
        out_specs.append(row_spec)
        out_shape.append(jax.ShapeDtypeStruct((bsz, s, dm), BF16))
    return pl.pallas_call(
        _xattn_kernel,
        grid=(bsz, s // tm),
        in_specs=[row_spec,
                  pl.BlockSpec((None, m, dm), lambda i, j: (i, 0, 0)),
                  pl.BlockSpec((None, m, dm), lambda i, j: (i, 0, 1)),
                  pl.BlockSpec((dm, dm), lambda i, j: (0, 0)),
                  pl.BlockSpec((dm, dm), lambda i, j: (0, 0)),
                  pl.BlockSpec((1, dm), lambda i, j: (0, 0)),
                  pl.BlockSpec((1, dm), lambda i, j: (0, 0))],
        out_specs=out_specs,
        out_shape=out_shape,
        compiler_params=_params("parallel", "parallel"),
        name="memory_xattn_ln",
    )(x, kv, kv, wq, wo, g.reshape(1, dm), b.reshape(1, dm))


def _ffn_kernel(x_ref, w1_ref, w3_ref, w2_ref, g_ref, b_ref, o_ref, acc):
    j = pl.program_id(1)

    @pl.when(j == 0)
    def _():
        acc[...] = DN_ALPHA * x_ref[...]

    xb = x_ref[...].astype(BF16)
    a = _dot(xb, w1_ref[...])
    gate = _dot(xb, w3_ref[...])
    h = (a * jax.nn.sigmoid(a) * gate).astype(BF16)
    acc[...] += _dot(h, w2_ref[...])

    @pl.when(j == pl.num_programs(1) - 1)
    def _():
        o_ref[...] = _layer_norm(acc[...], g_ref[...], b_ref[...])


def _ffn(x, w13, w2, g, b, tm=512, tf=1408):
    n, dm = x.shape
    ff = w2.shape[0]
    nf = ff // tf
    return pl.pallas_call(
        _ffn_kernel,
        grid=(n // tm, nf),
        in_specs=[pl.BlockSpec((tm, dm), lambda i, j: (i, 0)),
                  pl.BlockSpec((dm, tf), lambda i, j: (0, j)),
                  pl.BlockSpec((dm, tf), lambda i, j: (0, nf + j)),
                  pl.BlockSpec((tf, dm), lambda i, j: (j, 0)),
                  pl.BlockSpec((1, dm), lambda i, j: (0, 0)),
                  pl.BlockSpec((1, dm), lambda i, j: (0, 0))],
        out_specs=pl.BlockSpec((tm, dm), lambda i, j: (i, 0)),
        out_shape=jax.ShapeDtypeStruct((n, dm), F32),
        scratch_shapes=[pltpu.VMEM((tm, dm), F32)],
        compiler_params=_params("parallel", "arbitrary"),
        name="dense_swiglu_ln",
    )(x, w13, w13, w2, g.reshape(1, dm), b.reshape(1, dm))


def _rope_tables(seq_len):
    rows = seq_len // GRID_W
    row = jnp.broadcast_to(jnp.arange(rows, dtype=F32)[:, None], (rows, GRID_W)).reshape(-1)
    col = jnp.broadcast_to(jnp.arange(GRID_W, dtype=F32)[None, :], (rows, GRID_W)).reshape(-1)
    axis_dim = C_HD // 2
    freqs = ROPE_THETA ** (-jnp.arange(0, axis_dim, 2, dtype=F32) / axis_dim)
    ang_r = row[:, None] * freqs[None, :]
    ang_c = col[:, None] * freqs[None, :]
    ang = jnp.concatenate([ang_r, ang_r, ang_c, ang_c], -1)
    cos, sin = jnp.cos(ang), jnp.sin(ang)
    first = (jnp.arange(C_HD) % axis_dim) < axis_dim // 2
    sin_up = jnp.where(first[None, :], -sin, 0.0)
    sin_dn = jnp.where(first[None, :], 0.0, sin)
    return cos, sin_up, sin_dn


def _odd_in_kernel(x_ref, w_ref, qn_ref, kn_ref, cos_ref, su_ref, sd_ref, q_ref, k_ref, v_ref):
    xb = x_ref[...].astype(BF16)
    cos, su, sd = cos_ref[...], su_ref[...], sd_ref[...]
    quarter = C_HD // 4
    q_scale = math.log2(math.e) / math.sqrt(C_HD)

    def norm_rope(h, gain):
        h = h * lax.rsqrt(jnp.mean(h * h, -1, keepdims=True) + RMS_EPS) * gain
        up = pltpu.roll(h, C_HD - quarter, 1)
        dn = pltpu.roll(h, quarter, 1)
        return h * cos + up * su + dn * sd

    nq, nk = C_HEADS * C_HD, C_KV_HEADS * C_HD
    for i in range(C_HEADS):
        h = _dot(xb, w_ref[:, i * C_HD:(i + 1) * C_HD])
        q_ref[:, i * C_HD:(i + 1) * C_HD] = (norm_rope(h, qn_ref[...]) * q_scale).astype(BF16)
    for i in range(C_KV_HEADS):
        h = _dot(xb, w_ref[:, nq + i * C_HD:nq + (i + 1) * C_HD])
        k_ref[:, i * C_HD:(i + 1) * C_HD] = norm_rope(h, kn_ref[...]).astype(BF16)
    v_ref[...] = _dot(xb, w_ref[:, nq + nk:]).astype(BF16)


def _odd_in_proj(x, w, qn, kn, tm=512):
    b, s, dm = x.shape
    nq, nk = C_HEADS * C_HD, C_KV_HEADS * C_HD
    cos, su, sd = _rope_tables(s)
    tab = pl.BlockSpec((tm, C_HD), lambda i, j: (j, 0))
    vec = pl.BlockSpec((1, C_HD), lambda i, j: (0, 0))
    return pl.pallas_call(
        _odd_in_kernel,
        grid=(b, s // tm),
        in_specs=[pl.BlockSpec((None, tm, dm), lambda i, j: (i, j, 0)),
                  pl.BlockSpec(w.shape, lambda i, j: (0, 0)),
                  vec, vec, tab, tab, tab],
        out_specs=[pl.BlockSpec((None, tm, nq), lambda i, j: (i, j, 0)),
                   pl.BlockSpec((None, tm, nk), lambda i, j: (i, j, 0)),
                   pl.BlockSpec((None, tm, nk), lambda i, j: (i, j, 0))],
        out_shape=[jax.ShapeDtypeStruct((b, s, nq), BF16),
                   jax.ShapeDtypeStruct((b, s, nk), BF16),
                   jax.ShapeDtypeStruct((b, s, nk), BF16)],
        compiler_params=_params("parallel", "parallel"),
        name="gqa_in_proj_norm_rope",
    )(x, w, qn.reshape(1, C_HD), kn.reshape(1, C_HD), cos, su, sd)


def _gqa_kernel(q_ref, k_ref, v_ref, o_ref, m_sc, l_sc, acc_sc, *, seq, tq, tk):
    q = jnp.concatenate([q_ref[:, h * C_HD:(h + 1) * C_HD] for h in range(C_GROUP)], 0)
    m_sc[...] = jnp.full(m_sc.shape, -jnp.inf, F32)
    l_sc[...] = jnp.zeros(l_sc.shape, F32)
    acc_sc[...] = jnp.zeros(acc_sc.shape, F32)
    reps = tk // LANES

    def body(kc, carry):
        k0 = pl.multiple_of(kc * tk, tk)
        k = k_ref[pl.ds(k0, tk), :]
        v = v_ref[pl.ds(k0, tk), :]
        s = _dot_nt(q, k)
        m_old = m_sc[...]
        m_new = jnp.maximum(m_old, jnp.max(s, -1, keepdims=True))
        alpha = jnp.exp2(m_old - m_new)
        p = jnp.exp2(s - jnp.concatenate([m_new] * reps, 1))
        l_sc[...] = alpha * l_sc[...] + jnp.sum(p, -1, keepdims=True)
        acc_sc[...] = alpha * acc_sc[...] + _dot(p.astype(BF16), v)
        m_sc[...] = m_new
        return carry

    lax.fori_loop(0, seq // tk, body, 0, unroll=2)
    out = acc_sc[...] / l_sc[...]
    for h in range(C_GROUP):
        o_ref[:, h * C_HD:(h + 1) * C_HD] = out[h * tq:(h + 1) * tq].astype(o_ref.dtype)


def _gqa(q, k, v, tq=256, tk=512):
    b, s, nq = q.shape
    gw = C_GROUP * C_HD
    rows = C_GROUP * tq
    return pl.pallas_call(
        functools.partial(_gqa_kernel, seq=s, tq=tq, tk=tk),
        grid=(b, C_KV_HEADS, s // tq),
        in_specs=[pl.BlockSpec((None, tq, gw), lambda i, h, j: (i, j, h)),
                  pl.BlockSpec((None, s, C_HD), lambda i, h, j: (i, 0, h)),
                  pl.BlockSpec((None, s, C_HD), lambda i, h, j: (i, 0, h))],
        out_specs=pl.BlockSpec((None, tq, gw), lambda i, h, j: (i, j, h)),
        out_shape=jax.ShapeDtypeStruct((b, s, nq), BF16),
        scratch_shapes=[pltpu.VMEM((rows, C_HD), F32)] * 3,
        compiler_params=_params("parallel", "parallel", "parallel"),
        name="gqa_flash",
    )(q, k, v)


MOE_CHUNK = 512
MOE_TILE = 512
MOE_GROUP = 1024
UNSELECTED = -float(2 ** 26)


def _router_kernel(x_ref, w_ref, gate_ref, sel_ref):
    logits = jnp.dot(x_ref[...], w_ref[...], preferred_element_type=F32,
                     precision=lax.Precision.HIGHEST)
    lane = lax.broadcasted_iota(jnp.int32, logits.shape, 1)
    logits = jnp.where(lane < N_EXPERTS, logits, -jnp.inf)
    m1 = jnp.max(logits, -1, keepdims=True)
    i1 = jnp.min(jnp.where(logits == m1, lane, LANES), -1, keepdims=True)
    rest = jnp.where(lane == i1, -jnp.inf, logits)
    m2 = jnp.max(rest, -1, keepdims=True)
    i2 = jnp.min(jnp.where(rest == m2, lane, LANES), -1, keepdims=True)
    e = jnp.exp(m2 - m1)
    g1 = 1.0 / (1.0 + e)
    g2 = e / (1.0 + e)
    gate_ref[...] = jnp.where(lane == i1, g1, 0.0) + jnp.where(lane == i2, g2, 0.0)
    sel_ref[...] = jnp.where((lane == i1) | (lane == i2), 1.0, 0.0).astype(BF16)


def _router(x, w):
    n, dm = x.shape
    tm = MOE_CHUNK
    wp = jnp.zeros((dm, LANES), F32).at[:, :N_EXPERTS].set(w)
    return pl.pallas_call(
        _router_kernel,
        grid=(n // tm,),
        in_specs=[pl.BlockSpec((tm, dm), lambda i: (i, 0)),
                  pl.BlockSpec((dm, LANES), lambda i: (0, 0))],
        out_specs=[pl.BlockSpec((tm, LANES), lambda i: (i, 0))] * 2,
        out_shape=[jax.ShapeDtypeStruct((n, LANES), F32), jax.ShapeDtypeStruct((n, LANES), BF16)],
        compiler_params=_params("parallel"),
        name="moe_router_top2",
    )(x, wp)


def _rank_kernel(sel_ref, rank_ref, rank_t_ref, base_ref, cnt_ref, carry):
    @pl.when(pl.program_id(0) == 0)
    def _():
        carry[...] = jnp.zeros(carry.shape, F32)

    sel = sel_ref[...]
    n = sel.shape[0]
    row = lax.broadcasted_iota(jnp.int32, (n, n), 0)
    col = lax.broadcasted_iota(jnp.int32, (n, n), 1)
    earlier = jnp.where(row > col, 1.0, 0.0).astype(BF16)
    before = _dot(earlier, sel) + carry[...]
    chosen = sel.astype(F32)
    rank = jnp.where(chosen > 0.0, before, UNSELECTED)
    rank_ref[...] = rank
    rank_t_ref[...] = rank.T[:SUBLANES, :]
    cnt = jnp.sum(chosen, 0, keepdims=True)
    base_ref[...] = carry[...]
    cnt_ref[...] = cnt
    carry[...] += cnt


def _rank(sel):
    n = sel.shape[0]
    c = n // MOE_CHUNK
    stat = pl.BlockSpec((None, 1, LANES), lambda i: (i, 0, 0))
    return pl.pallas_call(
        _rank_kernel,
        grid=(c,),
        in_specs=[pl.BlockSpec((MOE_CHUNK, LANES), lambda i: (i, 0))],
        out_specs=[pl.BlockSpec((MOE_CHUNK, LANES), lambda i: (i, 0)),
                   pl.BlockSpec((SUBLANES, MOE_CHUNK), lambda i: (0, i)),
                   stat, stat],
        out_shape=[jax.ShapeDtypeStruct((n, LANES), F32),
                   jax.ShapeDtypeStruct((SUBLANES, n), F32),
                   jax.ShapeDtypeStruct((c, 1, LANES), F32),
                   jax.ShapeDtypeStruct((c, 1, LANES), F32)],
        scratch_shapes=[pltpu.VMEM((1, LANES), F32)],
        compiler_params=_params("arbitrary"),
        name="moe_rank",
    )(sel)


def _fill_forward(values, valid):
    idx = lax.cummax(jnp.where(valid, jnp.arange(values.shape[0]), -1))
    idx = jnp.where(idx < 0, jnp.argmax(valid), idx)
    return values[idx]


def _moe_plan(base, cnt, n_group_tiles):
    counts = base[-1] + cnt[-1]
    gsize = -(-counts // MOE_GROUP) * MOE_GROUP
    gend = jnp.cumsum(gsize)
    gstart = gend - gsize
    tile_expert = jnp.minimum(
        jnp.searchsorted(gend, jnp.arange(n_group_tiles) * MOE_GROUP, side='right'),
        N_EXPERTS - 1).astype(jnp.int32)
    n_tiles = n_group_tiles * (MOE_GROUP // MOE_TILE)
    first_slot = gstart[None, :] + base
    t0 = first_slot // MOE_TILE
    t1 = (first_slot + cnt - 1) // MOE_TILE
    v0 = cnt > 0
    v1 = v0 & (t1 > t0)
    td = jnp.stack([t0, t1], -1).transpose(1, 0, 2).reshape(-1)
    vd = jnp.stack([v0, v1], -1).transpose(1, 0, 2).reshape(-1)
    seen = lax.cummax(jnp.where(vd, td, -1))
    prev = jnp.concatenate([jnp.full((1,), -1, seen.dtype), seen[:-1]])
    first = vd & (td != prev)
    clip = lambda t: jnp.clip(t, 0, n_tiles - 1).astype(jnp.int32)
    as_i32 = lambda t: t.astype(jnp.int32)
    dispatch = (clip(_fill_forward(td, vd)), as_i32(vd), as_i32(first))
    t0f, v0f, t1f, v1f = t0.reshape(-1), v0.reshape(-1), t1.reshape(-1), v1.reshape(-1)
    combine = (clip(_fill_forward(t0f, v0f)), as_i32(v0f), clip(_fill_forward(t1f, v1f)), as_i32(v1f))
    return gstart, tile_expert, dispatch, combine


def _dispatch_kernel(tile_ref, valid_ref, first_ref, slot_ref, x_ref, zero_ref, o_ref, *, n_chunks):
    del zero_ref
    e = pl.program_id(0)
    w = (e * n_chunks + pl.program_id(1)) * 2 + pl.program_id(2)

    @pl.when(valid_ref[w] == 1)
    def _():
        local = slot_ref[pl.ds(e, 1), :] - (tile_ref[w] * MOE_TILE).astype(F32)
        sub = lax.broadcasted_iota(jnp.int32, (MOE_TILE, MOE_CHUNK), 0).astype(F32)
        onehot = jnp.where(sub == local, 1.0, 0.0).astype(BF16)
        rows = _dot(onehot, x_ref[...])

        @pl.when(first_ref[w] == 1)
        def _():
            o_ref[...] = rows.astype(BF16)

        @pl.when(first_ref[w] == 0)
        def _():
            o_ref[...] = (o_ref[...].astype(F32) + rows).astype(BF16)


def _dispatch(plan, slot_t, xb, n_slots):
    n, dm = xb.shape
    c = n // MOE_CHUNK
    grid_spec = pltpu.PrefetchScalarGridSpec(
        num_scalar_prefetch=3,
        grid=(N_EXPERTS, c, 2),
        in_specs=[pl.BlockSpec((SUBLANES, MOE_CHUNK), lambda e, i, k, *_: (0, i)),
                  pl.BlockSpec((MOE_CHUNK, dm), lambda e, i, k, *_: (i, 0)),
                  pl.BlockSpec(memory_space=pl.ANY)],
        out_specs=pl.BlockSpec((MOE_TILE, dm),
                               lambda e, i, k, tile, valid, first: (tile[(e * c + i) * 2 + k], 0)),
    )
    return pl.pallas_call(
        functools.partial(_dispatch_kernel, n_chunks=c),
        grid_spec=grid_spec,
        out_shape=jax.ShapeDtypeStruct((n_slots, dm), BF16),
        input_output_aliases={5: 0},
        compiler_params=_params("arbitrary", "arbitrary", "arbitrary"),
        name="moe_dispatch",
    )(*plan, slot_t, xb, jnp.zeros((n_slots, dm), BF16))


def _experts_kernel(expert_ref, x_ref, w1_ref, w3_ref, w2_ref, o_ref, acc):
    del expert_ref
    j = pl.program_id(1)
    xb = x_ref[...]
    a = _dot(xb, w1_ref[...])
    gate = _dot(xb, w3_ref[...])
    y = _dot((a * jax.nn.sigmoid(a) * gate).astype(BF16), w2_ref[...])

    @pl.when(j == 0)
    def _():
        acc[...] = y

    @pl.when(j > 0)
    def _():
        acc[...] += y

    @pl.when(j == pl.num_programs(1) - 1)
    def _():
        o_ref[...] = acc[...].astype(BF16)


def _experts(tile_expert, xs, w13, w2, tf=896):
    n_slots, dm = xs.shape
    ff = w2.shape[1]
    nf = ff // tf
    grid_spec = pltpu.PrefetchScalarGridSpec(
        num_scalar_prefetch=1,
        grid=(n_slots // MOE_GROUP, nf),
        in_specs=[pl.BlockSpec((MOE_GROUP, dm), lambda i, j, te: (i, 0)),
                  pl.BlockSpec((None, dm, tf), lambda i, j, te: (te[i], 0, j)),
                  pl.BlockSpec((None, dm, tf), lambda i, j, te: (te[i], 0, nf + j)),
                  pl.BlockSpec((None, tf, dm), lambda i, j, te: (te[i], j, 0))],
        out_specs=pl.BlockSpec((MOE_GROUP, dm), lambda i, j, te: (i, 0)),
        scratch_shapes=[pltpu.VMEM((MOE_GROUP, dm), F32)],
    )
    return pl.pallas_call(
        _experts_kernel,
        grid_spec=grid_spec,
        out_shape=jax.ShapeDtypeStruct((n_slots, dm), BF16),
        compiler_params=_params("parallel", "arbitrary"),
        name="moe_experts_swiglu",
    )(tile_expert, xs, w13, w13, w2)


def _combine_kernel(t0_ref, v0_ref, t1_ref, v1_ref, slot_ref, gate_ref, ya_ref, yb_ref, x_ref,
                    g_ref, b_ref, o_ref, acc):
    e = pl.program_id(1)
    w = pl.program_id(0) * N_EXPERTS + e

    @pl.when(e == 0)
    def _():
        acc[...] = DN_ALPHA * x_ref[...]

    def add_rows(tile, y_ref):
        lane = lax.broadcasted_iota(jnp.int32, (MOE_CHUNK, LANES), 1)
        slot = jnp.sum(jnp.where(lane == e, slot_ref[...], 0.0), -1, keepdims=True)
        gate = jnp.sum(jnp.where(lane == e, gate_ref[...], 0.0), -1, keepdims=True)
        col = lax.broadcasted_iota(jnp.int32, (MOE_CHUNK, MOE_TILE), 1).astype(F32)
        onehot = jnp.where(col == slot - (tile * MOE_TILE).astype(F32), 1.0, 0.0).astype(BF16)
        acc[...] += gate * _dot(onehot, y_ref[...])

    @pl.when(v0_ref[w] == 1)
    def _():
        add_rows(t0_ref[w], ya_ref)

    @pl.when(v1_ref[w] == 1)
    def _():
        add_rows(t1_ref[w], yb_ref)

    @pl.when(e == N_EXPERTS - 1)
    def _():
        o_ref[...] = _layer_norm(acc[...], g_ref[...], b_ref[...])


def _combine(plan, slot, gates, ys, x, g, b):
    n, dm = x.shape
    c = n // MOE_CHUNK
    tok = lambda width: pl.BlockSpec((MOE_CHUNK, width), lambda i, e, *_: (i, 0))
    vec = pl.BlockSpec((1, dm), lambda i, e, *_: (0, 0))
    grid_spec = pltpu.PrefetchScalarGridSpec(
        num_scalar_prefetch=4,
        grid=(c, N_EXPERTS),
        in_specs=[tok(LANES), tok(LANES),
                  pl.BlockSpec((MOE_TILE, dm), lambda i, e, t0, v0, t1, v1: (t0[i * N_EXPERTS + e], 0)),
                  pl.BlockSpec((MOE_TILE, dm), lambda i, e, t0, v0, t1, v1: (t1[i * N_EXPERTS + e], 0)),
                  tok(dm), vec, vec],
        out_specs=tok(dm),
        scratch_shapes=[pltpu.VMEM((MOE_CHUNK, dm), F32)],
    )
    return pl.pallas_call(
        _combine_kernel,
        grid_spec=grid_spec,
        out_shape=jax.ShapeDtypeStruct((n, dm), F32),
        compiler_params=_params("parallel", "arbitrary"),
        name="moe_combine_ln",
    )(*plan, slot, gates, ys, ys, x, g.reshape(1, dm), b.reshape(1, dm))


def _moe(x, xb, router_w, w13, w2, g, b):
    n, dm = x.shape
    n_slots = TOP_K * n + N_EXPERTS * MOE_GROUP
    gates, sel = _router(x, router_w)
    rank, rank_t, base, cnt = _rank(sel)
    as_counts = lambda t: t[:, 0, :N_EXPERTS].astype(jnp.int32)
    gstart, tile_expert, dispatch_plan, combine_plan = _moe_plan(
        as_counts(base), as_counts(cnt), n_slots // MOE_GROUP)
    gstart = gstart.astype(F32)
    slot = rank + jnp.zeros((LANES,), F32).at[:N_EXPERTS].set(gstart)[None, :]
    slot_t = rank_t + gstart[:, None]
    xs = _dispatch(dispatch_plan, slot_t, xb, n_slots)
    ys = _experts(tile_expert, xs, w13, w2)
    return _combine(combine_plan, slot, gates, ys, x, g, b)


TOP_K = 2


def _trunk(x, mem, w):
    b, s, dm = x.shape
    n = b * s
    mem2 = mem.reshape(-1, dm)
    for i in range(DEPTH):
        j = i // 2
        even = i % 2 == 0
        x2 = x.reshape(n, dm)
        if even:
            u, qkv = _matmul(x2, w['e_w_in'][j], (2 * CONV_CH, QKV_EVEN), (F32, BF16), "even_in_proj")
            a_out = _conv_module(u.reshape(b, s, 2 * CONV_CH), w['e_dw_w'][j], w['e_dw_b'][j],
                                 w['e_cn_g'][j], w['e_cn_b'][j])
            b_out = _mixture_of_dilations(qkv.reshape(b, s, QKV_EVEN))
            w_out = w['e_w_out'][j]
            x2 = _proj_ln([a_out.reshape(n, CONV_CH), b_out.reshape(n, B_WIDTH)],
                          [w_out[:CONV_CH], w_out[CONV_CH:]], x2, w['ln_g'][i, 0], w['ln_b'][i, 0])
        else:
            q, k, v = _odd_in_proj(x, w['o_w_in'][j], w['o_q_norm'][j], w['o_k_norm'][j])
            attn = _gqa(q, k, v)
            x2 = _proj_ln([attn.reshape(n, C_HEADS * C_HD)], [w['o_w_out'][j]], x2,
                          w['ln_g'][i, 0], w['ln_b'][i, 0])
        (kv,) = _matmul(mem2, w['x_wkv'][i], (2 * dm,), (BF16,), "memory_kv_proj", tm=256)
        outs = _xattn(x2.reshape(b, s, dm), kv.reshape(b, -1, 2 * dm), w['x_wq'][i], w['x_wo'][i],
                      w['ln_g'][i, 1], w['ln_b'][i, 1], with_bf16_copy=not even)
        x2 = outs[0].reshape(n, dm)
        if even:
            x2 = _ffn(x2, w['f_w13'][j], w['f_w2'][j], w['ln_g'][i, 2], w['ln_b'][i, 2])
        else:
            x2 = _moe(x2, outs[1].reshape(n, dm), w['m_router'][j], w['m_w13'][j], w['m_w2'][j],
                      w['ln_g'][i, 2], w['ln_b'][i, 2])
        x = x2.reshape(b, s, dm)
    return x


_MXU_WEIGHTS = ('e_w_in', 'e_w_out', 'f_w13', 'f_w2', 'o_w_in', 'o_w_out', 'm_w13', 'm_w2',
                'x_wq', 'x_wkv', 'x_wo')


def kernel(x_prompt, x_sample, mem_prompt, mem_sample, e_w_in, e_dw_w, e_dw_b, e_cn_g, e_cn_b,
           e_w_out, f_w13, f_w2, o_w_in, o_q_norm, o_k_norm, o_w_out, m_router, m_w13, m_w2,
           x_wq, x_wkv, x_wo, ln_g, ln_b):
    w = dict(e_w_in=e_w_in, e_dw_w=e_dw_w, e_dw_b=e_dw_b, e_cn_g=e_cn_g, e_cn_b=e_cn_b,
             e_w_out=e_w_out, f_w13=f_w13, f_w2=f_w2, o_w_in=o_w_in, o_q_norm=o_q_norm,
             o_k_norm=o_k_norm, o_w_out=o_w_out, m_router=m_router, m_w13=m_w13, m_w2=m_w2,
             x_wq=x_wq, x_wkv=x_wkv, x_wo=x_wo, ln_g=ln_g, ln_b=ln_b)
    for name in _MXU_WEIGHTS:
        w[name] = w[name].astype(BF16)
    return (_trunk(x_prompt, mem_prompt, w), _trunk(x_sample, mem_sample, w))
```

```python
import functools
import math

import numpy as np
import jax
import jax.numpy as jnp
from jax import lax
from jax.experimental import pallas as pl
from jax.experimental.pallas import tpu as pltpu

F32 = jnp.float32
BF16 = jnp.bfloat16

D_MODEL = 1024
DEPTH = 2
GRID_W = 64
CONV_CH = D_MODEL // 2
CONV_WIDTH = 31
B_DILATIONS = (1, 4, 16)
B_GROUPS = len(B_DILATIONS)
B_HEADS = 4
B_HD = 128
B_WIDTH = B_HEADS * B_HD
B_HALF = 64
QKV_EVEN = 3 * B_GROUPS * B_WIDTH
C_HEADS = 8
C_KV_HEADS = 2
C_GROUP = C_HEADS // C_KV_HEADS
C_HD = 128
ROPE_THETA = 10000.0
X_HEADS = 4
X_HD = D_MODEL // X_HEADS
FF_DENSE = 2816
N_EXPERTS = 8
FF_EXPERT = 3584
DN_ALPHA = (2 * DEPTH) ** 0.25
LN_EPS = 1e-5
RMS_EPS = 1e-6
NEG = -1e30

LANES = 128
SUBLANES = 8
V7X_VMEM_LIMIT = 56 * 1024 * 1024


def _params(*semantics):
    return pltpu.CompilerParams(dimension_semantics=semantics,
                                vmem_limit_bytes=V7X_VMEM_LIMIT)


def _layer_norm(y, g, b):
    mu = jnp.mean(y, -1, keepdims=True)
    yc = y - mu
    var = jnp.mean(yc * yc, -1, keepdims=True)
    return yc * lax.rsqrt(var + LN_EPS) * g + b


def _dot(a, b):
    return jnp.dot(a, b, preferred_element_type=F32)


def _dot_nt(a, b):
    return lax.dot_general(a, b, (((1,), (1,)), ((), ())), preferred_element_type=F32)


def _matmul_kernel(x_ref, w_ref, *o_refs, splits, chunk):
    xb = x_ref[...].astype(BF16)
    col = 0
    for o_ref, width in zip(o_refs, splits):
        for c in range(0, width, chunk):
            o_ref[:, c:c + chunk] = _dot(xb, w_ref[:, col + c:col + c + chunk]).astype(o_ref.dtype)
        col += width


def _matmul(x, w, splits, dtypes, name, tm=512, chunk=512):
    n, k = x.shape
    return pl.pallas_call(
        functools.partial(_matmul_kernel, splits=splits, chunk=chunk),
        grid=(n // tm,),
        in_specs=[pl.BlockSpec((tm, k), lambda i: (i, 0)),
                  pl.BlockSpec(w.shape, lambda i: (0, 0))],
        out_specs=[pl.BlockSpec((tm, s), lambda i: (i, 0)) for s in splits],
        out_shape=[jax.ShapeDtypeStruct((n, s), dt) for s, dt in zip(splits, dtypes)],
        compiler_params=_params("parallel"),
        name=name,
    )(x, w)


CONV_HALO = 16
CONV_ROWS = 64


def _conv_kernel(prev_ref, cur_ref, next_ref, w_ref, b_ref, g_ref, be_ref, o_ref, hwin, *, ts):
    s = pl.program_id(1)
    last = pl.num_programs(1) - 1

    def glu(u):
        return u[:, :CONV_CH] * jax.nn.sigmoid(u[:, CONV_CH:])

    hwin[0, 0:CONV_HALO, :] = jnp.where(s > 0, glu(prev_ref[...]), 0.0)
    hwin[0, CONV_HALO:CONV_HALO + ts, :] = glu(cur_ref[...])
    hwin[0, CONV_HALO + ts:, :] = jnp.where(s < last, glu(next_ref[...]), 0.0)
    shifted_rows = ts + 2 * CONV_HALO - SUBLANES
    for r in range(1, SUBLANES):
        hwin[r, 0:shifted_rows, :] = hwin[0, r:r + shifted_rows, :]
    pad = CONV_WIDTH // 2
    for c in range(0, ts, CONV_ROWS):
        acc = jnp.broadcast_to(b_ref[...], (CONV_ROWS, CONV_CH))
        for j in range(CONV_WIDTH):
            start = CONV_HALO + c + j - pad
            r = start % SUBLANES
            acc = acc + w_ref[j:j + 1, :] * hwin[r, start - r:start - r + CONV_ROWS, :]
        y = _layer_norm(acc, g_ref[...], be_ref[...])
        o_ref[c:c + CONV_ROWS, :] = (y * jax.nn.sigmoid(y)).astype(o_ref.dtype)


def _conv_module(u, dw_w, dw_b, cn_g, cn_b, ts=512):
    b, s, _ = u.shape
    hb = ts // CONV_HALO
    nhalo = s // CONV_HALO
    row = lambda a: a.reshape(1, CONV_CH)
    return pl.pallas_call(
        functools.partial(_conv_kernel, ts=ts),
        grid=(b, s // ts),
        in_specs=[
            pl.BlockSpec((None, CONV_HALO, 2 * CONV_CH),
                         lambda i, j: (i, jnp.maximum(j * hb - 1, 0), 0)),
            pl.BlockSpec((None, ts, 2 * CONV_CH), lambda i, j: (i, j, 0)),
            pl.BlockSpec((None, CONV_HALO, 2 * CONV_CH),
                         lambda i, j: (i, jnp.minimum((j + 1) * hb, nhalo - 1), 0)),
            pl.BlockSpec((CONV_WIDTH, CONV_CH), lambda i, j: (0, 0)),
            pl.BlockSpec((1, CONV_CH), lambda i, j: (0, 0)),
            pl.BlockSpec((1, CONV_CH), lambda i, j: (0, 0)),
            pl.BlockSpec((1, CONV_CH), lambda i, j: (0, 0)),
        ],
        out_specs=pl.BlockSpec((None, ts, CONV_CH), lambda i, j: (i, j, 0)),
        out_shape=jax.ShapeDtypeStruct((b, s, CONV_CH), BF16),
        scratch_shapes=[pltpu.VMEM((SUBLANES, ts + 2 * CONV_HALO, CONV_CH), F32)],
        compiler_params=_params("parallel", "parallel"),
        name="conv_module",
    )(u, u, u, dw_w, row(dw_b), row(cn_g), row(cn_b))


BAND_Q = 128
BAND_UNROLL = 4
DIL_TILE = 2048


def _band_block(q, k, v, slope, band_bias, key_token0, key_step, seq):
    s = _dot_nt(q, k) * (1.0 / math.sqrt(B_HD)) + slope * band_bias
    col = lax.broadcasted_iota(jnp.int32, (1, 2 * BAND_Q), 1)
    token = key_token0 + col * key_step
    in_seq = token.astype(jnp.uint32) < jnp.uint32(seq)
    s = jnp.where(in_seq, s, NEG)
    m = jnp.max(s, -1, keepdims=True)
    p = jnp.exp(s - m)
    l = jnp.sum(p, -1, keepdims=True)
    return _dot(p.astype(BF16), v) / l, m + jnp.log(l)


def _dilated_kernel(slope_ref, *refs, seq):
    q_refs = refs[0:3]
    kv_refs = refs[3:21]
    o_ref = refs[21]
    kw0, vw0, qf1, kw1, vw1, qf2, kw2, vw2, og0, og1, og2, lg0, lg1, lg2 = refs[22:]
    head = pl.program_id(1)
    t0 = pl.program_id(2) * DIL_TILE

    row = lax.broadcasted_iota(jnp.int32, (BAND_Q, 2 * BAND_Q), 0)
    col = lax.broadcasted_iota(jnp.int32, (BAND_Q, 2 * BAND_Q), 1)
    dist = jnp.abs(col - row - B_HALF)
    band_bias = jnp.where(dist <= B_HALF, -dist.astype(F32), NEG)

    def stage(win, group, which, dtype):
        halo = B_HALF * B_DILATIONS[group]
        prev_ref, cur_ref, next_ref = kv_refs[group * 6 + which * 3:group * 6 + which * 3 + 3]
        win[0:halo, :] = prev_ref[...].astype(dtype)
        win[halo:halo + DIL_TILE, :] = cur_ref[...].astype(dtype)
        win[halo + DIL_TILE:, :] = next_ref[...].astype(dtype)

    def slope_of(group):
        return slope_ref[group * B_HEADS + head] * float(B_DILATIONS[group])

    stage(kw0, 0, 0, BF16)
    stage(vw0, 0, 1, BF16)
    slope0 = slope_of(0)

    def body0(i, carry):
        i0 = pl.multiple_of(i * BAND_Q, BAND_Q)
        o, lse = _band_block(q_refs[0][pl.ds(i0, BAND_Q), :], kw0[pl.ds(i0, 2 * BAND_Q), :],
                             vw0[pl.ds(i0, 2 * BAND_Q), :], slope0, band_bias,
                             t0 + i0 - B_HALF, 1, seq)
        og0[pl.ds(i0, BAND_Q), :] = o
        lg0[pl.ds(i0, BAND_Q), :] = jnp.broadcast_to(lse, (BAND_Q, B_HD))
        return carry

    lax.fori_loop(0, DIL_TILE // BAND_Q, body0, 0, unroll=BAND_UNROLL)

    for group, qf, kw, vw, og, lg in ((1, qf1, kw1, vw1, og1, lg1), (2, qf2, kw2, vw2, og2, lg2)):
        d = B_DILATIONS[group]
        nq = DIL_TILE // d
        qf[...] = q_refs[group][...].astype(F32)
        stage(kw, group, 0, F32)
        stage(vw, group, 1, F32)
        slope = slope_of(group)

        def body(r, carry, d=d, nq=nq, qf=qf, kw=kw, vw=vw, og=og, lg=lg, slope=slope):
            qr = qf[pl.ds(r, nq, stride=d), :].astype(BF16)
            kr = kw[pl.ds(r, nq + 2 * B_HALF, stride=d), :].astype(BF16)
            vr = vw[pl.ds(r, nq + 2 * B_HALF, stride=d), :].astype(BF16)
            for i0 in range(0, nq, BAND_Q):
                o, lse = _band_block(qr[i0:i0 + BAND_Q], kr[i0:i0 + 2 * BAND_Q],
                                     vr[i0:i0 + 2 * BAND_Q], slope, band_bias,
                                     t0 + r + (i0 - B_HALF) * d, d, seq)
                og[pl.ds(r + i0 * d, BAND_Q, stride=d), :] = o
                lg[pl.ds(r + i0 * d, BAND_Q, stride=d), :] = jnp.broadcast_to(lse, (BAND_Q, B_HD))
            return carry

        lax.fori_loop(0, d, body, 0, unroll=max(1, BAND_UNROLL * BAND_Q * d // DIL_TILE))

    l0, l1, l2 = lg0[...], lg1[...], lg2[...]
    top = jnp.maximum(jnp.maximum(l0, l1), l2)
    w0, w1, w2 = jnp.exp(l0 - top), jnp.exp(l1 - top), jnp.exp(l2 - top)
    mix = (w0 * og0[...] + w1 * og1[...] + w2 * og2[...]) / (w0 + w1 + w2)
    o_ref[...] = mix.astype(o_ref.dtype)


def _mixture_of_dilations(qkv):
    b, s, _ = qkv.shape
    n = B_GROUPS * B_HEADS
    slopes = jnp.asarray(np.array([2.0 ** (-8.0 * (i + 1) / n) for i in range(n)], np.float32))
    n_tiles = s // DIL_TILE

    def column(which, group):
        return (which * B_GROUPS + group) * B_HEADS

    def tile_spec(col):
        return pl.BlockSpec((None, DIL_TILE, B_HD), lambda i, h, t: (i, t, col + h))

    def halo_specs(col, group):
        halo = B_HALF * B_DILATIONS[group]
        per_tile = DIL_TILE // halo
        last = s // halo - 1
        return [pl.BlockSpec((None, halo, B_HD),
                             lambda i, h, t: (i, jnp.maximum(t * per_tile - 1, 0), col + h)),
                tile_spec(col),
                pl.BlockSpec((None, halo, B_HD),
                             lambda i, h, t: (i, jnp.minimum((t + 1) * per_tile, last), col + h))]

    in_specs = [pl.BlockSpec(memory_space=pltpu.SMEM)]
    in_specs += [tile_spec(column(0, g)) for g in range(B_GROUPS)]
    for g in range(B_GROUPS):
        in_specs += halo_specs(column(1, g), g) + halo_specs(column(2, g), g)
    scratch = []
    for g, d in enumerate(B_DILATIONS):
        win = (DIL_TILE + 2 * B_HALF * d, B_HD)
        if g == 0:
            scratch += [pltpu.VMEM(win, BF16)] * 2
        else:
            scratch += [pltpu.VMEM((DIL_TILE, B_HD), F32), pltpu.VMEM(win, F32), pltpu.VMEM(win, F32)]
    scratch += [pltpu.VMEM((DIL_TILE, B_HD), F32)] * (2 * B_GROUPS)
    return pl.pallas_call(
        functools.partial(_dilated_kernel, seq=s),
        grid=(b, B_HEADS, n_tiles),
        in_specs=in_specs,
        out_specs=pl.BlockSpec((None, DIL_TILE, B_HD), lambda i, h, t: (i, t, h)),
        out_shape=jax.ShapeDtypeStruct((b, s, B_WIDTH), BF16),
        scratch_shapes=scratch,
        compiler_params=_params("parallel", "parallel", "parallel"),
        name="dilated_mixture_attn",
    )(slopes, *([qkv] * (1 + 3 + 6 * B_GROUPS - 1)))


def _proj_ln_kernel(*refs, n_in):
    ins = refs[:n_in]
    ws = refs[n_in:2 * n_in]
    x_ref, g_ref, b_ref, o_ref = refs[2 * n_in:]
    y = DN_ALPHA * x_ref[...]
    for a_ref, w_ref in zip(ins, ws):
        y = y + _dot(a_ref[...], w_ref[...])
    o_ref[...] = _layer_norm(y, g_ref[...], b_ref[...])


def _proj_ln(ins, ws, x, g, b, tm=512):
    n, dm = x.shape
    n_in = len(ins)
    in_specs = ([pl.BlockSpec((tm, a.shape[1]), lambda i: (i, 0)) for a in ins]
                + [pl.BlockSpec(w.shape, lambda i: (0, 0)) for w in ws]
                + [pl.BlockSpec((tm, dm), lambda i: (i, 0)),
                   pl.BlockSpec((1, dm), lambda i: (0, 0)),
                   pl.BlockSpec((1, dm), lambda i: (0, 0))])
    return pl.pallas_call(
        functools.partial(_proj_ln_kernel, n_in=n_in),
        grid=(n // tm,),
        in_specs=in_specs,
        out_specs=pl.BlockSpec((tm, dm), lambda i: (i, 0)),
        out_shape=jax.ShapeDtypeStruct((n, dm), F32),
        compiler_params=_params("parallel"),
        name="mixer_out_proj_ln",
    )(*ins, *ws, x, g.reshape(1, dm), b.reshape(1, dm))


def _xattn_kernel(x_ref, k_ref, v_ref, wq_ref, wo_ref, g_ref, b_ref, o_ref, *maybe_ob_ref):
    x = x_ref[...]
    q = (_dot(x.astype(BF16), wq_ref[...]) * (1.0 / math.sqrt(X_HD))).astype(BF16)
    heads = []
    for h in range(X_HEADS):
        sl = slice(h * X_HD, (h + 1) * X_HD)
        s = _dot_nt(q[:, sl], k_ref[:, sl])
        m = jnp.max(s, -1, keepdims=True)
        p = jnp.exp(s - m)
        l = jnp.sum(p, -1, keepdims=True)
        heads.append((_dot(p.astype(BF16), v_ref[:, sl]) / l).astype(BF16))
    o = jnp.concatenate(heads, -1)
    y = _layer_norm(DN_ALPHA * x + _dot(o, wo_ref[...]), g_ref[...], b_ref[...])
    o_ref[...] = y
    for ob_ref in maybe_ob_ref:
        ob_ref[...] = y.astype(BF16)


def _xattn(x, kv, wq, wo, g, b, with_bf16_copy, tm=512):
    bsz, s, dm = x.shape
    m = kv.shape[1]
    row_spec = pl.BlockSpec((None, tm, dm), lambda i, j: (i, j, 0))
    out_specs = [row_spec]
    out_shape = [jax.ShapeDtypeStruct((bsz, s, dm), F32)]
    if with_bf16_copy:
        out_specs.append(row_spec)
        out_shape.append(jax.ShapeDtypeStruct((bsz, s, dm), BF16))
    return pl.pallas_call(
        _xattn_kernel,
        grid=(bsz, s // tm),
        in_specs=[row_spec,
                  pl.BlockSpec((None, m, dm), lambda i, j: (i, 0, 0)),
                  pl.BlockSpec((None, m, dm), lambda i, j: (i, 0, 1)),
                  pl.BlockSpec((dm, dm), lambda i, j: (0, 0)),
                  pl.BlockSpec((dm, dm), lambda i, j: (0, 0)),
                  pl.BlockSpec((1, dm), lambda i, j: (0, 0)),
                  pl.BlockSpec((1, dm), lambda i, j: (0, 0))],
        out_specs=out_specs,
        out_shape=out_shape,
        compiler_params=_params("parallel", "parallel"),
        name="memory_xattn_ln",
    )(x, kv, kv, wq, wo, g.reshape(1, dm), b.reshape(1, dm))


def _ffn_kernel(x_ref, w1_ref, w3_ref, w2_ref, g_ref, b_ref, o_ref, acc):
    j = pl.program_id(1)

    @pl.when(j == 0)
    def _():
        acc[...] = DN_ALPHA * x_ref[...]

    xb = x_ref[...].astype(BF16)
    a = _dot(xb, w1_ref[...])
    gate = _dot(xb, w3_ref[...])
    h = (a * jax.nn.sigmoid(a) * gate).astype(BF16)
    acc[...] += _dot(h, w2_ref[...])

    @pl.when(j == pl.num_programs(1) - 1)
    def _():
        o_ref[...] = _layer_norm(acc[...], g_ref[...], b_ref[...])


def _ffn(x, w13, w2, g, b, tm=512, tf=1408):
    n, dm = x.shape
    ff = w2.shape[0]
    nf = ff // tf
    return pl.pallas_call(
        _ffn_kernel,
        grid=(n // tm, nf),
        in_specs=[pl.BlockSpec((tm, dm), lambda i, j: (i, 0)),
                  pl.BlockSpec((dm, tf), lambda i, j: (0, j)),
                  pl.BlockSpec((dm, tf), lambda i, j: (0, nf + j)),
                  pl.BlockSpec((tf, dm), lambda i, j: (j, 0)),
                  pl.BlockSpec((1, dm), lambda i, j: (0, 0)),
                  pl.BlockSpec((1, dm), lambda i, j: (0, 0))],
        out_specs=pl.BlockSpec((tm, dm), lambda i, j: (i, 0)),
        out_shape=jax.ShapeDtypeStruct((n, dm), F32),
        scratch_shapes=[pltpu.VMEM((tm, dm), F32)],
        compiler_params=_params("parallel", "arbitrary"),
        name="dense_swiglu_ln",
    )(x, w13, w13, w2, g.reshape(1, dm), b.reshape(1, dm))


def _rope_tables(seq_len):
    rows = seq_len // GRID_W
    row = jnp.broadcast_to(jnp.arange(rows, dtype=F32)[:, None], (rows, GRID_W)).reshape(-1)
    col = jnp.broadcast_to(jnp.arange(GRID_W, dtype=F32)[None, :], (rows, GRID_W)).reshape(-1)
    axis_dim = C_HD // 2
    freqs = ROPE_THETA ** (-jnp.arange(0, axis_dim, 2, dtype=F32) / axis_dim)
    ang_r = row[:, None] * freqs[None, :]
    ang_c = col[:, None] * freqs[None, :]
    ang = jnp.concatenate([ang_r, ang_r, ang_c, ang_c], -1)
    cos, sin = jnp.cos(ang), jnp.sin(ang)
    first = (jnp.arange(C_HD) % axis_dim) < axis_dim // 2
    sin_up = jnp.where(first[None, :], -sin, 0.0)
    sin_dn = jnp.where(first[None, :], 0.0, sin)
    return cos, sin_up, sin_dn


def _odd_in_kernel(x_ref, w_ref, qn_ref, kn_ref, cos_ref, su_ref, sd_ref, qt_ref, k_ref, vt_ref):
    xb = x_ref[...].astype(BF16)
    cos, su, sd = cos_ref[...], su_ref[...], sd_ref[...]
    quarter = C_HD // 4
    q_scale = math.log2(math.e) / math.sqrt(C_HD)

    def norm_rope(h, gain):
        h = h * lax.rsqrt(jnp.mean(h * h, -1, keepdims=True) + RMS_EPS) * gain
        up = pltpu.roll(h, C_HD - quarter, 1)
        dn = pltpu.roll(h, quarter, 1)
        return h * cos + up * su + dn * sd

    nq, nk = C_HEADS * C_HD, C_KV_HEADS * C_HD
    for i in range(C_HEADS):
        h = _dot(xb, w_ref[:, i * C_HD:(i + 1) * C_HD])
        qt_ref[i * C_HD:(i + 1) * C_HD, :] = (norm_rope(h, qn_ref[...]) * q_scale).T.astype(BF16)
    for i in range(C_KV_HEADS):
        h = _dot(xb, w_ref[:, nq + i * C_HD:nq + (i + 1) * C_HD])
        k_ref[:, i * C_HD:(i + 1) * C_HD] = norm_rope(h, kn_ref[...]).astype(BF16)
        v = _dot(xb, w_ref[:, nq + nk + i * C_HD:nq + nk + (i + 1) * C_HD])
        vt_ref[i * C_HD:(i + 1) * C_HD, :] = v.T.astype(BF16)


def _odd_in_proj(x, w, qn, kn, tm=512):
    b, s, dm = x.shape
    nq, nk = C_HEADS * C_HD, C_KV_HEADS * C_HD
    cos, su, sd = _rope_tables(s)
    tab = pl.BlockSpec((tm, C_HD), lambda i, j: (j, 0))
    vec = pl.BlockSpec((1, C_HD), lambda i, j: (0, 0))
    return pl.pallas_call(
        _odd_in_kernel,
        grid=(b, s // tm),
        in_specs=[pl.BlockSpec((None, tm, dm), lambda i, j: (i, j, 0)),
                  pl.BlockSpec(w.shape, lambda i, j: (0, 0)),
                  vec, vec, tab, tab, tab],
        out_specs=[pl.BlockSpec((None, nq, tm), lambda i, j: (i, 0, j)),
                   pl.BlockSpec((None, tm, nk), lambda i, j: (i, j, 0)),
                   pl.BlockSpec((None, nk, tm), lambda i, j: (i, 0, j))],
        out_shape=[jax.ShapeDtypeStruct((b, nq, s), BF16),
                   jax.ShapeDtypeStruct((b, s, nk), BF16),
                   jax.ShapeDtypeStruct((b, nk, s), BF16)],
        compiler_params=_params("parallel", "parallel"),
        name="gqa_in_proj_norm_rope",
    )(x, w, qn.reshape(1, C_HD), kn.reshape(1, C_HD), cos, su, sd)


def _gqa_kernel(qt_ref, k_ref, vt_ref, o_ref, acc_sc, st_sc, *, seq, tq, tk):
    qt = jnp.concatenate([qt_ref[h * C_HD:(h + 1) * C_HD, :] for h in range(C_GROUP)], 1)
    cols = C_GROUP * tq
    n = seq // tk
    acc_sc[...] = jnp.zeros(acc_sc.shape, F32)

    def scores(kc):
        k0 = pl.multiple_of(kc * tk, tk)
        return _dot(k_ref[pl.ds(k0, tk), :], qt)

    def consume(kc, st, m_old, l_old):
        k0 = pl.multiple_of(kc * tk, tk)
        m_new = jnp.maximum(m_old, jnp.max(st, 0, keepdims=True))
        alpha = jnp.exp2(m_old - m_new)
        p = jnp.exp2(st - m_new)
        l_new = alpha * l_old + jnp.sum(p, 0, keepdims=True)
        acc_sc[...] = alpha * acc_sc[...] + _dot(vt_ref[:, pl.ds(k0, tk)], p.astype(BF16))
        return m_new, l_new

    st_a, st_b = st_sc.at[0], st_sc.at[1]
    st_a[...] = scores(0)

    def body(i, carry):
        kc = 2 * i
        st_b[...] = scores(kc + 1)
        m, l = consume(kc, st_a[...], *carry)
        st_a[...] = scores(jnp.minimum(kc + 2, n - 1))
        return consume(kc + 1, st_b[...], m, l)

    init = (jnp.full((1, cols), -jnp.inf, F32), jnp.zeros((1, cols), F32))
    _, l = lax.fori_loop(0, n // 2, body, init)
    out = acc_sc[...] / l
    for h in range(C_GROUP):
        o_ref[:, h * C_HD:(h + 1) * C_HD] = out[:, h * tq:(h + 1) * tq].T.astype(o_ref.dtype)


def _gqa(qt, k, vt, tq=512, tk=512):
    b, nq, s = qt.shape
    gw = C_GROUP * C_HD
    assert (s // tk) % 2 == 0
    return pl.pallas_call(
        functools.partial(_gqa_kernel, seq=s, tq=tq, tk=tk),
        grid=(b, C_KV_HEADS, s // tq),
        in_specs=[pl.BlockSpec((None, gw, tq), lambda i, h, j: (i, h, j)),
                  pl.BlockSpec((None, s, C_HD), lambda i, h, j: (i, 0, h)),
                  pl.BlockSpec((None, C_HD, s), lambda i, h, j: (i, h, 0))],
        out_specs=pl.BlockSpec((None, tq, gw), lambda i, h, j: (i, j, h)),
        out_shape=jax.ShapeDtypeStruct((b, s, nq), BF16),
        scratch_shapes=[pltpu.VMEM((C_HD, C_GROUP * tq), F32),
                        pltpu.VMEM((2, tk, C_GROUP * tq), F32)],
        compiler_params=_params("parallel", "parallel", "parallel"),
        name="gqa_flash",
    )(qt, k, vt)


MOE_CHUNK = 512
MOE_TILE = 512
MOE_GROUP = 1024
UNSELECTED = -float(2 ** 26)


def _router_kernel(x_ref, w_ref, gate_ref, sel_ref):
    logits = jnp.dot(x_ref[...], w_ref[...], preferred_element_type=F32,
                     precision=lax.Precision.HIGHEST)
    lane = lax.broadcasted_iota(jnp.int32, logits.shape, 1)
    logits = jnp.where(lane < N_EXPERTS, logits, -jnp.inf)
    m1 = jnp.max(logits, -1, keepdims=True)
    i1 = jnp.min(jnp.where(logits == m1, lane, LANES), -1, keepdims=True)
    rest = jnp.where(lane == i1, -jnp.inf, logits)
    m2 = jnp.max(rest, -1, keepdims=True)
    i2 = jnp.min(jnp.where(rest == m2, lane, LANES), -1, keepdims=True)
    e = jnp.exp(m2 - m1)
    g1 = 1.0 / (1.0 + e)
    g2 = e / (1.0 + e)
    gate_ref[...] = jnp.where(lane == i1, g1, 0.0) + jnp.where(lane == i2, g2, 0.0)
    sel_ref[...] = jnp.where((lane == i1) | (lane == i2), 1.0, 0.0).astype(BF16)


def _router(x, w):
    n, dm = x.shape
    tm = MOE_CHUNK
    wp = jnp.zeros((dm, LANES), F32).at[:, :N_EXPERTS].set(w)
    return pl.pallas_call(
        _router_kernel,
        grid=(n // tm,),
        in_specs=[pl.BlockSpec((tm, dm), lambda i: (i, 0)),
                  pl.BlockSpec((dm, LANES), lambda i: (0, 0))],
        out_specs=[pl.BlockSpec((tm, LANES), lambda i: (i, 0))] * 2,
        out_shape=[jax.ShapeDtypeStruct((n, LANES), F32), jax.ShapeDtypeStruct((n, LANES), BF16)],
        compiler_params=_params("parallel"),
        name="moe_router_top2",
    )(x, wp)


def _rank_kernel(sel_ref, rank_ref, rank_t_ref, base_ref, cnt_ref, carry):
    @pl.when(pl.program_id(0) == 0)
    def _():
        carry[...] = jnp.zeros(carry.shape, F32)

    sel = sel_ref[...]
    n = sel.shape[0]
    row = lax.broadcasted_iota(jnp.int32, (n, n), 0)
    col = lax.broadcasted_iota(jnp.int32, (n, n), 1)
    earlier = jnp.where(row > col, 1.0, 0.0).astype(BF16)
    before = _dot(earlier, sel) + carry[...]
    chosen = sel.astype(F32)
    rank = jnp.where(chosen > 0.0, before, UNSELECTED)
    rank_ref[...] = rank
    rank_t_ref[...] = rank.T[:SUBLANES, :]
    cnt = jnp.sum(chosen, 0, keepdims=True)
    base_ref[...] = carry[...]
    cnt_ref[...] = cnt
    carry[...] += cnt


def _rank(sel):
    n = sel.shape[0]
    c = n // MOE_CHUNK
    stat = pl.BlockSpec((None, 1, LANES), lambda i: (i, 0, 0))
    return pl.pallas_call(
        _rank_kernel,
        grid=(c,),
        in_specs=[pl.BlockSpec((MOE_CHUNK, LANES), lambda i: (i, 0))],
        out_specs=[pl.BlockSpec((MOE_CHUNK, LANES), lambda i: (i, 0)),
                   pl.BlockSpec((SUBLANES, MOE_CHUNK), lambda i: (0, i)),
                   stat, stat],
        out_shape=[jax.ShapeDtypeStruct((n, LANES), F32),
                   jax.ShapeDtypeStruct((SUBLANES, n), F32),
                   jax.ShapeDtypeStruct((c, 1, LANES), F32),
                   jax.ShapeDtypeStruct((c, 1, LANES), F32)],
        scratch_shapes=[pltpu.VMEM((1, LANES), F32)],
        compiler_params=_params("arbitrary"),
        name="moe_rank",
    )(sel)


def _fill_forward(values, valid):
    idx = lax.cummax(jnp.where(valid, jnp.arange(values.shape[0]), -1))
    idx = jnp.where(idx < 0, jnp.argmax(valid), idx)
    return values[idx]


def _moe_plan(base, cnt, n_group_tiles):
    counts = base[-1] + cnt[-1]
    gsize = -(-counts // MOE_GROUP) * MOE_GROUP
    gend = jnp.cumsum(gsize)
    gstart = gend - gsize
    tile_expert = jnp.minimum(
        jnp.searchsorted(gend, jnp.arange(n_group_tiles) * MOE_GROUP, side='right'),
        N_EXPERTS - 1).astype(jnp.int32)
    n_tiles = n_group_tiles * (MOE_GROUP // MOE_TILE)
    first_slot = gstart[None, :] + base
    t0 = first_slot // MOE_TILE
    t1 = (first_slot + cnt - 1) // MOE_TILE
    v0 = cnt > 0
    v1 = v0 & (t1 > t0)
    td = jnp.stack([t0, t1], -1).transpose(1, 0, 2).reshape(-1)
    vd = jnp.stack([v0, v1], -1).transpose(1, 0, 2).reshape(-1)
    seen = lax.cummax(jnp.where(vd, td, -1))
    prev = jnp.concatenate([jnp.full((1,), -1, seen.dtype), seen[:-1]])
    first = vd & (td != prev)
    clip = lambda t: jnp.clip(t, 0, n_tiles - 1).astype(jnp.int32)
    as_i32 = lambda t: t.astype(jnp.int32)
    dispatch = (clip(_fill_forward(td, vd)), as_i32(vd), as_i32(first))
    t0f, v0f, t1f, v1f = t0.reshape(-1), v0.reshape(-1), t1.reshape(-1), v1.reshape(-1)
    combine = (clip(_fill_forward(t0f, v0f)), as_i32(v0f), clip(_fill_forward(t1f, v1f)), as_i32(v1f))
    return gstart, tile_expert, dispatch, combine


def _dispatch_kernel(tile_ref, valid_ref, first_ref, slot_ref, x_ref, zero_ref, o_ref, *, n_chunks):
    del zero_ref
    e = pl.program_id(0)
    w = (e * n_chunks + pl.program_id(1)) * 2 + pl.program_id(2)

    @pl.when(valid_ref[w] == 1)
    def _():
        local = slot_ref[pl.ds(e, 1), :] - (tile_ref[w] * MOE_TILE).astype(F32)
        sub = lax.broadcasted_iota(jnp.int32, (MOE_TILE, MOE_CHUNK), 0).astype(F32)
        onehot = jnp.where(sub == local, 1.0, 0.0).astype(BF16)
        rows = _dot(onehot, x_ref[...])

        @pl.when(first_ref[w] == 1)
        def _():
            o_ref[...] = rows.astype(BF16)

        @pl.when(first_ref[w] == 0)
        def _():
            o_ref[...] += rows.astype(BF16)


def _dispatch(plan, slot_t, xb, n_slots):
    n, dm = xb.shape
    c = n // MOE_CHUNK
    grid_spec = pltpu.PrefetchScalarGridSpec(
        num_scalar_prefetch=3,
        grid=(N_EXPERTS, c, 2),
        in_specs=[pl.BlockSpec((SUBLANES, MOE_CHUNK), lambda e, i, k, *_: (0, i)),
                  pl.BlockSpec((MOE_CHUNK, dm), lambda e, i, k, *_: (i, 0)),
                  pl.BlockSpec(memory_space=pl.ANY)],
        out_specs=pl.BlockSpec((MOE_TILE, dm),
                               lambda e, i, k, tile, valid, first: (tile[(e * c + i) * 2 + k], 0)),
    )
    return pl.pallas_call(
        functools.partial(_dispatch_kernel, n_chunks=c),
        grid_spec=grid_spec,
        out_shape=jax.ShapeDtypeStruct((n_slots, dm), BF16),
        input_output_aliases={5: 0},
        compiler_params=_params("arbitrary", "arbitrary", "arbitrary"),
        name="moe_dispatch",
    )(*plan, slot_t, xb, jnp.zeros((n_slots, dm), BF16))


def _experts_kernel(expert_ref, x_ref, w1_ref, w3_ref, w2_ref, o_ref, acc):
    del expert_ref
    j = pl.program_id(1)
    xb = x_ref[...]
    a = _dot(xb, w1_ref[...])
    gate = _dot(xb, w3_ref[...])
    y = _dot((a * jax.nn.sigmoid(a) * gate).astype(BF16), w2_ref[...])

    @pl.when(j == 0)
    def _():
        acc[...] = y

    @pl.when(j > 0)
    def _():
        acc[...] += y

    @pl.when(j == pl.num_programs(1) - 1)
    def _():
        o_ref[...] = acc[...].astype(BF16)


def _experts(tile_expert, xs, w13, w2, tf=896):
    n_slots, dm = xs.shape
    ff = w2.shape[1]
    nf = ff // tf
    grid_spec = pltpu.PrefetchScalarGridSpec(
        num_scalar_prefetch=1,
        grid=(n_slots // MOE_GROUP, nf),
        in_specs=[pl.BlockSpec((MOE_GROUP, dm), lambda i, j, te: (i, 0)),
                  pl.BlockSpec((None, dm, tf), lambda i, j, te: (te[i], 0, j)),
                  pl.BlockSpec((None, dm, tf), lambda i, j, te: (te[i], 0, nf + j)),
                  pl.BlockSpec((None, tf, dm), lambda i, j, te: (te[i], j, 0))],
        out_specs=pl.BlockSpec((MOE_GROUP, dm), lambda i, j, te: (i, 0)),
        scratch_shapes=[pltpu.VMEM((MOE_GROUP, dm), F32)],
    )
    return pl.pallas_call(
        _experts_kernel,
        grid_spec=grid_spec,
        out_shape=jax.ShapeDtypeStruct((n_slots, dm), BF16),
        compiler_params=_params("parallel", "arbitrary"),
        name="moe_experts_swiglu",
    )(tile_expert, xs, w13, w13, w2)


def _combine_kernel(t0_ref, v0_ref, t1_ref, v1_ref, slot_ref, gate_ref, ya_ref, yb_ref, x_ref,
                    g_ref, b_ref, o_ref, acc):
    e = pl.program_id(1)
    w = pl.program_id(0) * N_EXPERTS + e

    @pl.when(e == 0)
    def _():
        acc[...] = DN_ALPHA * x_ref[...]

    def add_rows(tile, y_ref):
        lane = lax.broadcasted_iota(jnp.int32, (MOE_CHUNK, LANES), 1)
        slot = jnp.sum(jnp.where(lane == e, slot_ref[...], 0.0), -1, keepdims=True)
        gate = jnp.sum(jnp.where(lane == e, gate_ref[...], 0.0), -1, keepdims=True)
        col = lax.broadcasted_iota(jnp.int32, (MOE_CHUNK, MOE_TILE), 1).astype(F32)
        onehot = jnp.where(col == slot - (tile * MOE_TILE).astype(F32), 1.0, 0.0).astype(BF16)
        acc[...] += gate * _dot(onehot, y_ref[...])

    @pl.when(v0_ref[w] == 1)
    def _():
        add_rows(t0_ref[w], ya_ref)

    @pl.when(v1_ref[w] == 1)
    def _():
        add_rows(t1_ref[w], yb_ref)

    @pl.when(e == N_EXPERTS - 1)
    def _():
        o_ref[...] = _layer_norm(acc[...], g_ref[...], b_ref[...])


def _combine(plan, slot, gates, ys, x, g, b):
    n, dm = x.shape
    c = n // MOE_CHUNK
    tok = lambda width: pl.BlockSpec((MOE_CHUNK, width), lambda i, e, *_: (i, 0))
    vec = pl.BlockSpec((1, dm), lambda i, e, *_: (0, 0))
    grid_spec = pltpu.PrefetchScalarGridSpec(
        num_scalar_prefetch=4,
        grid=(c, N_EXPERTS),
        in_specs=[tok(LANES), tok(LANES),
                  pl.BlockSpec((MOE_TILE, dm), lambda i, e, t0, v0, t1, v1: (t0[i * N_EXPERTS + e], 0)),
                  pl.BlockSpec((MOE_TILE, dm), lambda i, e, t0, v0, t1, v1: (t1[i * N_EXPERTS + e], 0)),
                  tok(dm), vec, vec],
        out_specs=tok(dm),
        scratch_shapes=[pltpu.VMEM((MOE_CHUNK, dm), F32)],
    )
    return pl.pallas_call(
        _combine_kernel,
        grid_spec=grid_spec,
        out_shape=jax.ShapeDtypeStruct((n, dm), F32),
        compiler_params=_params("parallel", "arbitrary"),
        name="moe_combine_ln",
    )(*plan, slot, gates, ys, ys, x, g.reshape(1, dm), b.reshape(1, dm))


def _moe(x, xb, router_w, w13, w2, g, b):
    n, dm = x.shape
    n_slots = TOP_K * n + N_EXPERTS * MOE_GROUP
    gates, sel = _router(x, router_w)
    rank, rank_t, base, cnt = _rank(sel)
    as_counts = lambda t: t[:, 0, :N_EXPERTS].astype(jnp.int32)
    gstart, tile_expert, dispatch_plan, combine_plan = _moe_plan(
        as_counts(base), as_counts(cnt), n_slots // MOE_GROUP)
    gstart = gstart.astype(F32)
    slot = rank + jnp.zeros((LANES,), F32).at[:N_EXPERTS].set(gstart)[None, :]
    slot_t = rank_t + gstart[:, None]
    xs = _dispatch(dispatch_plan, slot_t, xb, n_slots)
    ys = _experts(tile_expert, xs, w13, w2)
    return _combine(combine_plan, slot, gates, ys, x, g, b)


TOP_K = 2


def _trunk(x, mem, w):
    b, s, dm = x.shape
    n = b * s
    mem2 = mem.reshape(-1, dm)
    for i in range(DEPTH):
        j = i // 2
        even = i % 2 == 0
        x2 = x.reshape(n, dm)
        if even:
            u, qkv = _matmul(x2, w['e_w_in'][j], (2 * CONV_CH, QKV_EVEN), (F32, BF16), "even_in_proj")
            a_out = _conv_module(u.reshape(b, s, 2 * CONV_CH), w['e_dw_w'][j], w['e_dw_b'][j],
                                 w['e_cn_g'][j], w['e_cn_b'][j])
            b_out = _mixture_of_dilations(qkv.reshape(b, s, QKV_EVEN))
            w_out = w['e_w_out'][j]
            x2 = _proj_ln([a_out.reshape(n, CONV_CH), b_out.reshape(n, B_WIDTH)],
                          [w_out[:CONV_CH], w_out[CONV_CH:]], x2, w['ln_g'][i, 0], w['ln_b'][i, 0])
        else:
            qt, k, vt = _odd_in_proj(x, w['o_w_in'][j], w['o_q_norm'][j], w['o_k_norm'][j])
            attn = _gqa(qt, k, vt)
            x2 = _proj_ln([attn.reshape(n, C_HEADS * C_HD)], [w['o_w_out'][j]], x2,
                          w['ln_g'][i, 0], w['ln_b'][i, 0])
        (kv,) = _matmul(mem2, w['x_wkv'][i], (2 * dm,), (BF16,), "memory_kv_proj", tm=256)
        outs = _xattn(x2.reshape(b, s, dm), kv.reshape(b, -1, 2 * dm), w['x_wq'][i], w['x_wo'][i],
                      w['ln_g'][i, 1], w['ln_b'][i, 1], with_bf16_copy=not even)
        x2 = outs[0].reshape(n, dm)
        if even:
            x2 = _ffn(x2, w['f_w13'][j], w['f_w2'][j], w['ln_g'][i, 2], w['ln_b'][i, 2])
        else:
            x2 = _moe(x2, outs[1].reshape(n, dm), w['m_router'][j], w['m_w13'][j], w['m_w2'][j],
                      w['ln_g'][i, 2], w['ln_b'][i, 2])
        x = x2.reshape(b, s, dm)
    return x


_MXU_WEIGHTS = ('e_w_in', 'e_w_out', 'f_w13', 'f_w2', 'o_w_in', 'o_w_out', 'm_w13', 'm_w2',
                'x_wq', 'x_wkv', 'x_wo')


def kernel(x_prompt, x_sample, mem_prompt, mem_sample, e_w_in, e_dw_w, e_dw_b, e_cn_g, e_cn_b,
           e_w_out, f_w13, f_w2, o_w_in, o_q_norm, o_k_norm, o_w_out, m_router, m_w13, m_w2,
           x_wq, x_wkv, x_wo, ln_g, ln_b):
    w = dict(e_w_in=e_w_in, e_dw_w=e_dw_w, e_dw_b=e_dw_b, e_cn_g=e_cn_g, e_cn_b=e_cn_b,
             e_w_out=e_w_out, f_w13=f_w13, f_w2=f_w2, o_w_in=o_w_in, o_q_norm=o_q_norm,
             o_k_norm=o_k_norm, o_w_out=o_w_out, m_router=m_router, m_w13=m_w13, m_w2=m_w2,
             x_wq=x_wq, x_wkv=x_wkv, x_wo=x_wo, ln_g=ln_g, ln_b=ln_b)
    for name in _MXU_WEIGHTS:
        w[name] = w[name].astype(BF16)
    return (_trunk(x_prompt, mem_prompt, w), _trunk(x_sample, mem_sample, w))
```

```python
import functools
import math

import numpy as np
import jax
import jax.numpy as jnp
from jax import lax
from jax.experimental import pallas as pl
from jax.experimental.pallas import tpu as pltpu

F32 = jnp.float32
BF16 = jnp.bfloat16

D_MODEL = 1024
DEPTH = 2
GRID_W = 64
CONV_CH = D_MODEL // 2
CONV_WIDTH = 31
B_DILATIONS = (1, 4, 16)
B_GROUPS = len(B_DILATIONS)
B_HEADS = 4
B_HD = 128
B_WIDTH = B_HEADS * B_HD
B_HALF = 64
QKV_EVEN = 3 * B_GROUPS * B_WIDTH
C_HEADS = 8
C_KV_HEADS = 2
C_GROUP = C_HEADS // C_KV_HEADS
C_HD = 128
ROPE_THETA = 10000.0
X_HEADS = 4
X_HD = D_MODEL // X_HEADS
FF_DENSE = 2816
N_EXPERTS = 8
FF_EXPERT = 3584
DN_ALPHA = (2 * DEPTH) ** 0.25
LN_EPS = 1e-5
RMS_EPS = 1e-6
NEG = -1e30

LANES = 128
SUBLANES = 8
V7X_VMEM_LIMIT = 56 * 1024 * 1024


def _params(*semantics):
    return pltpu.CompilerParams(dimension_semantics=semantics,
                                vmem_limit_bytes=V7X_VMEM_LIMIT)


def _layer_norm(y, g, b):
    mu = jnp.mean(y, -1, keepdims=True)
    yc = y - mu
    var = jnp.mean(yc * yc, -1, keepdims=True)
    return yc * lax.rsqrt(var + LN_EPS) * g + b


def _dot(a, b):
    return jnp.dot(a, b, preferred_element_type=F32)


def _dot_nt(a, b):
    return lax.dot_general(a, b, (((1,), (1,)), ((), ())), preferred_element_type=F32)


def _matmul_kernel(x_ref, w_ref, *o_refs, splits, chunk):
    xb = x_ref[...].astype(BF16)
    col = 0
    for o_ref, width in zip(o_refs, splits):
        for c in range(0, width, chunk):
            o_ref[:, c:c + chunk] = _dot(xb, w_ref[:, col + c:col + c + chunk]).astype(o_ref.dtype)
        col += width


def _matmul(x, w, splits, dtypes, name, tm=512, chunk=512):
    n, k = x.shape
    return pl.pallas_call(
        functools.partial(_matmul_kernel, splits=splits, chunk=chunk),
        grid=(n // tm,),
        in_specs=[pl.BlockSpec((tm, k), lambda i: (i, 0)),
                  pl.BlockSpec(w.shape, lambda i: (0, 0))],
        out_specs=[pl.BlockSpec((tm, s), lambda i: (i, 0)) for s in splits],
        out_shape=[jax.ShapeDtypeStruct((n, s), dt) for s, dt in zip(splits, dtypes)],
        compiler_params=_params("parallel"),
        name=name,
    )(x, w)


CONV_HALO = 16
CONV_ROWS = 64


def _conv_kernel(prev_ref, cur_ref, next_ref, w_ref, b_ref, g_ref, be_ref, o_ref, hwin, *, ts):
    s = pl.program_id(1)
    last = pl.num_programs(1) - 1

    def glu(u):
        return u[:, :CONV_CH] * jax.nn.sigmoid(u[:, CONV_CH:])

    hwin[0, 0:CONV_HALO, :] = jnp.where(s > 0, glu(prev_ref[...]), 0.0)
    hwin[0, CONV_HALO:CONV_HALO + ts, :] = glu(cur_ref[...])
    hwin[0, CONV_HALO + ts:, :] = jnp.where(s < last, glu(next_ref[...]), 0.0)
    shifted_rows = ts + 2 * CONV_HALO - SUBLANES
    for r in range(1, SUBLANES):
        hwin[r, 0:shifted_rows, :] = hwin[0, r:r + shifted_rows, :]
    pad = CONV_WIDTH // 2
    for c in range(0, ts, CONV_ROWS):
        acc = jnp.broadcast_to(b_ref[...], (CONV_ROWS, CONV_CH))
        for j in range(CONV_WIDTH):
            start = CONV_HALO + c + j - pad
            r = start % SUBLANES
            acc = acc + w_ref[j:j + 1, :] * hwin[r, start - r:start - r + CONV_ROWS, :]
        y = _layer_norm(acc, g_ref[...], be_ref[...])
        o_ref[c:c + CONV_ROWS, :] = (y * jax.nn.sigmoid(y)).astype(o_ref.dtype)


def _conv_module(u, dw_w, dw_b, cn_g, cn_b, ts=512):
    b, s, _ = u.shape
    hb = ts // CONV_HALO
    nhalo = s // CONV_HALO
    row = lambda a: a.reshape(1, CONV_CH)
    return pl.pallas_call(
        functools.partial(_conv_kernel, ts=ts),
        grid=(b, s // ts),
        in_specs=[
            pl.BlockSpec((None, CONV_HALO, 2 * CONV_CH),
                         lambda i, j: (i, jnp.maximum(j * hb - 1, 0), 0)),
            pl.BlockSpec((None, ts, 2 * CONV_CH), lambda i, j: (i, j, 0)),
            pl.BlockSpec((None, CONV_HALO, 2 * CONV_CH),
                         lambda i, j: (i, jnp.minimum((j + 1) * hb, nhalo - 1), 0)),
            pl.BlockSpec((CONV_WIDTH, CONV_CH), lambda i, j: (0, 0)),
            pl.BlockSpec((1, CONV_CH), lambda i, j: (0, 0)),
            pl.BlockSpec((1, CONV_CH), lambda i, j: (0, 0)),
            pl.BlockSpec((1, CONV_CH), lambda i, j: (0, 0)),
        ],
        out_specs=pl.BlockSpec((None, ts, CONV_CH), lambda i, j: (i, j, 0)),
        out_shape=jax.ShapeDtypeStruct((b, s, CONV_CH), BF16),
        scratch_shapes=[pltpu.VMEM((SUBLANES, ts + 2 * CONV_HALO, CONV_CH), F32)],
        compiler_params=_params("parallel", "parallel"),
        name="conv_module",
    )(u, u, u, dw_w, row(dw_b), row(cn_g), row(cn_b))


BAND_Q = 128
BAND_UNROLL = 4
DIL_TILE = 2048


def _band_block(q, k, v, slope, band_bias, key_token0, key_step, seq):
    s = _dot_nt(q, k) * (1.0 / math.sqrt(B_HD)) + slope * band_bias
    col = lax.broadcasted_iota(jnp.int32, (1, 2 * BAND_Q), 1)
    token = key_token0 + col * key_step
    in_seq = token.astype(jnp.uint32) < jnp.uint32(seq)
    s = jnp.where(in_seq, s, NEG)
    m = jnp.max(s, -1, keepdims=True)
    p = jnp.exp(s - m)
    l = jnp.sum(p, -1, keepdims=True)
    return _dot(p.astype(BF16), v) / l, m + jnp.log(l)


def _dilated_kernel(slope_ref, *refs, seq):
    q_refs = refs[0:3]
    kv_refs = refs[3:21]
    o_ref = refs[21]
    kw0, vw0, qf1, kw1, vw1, qf2, kw2, vw2, og0, og1, og2, lg0, lg1, lg2 = refs[22:]
    head = pl.program_id(1)
    t0 = pl.program_id(2) * DIL_TILE

    row = lax.broadcasted_iota(jnp.int32, (BAND_Q, 2 * BAND_Q), 0)
    col = lax.broadcasted_iota(jnp.int32, (BAND_Q, 2 * BAND_Q), 1)
    dist = jnp.abs(col - row - B_HALF)
    band_bias = jnp.where(dist <= B_HALF, -dist.astype(F32), NEG)

    def stage(win, group, which, dtype):
        halo = B_HALF * B_DILATIONS[group]
        prev_ref, cur_ref, next_ref = kv_refs[group * 6 + which * 3:group * 6 + which * 3 + 3]
        win[0:halo, :] = prev_ref[...].astype(dtype)
        win[halo:halo + DIL_TILE, :] = cur_ref[...].astype(dtype)
        win[halo + DIL_TILE:, :] = next_ref[...].astype(dtype)

    def slope_of(group):
        return slope_ref[group * B_HEADS + head] * float(B_DILATIONS[group])

    stage(kw0, 0, 0, BF16)
    stage(vw0, 0, 1, BF16)
    slope0 = slope_of(0)

    def body0(i, carry):
        i0 = pl.multiple_of(i * BAND_Q, BAND_Q)
        o, lse = _band_block(q_refs[0][pl.ds(i0, BAND_Q), :], kw0[pl.ds(i0, 2 * BAND_Q), :],
                             vw0[pl.ds(i0, 2 * BAND_Q), :], slope0, band_bias,
                             t0 + i0 - B_HALF, 1, seq)
        og0[pl.ds(i0, BAND_Q), :] = o
        lg0[pl.ds(i0, BAND_Q), :] = jnp.broadcast_to(lse, (BAND_Q, B_HD))
        return carry

    lax.fori_loop(0, DIL_TILE // BAND_Q, body0, 0, unroll=BAND_UNROLL)

    for group, qf, kw, vw, og, lg in ((1, qf1, kw1, vw1, og1, lg1), (2, qf2, kw2, vw2, og2, lg2)):
        d = B_DILATIONS[group]
        nq = DIL_TILE // d
        qf[...] = q_refs[group][...].astype(F32)
        stage(kw, group, 0, F32)
        stage(vw, group, 1, F32)
        slope = slope_of(group)

        def body(r, carry, d=d, nq=nq, qf=qf, kw=kw, vw=vw, og=og, lg=lg, slope=slope):
            qr = qf[pl.ds(r, nq, stride=d), :].astype(BF16)
            kr = kw[pl.ds(r, nq + 2 * B_HALF, stride=d), :].astype(BF16)
            vr = vw[pl.ds(r, nq + 2 * B_HALF, stride=d), :].astype(BF16)
            for i0 in range(0, nq, BAND_Q):
                o, lse = _band_block(qr[i0:i0 + BAND_Q], kr[i0:i0 + 2 * BAND_Q],
                                     vr[i0:i0 + 2 * BAND_Q], slope, band_bias,
                                     t0 + r + (i0 - B_HALF) * d, d, seq)
                og[pl.ds(r + i0 * d, BAND_Q, stride=d), :] = o
                lg[pl.ds(r + i0 * d, BAND_Q, stride=d), :] = jnp.broadcast_to(lse, (BAND_Q, B_HD))
            return carry

        lax.fori_loop(0, d, body, 0, unroll=max(1, BAND_UNROLL * BAND_Q * d // DIL_TILE))

    l0, l1, l2 = lg0[...], lg1[...], lg2[...]
    top = jnp.maximum(jnp.maximum(l0, l1), l2)
    w0, w1, w2 = jnp.exp(l0 - top), jnp.exp(l1 - top), jnp.exp(l2 - top)
    mix = (w0 * og0[...] + w1 * og1[...] + w2 * og2[...]) / (w0 + w1 + w2)
    o_ref[...] = mix.astype(o_ref.dtype)


def _mixture_of_dilations(qkv):
    b, s, _ = qkv.shape
    n = B_GROUPS * B_HEADS
    slopes = jnp.asarray(np.array([2.0 ** (-8.0 * (i + 1) / n) for i in range(n)], np.float32))
    n_tiles = s // DIL_TILE

    def column(which, group):
        return (which * B_GROUPS + group) * B_HEADS

    def tile_spec(col):
        return pl.BlockSpec((None, DIL_TILE, B_HD), lambda i, h, t: (i, t, col + h))

    def halo_specs(col, group):
        halo = B_HALF * B_DILATIONS[group]
        per_tile = DIL_TILE // halo
        last = s // halo - 1
        return [pl.BlockSpec((None, halo, B_HD),
                             lambda i, h, t: (i, jnp.maximum(t * per_tile - 1, 0), col + h)),
                tile_spec(col),
                pl.BlockSpec((None, halo, B_HD),
                             lambda i, h, t: (i, jnp.minimum((t + 1) * per_tile, last), col + h))]

    in_specs = [pl.BlockSpec(memory_space=pltpu.SMEM)]
    in_specs += [tile_spec(column(0, g)) for g in range(B_GROUPS)]
    for g in range(B_GROUPS):
        in_specs += halo_specs(column(1, g), g) + halo_specs(column(2, g), g)
    scratch = []
    for g, d in enumerate(B_DILATIONS):
        win = (DIL_TILE + 2 * B_HALF * d, B_HD)
        if g == 0:
            scratch += [pltpu.VMEM(win, BF16)] * 2
        else:
            scratch += [pltpu.VMEM((DIL_TILE, B_HD), F32), pltpu.VMEM(win, F32), pltpu.VMEM(win, F32)]
    scratch += [pltpu.VMEM((DIL_TILE, B_HD), F32)] * (2 * B_GROUPS)
    return pl.pallas_call(
        functools.partial(_dilated_kernel, seq=s),
        grid=(b, B_HEADS, n_tiles),
        in_specs=in_specs,
        out_specs=pl.BlockSpec((None, DIL_TILE, B_HD), lambda i, h, t: (i, t, h)),
        out_shape=jax.ShapeDtypeStruct((b, s, B_WIDTH), BF16),
        scratch_shapes=scratch,
        compiler_params=_params("parallel", "parallel", "parallel"),
        name="dilated_mixture_attn",
    )(slopes, *([qkv] * (1 + 3 + 6 * B_GROUPS - 1)))


def _proj_ln_kernel(*refs, n_in):
    ins = refs[:n_in]
    ws = refs[n_in:2 * n_in]
    x_ref, g_ref, b_ref, o_ref = refs[2 * n_in:]
    y = DN_ALPHA * x_ref[...]
    for a_ref, w_ref in zip(ins, ws):
        y = y + _dot(a_ref[...], w_ref[...])
    o_ref[...] = _layer_norm(y, g_ref[...], b_ref[...])


def _proj_ln(ins, ws, x, g, b, tm=512):
    n, dm = x.shape
    n_in = len(ins)
    in_specs = ([pl.BlockSpec((tm, a.shape[1]), lambda i: (i, 0)) for a in ins]
                + [pl.BlockSpec(w.shape, lambda i: (0, 0)) for w in ws]
                + [pl.BlockSpec((tm, dm), lambda i: (i, 0)),
                   pl.BlockSpec((1, dm), lambda i: (0, 0)),
                   pl.BlockSpec((1, dm), lambda i: (0, 0))])
    return pl.pallas_call(
        functools.partial(_proj_ln_kernel, n_in=n_in),
        grid=(n // tm,),
        in_specs=in_specs,
        out_specs=pl.BlockSpec((tm, dm), lambda i: (i, 0)),
        out_shape=jax.ShapeDtypeStruct((n, dm), F32),
        compiler_params=_params("parallel"),
        name="mixer_out_proj_ln",
    )(*ins, *ws, x, g.reshape(1, dm), b.reshape(1, dm))


def _xattn_kernel(x_ref, k_ref, v_ref, wq_ref, wo_ref, g_ref, b_ref, o_ref, *maybe_ob_ref):
    x = x_ref[...]
    q = (_dot(x.astype(BF16), wq_ref[...]) * (1.0 / math.sqrt(X_HD))).astype(BF16)
    heads = []
    for h in range(X_HEADS):
        sl = slice(h * X_HD, (h + 1) * X_HD)
        s = _dot_nt(q[:, sl], k_ref[:, sl])
        m = jnp.max(s, -1, keepdims=True)
        p = jnp.exp(s - m)
        l = jnp.sum(p, -1, keepdims=True)
        heads.append((_dot(p.astype(BF16), v_ref[:, sl]) / l).astype(BF16))
    o = jnp.concatenate(heads, -1)
    y = _layer_norm(DN_ALPHA * x + _dot(o, wo_ref[...]), g_ref[...], b_ref[...])
    o_ref[...] = y
    for ob_ref in maybe_ob_ref:
        ob_ref[...] = y.astype(BF16)


def _xattn(x, kv, wq, wo, g, b, with_bf16_copy, tm=512):
    bsz, s, dm = x.shape
    m = kv.shape[1]
    row_spec = pl.BlockSpec((None, tm, dm), lambda i, j: (i, j, 0))
    out_specs = [row_spec]
    out_shape = [jax.ShapeDtypeStruct((bsz, s, dm), F32)]
    if with_bf16_copy:
        out_specs.append(row_spec)
        out_shape.append(jax.ShapeDtypeStruct((bsz, s, dm), BF16))
    return pl.pallas_call(
        _xattn_kernel,
        grid=(bsz, s // tm),
        in_specs=[row_spec,
                  pl.BlockSpec((None, m, dm), lambda i, j: (i, 0, 0)),
                  pl.BlockSpec((None, m, dm), lambda i, j: (i, 0, 1)),
                  pl.BlockSpec((dm, dm), lambda i, j: (0, 0)),
                  pl.BlockSpec((dm, dm), lambda i, j: (0, 0)),
                  pl.BlockSpec((1, dm), lambda i, j: (0, 0)),
                  pl.BlockSpec((1, dm), lambda i, j: (0, 0))],
        out_specs=out_specs,
        out_shape=out_shape,
        compiler_params=_params("parallel", "parallel"),
        name="memory_xattn_ln",
    )(x, kv, kv, wq, wo, g.reshape(1, dm), b.reshape(1, dm))


def _ffn_kernel(x_ref, w1_ref, w3_ref, w2_ref, g_ref, b_ref, o_ref, acc):
    j = pl.program_id(1)

    @pl.when(j == 0)
    def _():
        acc[...] = DN_ALPHA * x_ref[...]

    xb = x_ref[...].astype(BF16)
    a = _dot(xb, w1_ref[...])
    gate = _dot(xb, w3_ref[...])
    h = (a * jax.nn.sigmoid(a) * gate).astype(BF16)
    acc[...] += _dot(h, w2_ref[...])

    @pl.when(j == pl.num_programs(1) - 1)
    def _():
        o_ref[...] = _layer_norm(acc[...], g_ref[...], b_ref[...])


def _ffn(x, w13, w2, g, b, tm=512, tf=1408):
    n, dm = x.shape
    ff = w2.shape[0]
    nf = ff // tf
    return pl.pallas_call(
        _ffn_kernel,
        grid=(n // tm, nf),
        in_specs=[pl.BlockSpec((tm, dm), lambda i, j: (i, 0)),
                  pl.BlockSpec((dm, tf), lambda i, j: (0, j)),
                  pl.BlockSpec((dm, tf), lambda i, j: (0, nf + j)),
                  pl.BlockSpec((tf, dm), lambda i, j: (j, 0)),
                  pl.BlockSpec((1, dm), lambda i, j: (0, 0)),
                  pl.BlockSpec((1, dm), lambda i, j: (0, 0))],
        out_specs=pl.BlockSpec((tm, dm), lambda i, j: (i, 0)),
        out_shape=jax.ShapeDtypeStruct((n, dm), F32),
        scratch_shapes=[pltpu.VMEM((tm, dm), F32)],
        compiler_params=_params("parallel", "arbitrary"),
        name="dense_swiglu_ln",
    )(x, w13, w13, w2, g.reshape(1, dm), b.reshape(1, dm))


def _rope_tables(seq_len):
    rows = seq_len // GRID_W
    row = jnp.broadcast_to(jnp.arange(rows, dtype=F32)[:, None], (rows, GRID_W)).reshape(-1)
    col = jnp.broadcast_to(jnp.arange(GRID_W, dtype=F32)[None, :], (rows, GRID_W)).reshape(-1)
    axis_dim = C_HD // 2
    freqs = ROPE_THETA ** (-jnp.arange(0, axis_dim, 2, dtype=F32) / axis_dim)
    ang_r = row[:, None] * freqs[None, :]
    ang_c = col[:, None] * freqs[None, :]
    ang = jnp.concatenate([ang_r, ang_r, ang_c, ang_c], -1)
    cos, sin = jnp.cos(ang), jnp.sin(ang)
    first = (jnp.arange(C_HD) % axis_dim) < axis_dim // 2
    sin_up = jnp.where(first[None, :], -sin, 0.0)
    sin_dn = jnp.where(first[None, :], 0.0, sin)
    return cos, sin_up, sin_dn


def _odd_in_kernel(x_ref, w_ref, qn_ref, kn_ref, cos_ref, su_ref, sd_ref, qt_ref, k_ref, vt_ref):
    xb = x_ref[...].astype(BF16)
    cos, su, sd = cos_ref[...], su_ref[...], sd_ref[...]
    quarter = C_HD // 4
    q_scale = math.log2(math.e) / math.sqrt(C_HD)

    def norm_rope(h, gain):
        h = h * lax.rsqrt(jnp.mean(h * h, -1, keepdims=True) + RMS_EPS) * gain
        up = pltpu.roll(h, C_HD - quarter, 1)
        dn = pltpu.roll(h, quarter, 1)
        return h * cos + up * su + dn * sd

    nq, nk = C_HEADS * C_HD, C_KV_HEADS * C_HD
    for i in range(C_HEADS):
        h = _dot(xb, w_ref[:, i * C_HD:(i + 1) * C_HD])
        qt_ref[i * C_HD:(i + 1) * C_HD, :] = (norm_rope(h, qn_ref[...]) * q_scale).T.astype(BF16)
    for i in range(C_KV_HEADS):
        h = _dot(xb, w_ref[:, nq + i * C_HD:nq + (i + 1) * C_HD])
        k_ref[:, i * C_HD:(i + 1) * C_HD] = norm_rope(h, kn_ref[...]).astype(BF16)
        v = _dot(xb, w_ref[:, nq + nk + i * C_HD:nq + nk + (i + 1) * C_HD])
        vt_ref[i * C_HD:(i + 1) * C_HD, :] = v.T.astype(BF16)


def _odd_in_proj(x, w, qn, kn, tm=512):
    b, s, dm = x.shape
    nq, nk = C_HEADS * C_HD, C_KV_HEADS * C_HD
    cos, su, sd = _rope_tables(s)
    tab = pl.BlockSpec((tm, C_HD), lambda i, j: (j, 0))
    vec = pl.BlockSpec((1, C_HD), lambda i, j: (0, 0))
    return pl.pallas_call(
        _odd_in_kernel,
        grid=(b, s // tm),
        in_specs=[pl.BlockSpec((None, tm, dm), lambda i, j: (i, j, 0)),
                  pl.BlockSpec(w.shape, lambda i, j: (0, 0)),
                  vec, vec, tab, tab, tab],
        out_specs=[pl.BlockSpec((None, nq, tm), lambda i, j: (i, 0, j)),
                   pl.BlockSpec((None, tm, nk), lambda i, j: (i, j, 0)),
                   pl.BlockSpec((None, nk, tm), lambda i, j: (i, 0, j))],
        out_shape=[jax.ShapeDtypeStruct((b, nq, s), BF16),
                   jax.ShapeDtypeStruct((b, s, nk), BF16),
                   jax.ShapeDtypeStruct((b, nk, s), BF16)],
        compiler_params=_params("parallel", "parallel"),
        name="gqa_in_proj_norm_rope",
    )(x, w, qn.reshape(1, C_HD), kn.reshape(1, C_HD), cos, su, sd)


def _gqa_kernel(qt_ref, k_ref, vt_ref, o_ref, acc_sc, st_sc, *, seq, tq, tk):
    qt = jnp.concatenate([qt_ref[h * C_HD:(h + 1) * C_HD, :] for h in range(C_GROUP)], 1)
    cols = C_GROUP * tq
    n = seq // tk
    acc_sc[...] = jnp.zeros(acc_sc.shape, F32)

    def scores(kc):
        k0 = pl.multiple_of(kc * tk, tk)
        return _dot(k_ref[pl.ds(k0, tk), :], qt)

    def consume(kc, st, m_old, l_old):
        k0 = pl.multiple_of(kc * tk, tk)
        m_new = jnp.maximum(m_old, jnp.max(st, 0, keepdims=True))
        alpha = jnp.exp2(m_old - m_new)
        p = jnp.exp2(st - m_new)
        l_new = alpha * l_old + jnp.sum(p, 0, keepdims=True)
        acc_sc[...] = alpha * acc_sc[...] + _dot(vt_ref[:, pl.ds(k0, tk)], p.astype(BF16))
        return m_new, l_new

    st_a, st_b = st_sc.at[0], st_sc.at[1]
    st_a[...] = scores(0)

    def body(i, carry):
        kc = 2 * i
        st_b[...] = scores(kc + 1)
        m, l = consume(kc, st_a[...], *carry)
        st_a[...] = scores(jnp.minimum(kc + 2, n - 1))
        return consume(kc + 1, st_b[...], m, l)

    init = (jnp.full((1, cols), -jnp.inf, F32), jnp.zeros((1, cols), F32))
    _, l = lax.fori_loop(0, n // 2, body, init)
    out = acc_sc[...] / l
    for h in range(C_GROUP):
        o_ref[:, h * C_HD:(h + 1) * C_HD] = out[:, h * tq:(h + 1) * tq].T.astype(o_ref.dtype)


def _gqa(qt, k, vt, tq=512, tk=512):
    b, nq, s = qt.shape
    gw = C_GROUP * C_HD
    assert (s // tk) % 2 == 0
    return pl.pallas_call(
        functools.partial(_gqa_kernel, seq=s, tq=tq, tk=tk),
        grid=(b, C_KV_HEADS, s // tq),
        in_specs=[pl.BlockSpec((None, gw, tq), lambda i, h, j: (i, h, j)),
                  pl.BlockSpec((None, s, C_HD), lambda i, h, j: (i, 0, h)),
                  pl.BlockSpec((None, C_HD, s), lambda i, h, j: (i, h, 0))],
        out_specs=pl.BlockSpec((None, tq, gw), lambda i, h, j: (i, j, h)),
        out_shape=jax.ShapeDtypeStruct((b, s, nq), BF16),
        scratch_shapes=[pltpu.VMEM((C_HD, C_GROUP * tq), F32),
                        pltpu.VMEM((2, tk, C_GROUP * tq), F32)],
        compiler_params=_params("parallel", "parallel", "parallel"),
        name="gqa_flash",
    )(qt, k, vt)


TOP_K = 2
MOE_CHUNK = 512
MOE_TILE = 256
MOE_SPAN = MOE_CHUNK // MOE_TILE + 1
MOE_GROUP = 1024
UNSELECTED = -float(2 ** 26)


def _router_kernel(x_ref, w_ref, gate_ref, sel_ref):
    logits = jnp.dot(x_ref[...], w_ref[...], preferred_element_type=F32,
                     precision=lax.Precision.HIGHEST)
    lane = lax.broadcasted_iota(jnp.int32, logits.shape, 1)
    logits = jnp.where(lane < N_EXPERTS, logits, -jnp.inf)
    m1 = jnp.max(logits, -1, keepdims=True)
    i1 = jnp.min(jnp.where(logits == m1, lane, LANES), -1, keepdims=True)
    rest = jnp.where(lane == i1, -jnp.inf, logits)
    m2 = jnp.max(rest, -1, keepdims=True)
    i2 = jnp.min(jnp.where(rest == m2, lane, LANES), -1, keepdims=True)
    e = jnp.exp(m2 - m1)
    g1 = 1.0 / (1.0 + e)
    g2 = e / (1.0 + e)
    gate_ref[...] = jnp.where(lane == i1, g1, 0.0) + jnp.where(lane == i2, g2, 0.0)
    sel_ref[...] = jnp.where((lane == i1) | (lane == i2), 1.0, 0.0).astype(BF16)


def _router(x, w):
    n, dm = x.shape
    tm = MOE_CHUNK
    wp = jnp.zeros((dm, LANES), F32).at[:, :N_EXPERTS].set(w)
    return pl.pallas_call(
        _router_kernel,
        grid=(n // tm,),
        in_specs=[pl.BlockSpec((tm, dm), lambda i: (i, 0)),
                  pl.BlockSpec((dm, LANES), lambda i: (0, 0))],
        out_specs=[pl.BlockSpec((tm, LANES), lambda i: (i, 0))] * 2,
        out_shape=[jax.ShapeDtypeStruct((n, LANES), F32), jax.ShapeDtypeStruct((n, LANES), BF16)],
        compiler_params=_params("parallel"),
        name="moe_router_top2",
    )(x, wp)


def _rank_kernel(sel_ref, rank_ref, rank_t_ref, base_ref, cnt_ref, carry):
    @pl.when(pl.program_id(0) == 0)
    def _():
        carry[...] = jnp.zeros(carry.shape, F32)

    sel = sel_ref[...]
    n = sel.shape[0]
    row = lax.broadcasted_iota(jnp.int32, (n, n), 0)
    col = lax.broadcasted_iota(jnp.int32, (n, n), 1)
    earlier = jnp.where(row > col, 1.0, 0.0).astype(BF16)
    before = _dot(earlier, sel) + carry[...]
    chosen = sel.astype(F32)
    rank = jnp.where(chosen > 0.0, before, UNSELECTED)
    rank_ref[...] = rank
    rank_t_ref[...] = rank.T[:SUBLANES, :]
    cnt = jnp.sum(chosen, 0, keepdims=True)
    base_ref[...] = carry[...]
    cnt_ref[...] = cnt
    carry[...] += cnt


def _rank(sel):
    n = sel.shape[0]
    c = n // MOE_CHUNK
    stat = pl.BlockSpec((None, 1, LANES), lambda i: (i, 0, 0))
    return pl.pallas_call(
        _rank_kernel,
        grid=(c,),
        in_specs=[pl.BlockSpec((MOE_CHUNK, LANES), lambda i: (i, 0))],
        out_specs=[pl.BlockSpec((MOE_CHUNK, LANES), lambda i: (i, 0)),
                   pl.BlockSpec((SUBLANES, MOE_CHUNK), lambda i: (0, i)),
                   stat, stat],
        out_shape=[jax.ShapeDtypeStruct((n, LANES), F32),
                   jax.ShapeDtypeStruct((SUBLANES, n), F32),
                   jax.ShapeDtypeStruct((c, 1, LANES), F32),
                   jax.ShapeDtypeStruct((c, 1, LANES), F32)],
        scratch_shapes=[pltpu.VMEM((1, LANES), F32)],
        compiler_params=_params("arbitrary"),
        name="moe_rank",
    )(sel)


def _fill_forward(values, valid):
    idx = lax.cummax(jnp.where(valid, jnp.arange(values.shape[0]), -1))
    idx = jnp.where(idx < 0, jnp.argmax(valid), idx)
    return values[idx]


def _moe_plan(base, cnt, n_group_tiles):
    counts = base[-1] + cnt[-1]
    gsize = -(-counts // MOE_GROUP) * MOE_GROUP
    gend = jnp.cumsum(gsize)
    gstart = gend - gsize
    n_used = gend[-1] // MOE_GROUP
    group = jnp.minimum(jnp.arange(n_group_tiles), n_used - 1)
    group_expert = jnp.minimum(jnp.searchsorted(gend, group * MOE_GROUP, side='right'), N_EXPERTS - 1)
    experts = (group.astype(jnp.int32), group_expert.astype(jnp.int32),
               (jnp.arange(n_group_tiles) < n_used).astype(jnp.int32))
    n_chunks = base.shape[0]
    n_tiles = n_group_tiles * (MOE_GROUP // MOE_TILE)
    clip = lambda t: jnp.clip(t, 0, n_tiles - 1).astype(jnp.int32)
    as_i32 = lambda t: t.astype(jnp.int32)
    first_slot = gstart[None, :] + base
    t_first = first_slot // MOE_TILE
    t_last = (first_slot + cnt - 1) // MOE_TILE
    span = jnp.arange(MOE_SPAN)
    tiles = t_first[..., None] + span
    valid = (cnt[..., None] > 0) & (tiles <= t_last[..., None])
    td = tiles.transpose(1, 0, 2).reshape(-1)
    vd = valid.transpose(1, 0, 2).reshape(-1)
    seen = lax.cummax(jnp.where(vd, td, -1))
    prev = jnp.concatenate([jnp.full((1,), -1, seen.dtype), seen[:-1]])
    first = vd & (td != prev)
    chunk_d = jnp.broadcast_to(jnp.arange(n_chunks)[None, :, None], (N_EXPERTS, n_chunks, MOE_SPAN)).reshape(-1)
    expert_d = jnp.broadcast_to(jnp.arange(N_EXPERTS)[:, None, None], (N_EXPERTS, n_chunks, MOE_SPAN)).reshape(-1)
    n_items = N_EXPERTS * n_chunks + n_tiles
    n_valid = jnp.sum(vd)
    where = jnp.where(vd, jnp.cumsum(vd) - 1, n_items)
    take = jnp.minimum(jnp.arange(n_items), n_valid - 1)

    def compact(values):
        return jnp.zeros((n_items,), jnp.int32).at[where].set(as_i32(values), mode='drop')[take]

    live = as_i32(jnp.arange(n_items) < n_valid)
    dispatch = (clip(compact(td)), compact(chunk_d), compact(expert_d), compact(first) * live, live)
    combine = []
    for k in range(MOE_SPAN):
        tk, vk = tiles[..., k].reshape(-1), valid[..., k].reshape(-1)
        combine += [clip(_fill_forward(tk, vk)), as_i32(vk)]
    return gstart, experts, dispatch, tuple(combine)


def _dispatch_kernel(tile_ref, chunk_ref, expert_ref, first_ref, live_ref, slot_ref, x_ref, zero_ref, o_ref):
    del chunk_ref, zero_ref
    w = pl.program_id(0)

    @pl.when(live_ref[w] == 1)
    def _():
        local = slot_ref[pl.ds(expert_ref[w], 1), :] - (tile_ref[w] * MOE_TILE).astype(F32)
        sub = lax.broadcasted_iota(jnp.int32, (MOE_TILE, MOE_CHUNK), 0).astype(F32)
        onehot = jnp.where(sub == local, 1.0, 0.0).astype(BF16)
        rows = _dot(onehot, x_ref[...])

        @pl.when(first_ref[w] == 1)
        def _():
            o_ref[...] = rows.astype(BF16)

        @pl.when(first_ref[w] == 0)
        def _():
            o_ref[...] += rows.astype(BF16)


def _dispatch(plan, slot_t, xb, n_slots):
    n, dm = xb.shape
    grid_spec = pltpu.PrefetchScalarGridSpec(
        num_scalar_prefetch=5,
        grid=(plan[0].shape[0],),
        in_specs=[pl.BlockSpec((SUBLANES, MOE_CHUNK), lambda w, tile, chunk, *_: (0, chunk[w])),
                  pl.BlockSpec((MOE_CHUNK, dm), lambda w, tile, chunk, *_: (chunk[w], 0)),
                  pl.BlockSpec(memory_space=pl.ANY)],
        out_specs=pl.BlockSpec((MOE_TILE, dm), lambda w, tile, *_: (tile[w], 0)),
    )
    return pl.pallas_call(
        _dispatch_kernel,
        grid_spec=grid_spec,
        out_shape=jax.ShapeDtypeStruct((n_slots, dm), BF16),
        input_output_aliases={7: 0},
        compiler_params=_params("arbitrary"),
        name="moe_dispatch",
    )(*plan, slot_t, xb, jnp.zeros((n_slots, dm), BF16))


def _experts_kernel(tile_ref, expert_ref, live_ref, x_ref, w1_ref, w3_ref, w2_ref, o_ref, acc):
    del tile_ref, expert_ref
    j = pl.program_id(1)

    @pl.when(live_ref[pl.program_id(0)] == 1)
    def _():
        xb = x_ref[...]
        a = _dot(xb, w1_ref[...])
        gate = _dot(xb, w3_ref[...])
        y = _dot((a * jax.nn.sigmoid(a) * gate).astype(BF16), w2_ref[...])

        @pl.when(j == 0)
        def _():
            acc[...] = y

        @pl.when(j > 0)
        def _():
            acc[...] += y

        @pl.when(j == pl.num_programs(1) - 1)
        def _():
            o_ref[...] = acc[...].astype(BF16)


def _experts(plan, xs, w13, w2, tf=896):
    n_slots, dm = xs.shape
    ff = w2.shape[1]
    nf = ff // tf

    def chunk(i, j, live):
        return j * live[i] + (nf - 1) * (1 - live[i])

    grid_spec = pltpu.PrefetchScalarGridSpec(
        num_scalar_prefetch=3,
        grid=(n_slots // MOE_GROUP, nf),
        in_specs=[pl.BlockSpec((MOE_GROUP, dm), lambda i, j, tile, te, live: (tile[i], 0)),
                  pl.BlockSpec((None, dm, tf), lambda i, j, tile, te, live: (te[i], 0, chunk(i, j, live))),
                  pl.BlockSpec((None, dm, tf), lambda i, j, tile, te, live: (te[i], 0, nf + chunk(i, j, live))),
                  pl.BlockSpec((None, tf, dm), lambda i, j, tile, te, live: (te[i], chunk(i, j, live), 0))],
        out_specs=pl.BlockSpec((MOE_GROUP, dm), lambda i, j, tile, te, live: (tile[i], 0)),
        scratch_shapes=[pltpu.VMEM((MOE_GROUP, dm), F32)],
    )
    return pl.pallas_call(
        _experts_kernel,
        grid_spec=grid_spec,
        out_shape=jax.ShapeDtypeStruct((n_slots, dm), BF16),
        compiler_params=_params("arbitrary", "arbitrary"),
        name="moe_experts_swiglu",
    )(*plan, xs, w13, w13, w2)


def _combine_kernel(*refs):
    plan = refs[:2 * MOE_SPAN]
    slot_ref, gate_ref = refs[2 * MOE_SPAN:2 * MOE_SPAN + 2]
    y_refs = refs[2 * MOE_SPAN + 2:3 * MOE_SPAN + 2]
    x_ref, g_ref, b_ref, o_ref, acc = refs[3 * MOE_SPAN + 2:]
    e = pl.program_id(1)
    w = pl.program_id(0) * N_EXPERTS + e

    @pl.when(e == 0)
    def _():
        acc[...] = DN_ALPHA * x_ref[...]

    def add_rows(tile, y_ref):
        lane = lax.broadcasted_iota(jnp.int32, (MOE_CHUNK, LANES), 1)
        slot = jnp.sum(jnp.where(lane == e, slot_ref[...], 0.0), -1, keepdims=True)
        gate = jnp.sum(jnp.where(lane == e, gate_ref[...], 0.0), -1, keepdims=True)
        col = lax.broadcasted_iota(jnp.int32, (MOE_CHUNK, MOE_TILE), 1).astype(F32)
        onehot = jnp.where(col == slot - (tile * MOE_TILE).astype(F32), 1.0, 0.0).astype(BF16)
        acc[...] += gate * _dot(onehot, y_ref[...])

    for k in range(MOE_SPAN):
        @pl.when(plan[2 * k + 1][w] == 1)
        def _(k=k):
            add_rows(plan[2 * k][w], y_refs[k])

    @pl.when(e == N_EXPERTS - 1)
    def _():
        o_ref[...] = _layer_norm(acc[...], g_ref[...], b_ref[...])


def _combine(plan, slot, gates, ys, x, g, b):
    n, dm = x.shape
    c = n // MOE_CHUNK
    tok = lambda width: pl.BlockSpec((MOE_CHUNK, width), lambda i, e, *_: (i, 0))
    vec = pl.BlockSpec((1, dm), lambda i, e, *_: (0, 0))

    def tile_spec(k):
        return pl.BlockSpec((MOE_TILE, dm), lambda i, e, *plan: (plan[2 * k][i * N_EXPERTS + e], 0))

    grid_spec = pltpu.PrefetchScalarGridSpec(
        num_scalar_prefetch=2 * MOE_SPAN,
        grid=(c, N_EXPERTS),
        in_specs=[tok(LANES), tok(LANES)] + [tile_spec(k) for k in range(MOE_SPAN)] + [tok(dm), vec, vec],
        out_specs=tok(dm),
        scratch_shapes=[pltpu.VMEM((MOE_CHUNK, dm), F32)],
    )
    return pl.pallas_call(
        _combine_kernel,
        grid_spec=grid_spec,
        out_shape=jax.ShapeDtypeStruct((n, dm), F32),
        compiler_params=_params("parallel", "arbitrary"),
        name="moe_combine_ln",
    )(*plan, slot, gates, *([ys] * MOE_SPAN), x, g.reshape(1, dm), b.reshape(1, dm))


def _moe(x, xb, router_w, w13, w2, g, b):
    n, dm = x.shape
    n_slots = TOP_K * n + N_EXPERTS * MOE_GROUP
    gates, sel = _router(x, router_w)
    rank, rank_t, base, cnt = _rank(sel)
    as_counts = lambda t: t[:, 0, :N_EXPERTS].astype(jnp.int32)
    gstart, experts_plan, dispatch_plan, combine_plan = _moe_plan(
        as_counts(base), as_counts(cnt), n_slots // MOE_GROUP)
    gstart = gstart.astype(F32)
    slot = rank + jnp.zeros((LANES,), F32).at[:N_EXPERTS].set(gstart)[None, :]
    slot_t = rank_t + gstart[:, None]
    xs = _dispatch(dispatch_plan, slot_t, xb, n_slots)
    ys = _experts(experts_plan, xs, w13, w2)
    return _combine(combine_plan, slot, gates, ys, x, g, b)


def _trunk(x, mem, w):
    b, s, dm = x.shape
    n = b * s
    mem2 = mem.reshape(-1, dm)
    for i in range(DEPTH):
        j = i // 2
        even = i % 2 == 0
        x2 = x.reshape(n, dm)
        if even:
            u, qkv = _matmul(x2, w['e_w_in'][j], (2 * CONV_CH, QKV_EVEN), (F32, BF16), "even_in_proj")
            a_out = _conv_module(u.reshape(b, s, 2 * CONV_CH), w['e_dw_w'][j], w['e_dw_b'][j],
                                 w['e_cn_g'][j], w['e_cn_b'][j])
            b_out = _mixture_of_dilations(qkv.reshape(b, s, QKV_EVEN))
            w_out = w['e_w_out'][j]
            x2 = _proj_ln([a_out.reshape(n, CONV_CH), b_out.reshape(n, B_WIDTH)],
                          [w_out[:CONV_CH], w_out[CONV_CH:]], x2, w['ln_g'][i, 0], w['ln_b'][i, 0])
        else:
            qt, k, vt = _odd_in_proj(x, w['o_w_in'][j], w['o_q_norm'][j], w['o_k_norm'][j])
            attn = _gqa(qt, k, vt)
            x2 = _proj_ln([attn.reshape(n, C_HEADS * C_HD)], [w['o_w_out'][j]], x2,
                          w['ln_g'][i, 0], w['ln_b'][i, 0])
        (kv,) = _matmul(mem2, w['x_wkv'][i], (2 * dm,), (BF16,), "memory_kv_proj", tm=256)
        outs = _xattn(x2.reshape(b, s, dm), kv.reshape(b, -1, 2 * dm), w['x_wq'][i], w['x_wo'][i],
                      w['ln_g'][i, 1], w['ln_b'][i, 1], with_bf16_copy=not even)
        x2 = outs[0].reshape(n, dm)
        if even:
            x2 = _ffn(x2, w['f_w13'][j], w['f_w2'][j], w['ln_g'][i, 2], w['ln_b'][i, 2])
        else:
            x2 = _moe(x2, outs[1].reshape(n, dm), w['m_router'][j], w['m_w13'][j], w['m_w2'][j],
                      w['ln_g'][i, 2], w['ln_b'][i, 2])
        x = x2.reshape(b, s, dm)
    return x


_MXU_WEIGHTS = ('e_w_in', 'e_w_out', 'f_w13', 'f_w2', 'o_w_in', 'o_w_out', 'm_w13', 'm_w2',
                'x_wq', 'x_wkv', 'x_wo')


def kernel(x_prompt, x_sample, mem_prompt, mem_sample, e_w_in, e_dw_w, e_dw_b, e_cn_g, e_cn_b,
           e_w_out, f_w13, f_w2, o_w_in, o_q_norm, o_k_norm, o_w_out, m_router, m_w13, m_w2,
           x_wq, x_wkv, x_wo, ln_g, ln_b):
    w = dict(e_w_in=e_w_in, e_dw_w=e_dw_w, e_dw_b=e_dw_b, e_cn_g=e_cn_g, e_cn_b=e_cn_b,
             e_w_out=e_w_out, f_w13=f_w13, f_w2=f_w2, o_w_in=o_w_in, o_q_norm=o_q_norm,
             o_k_norm=o_k_norm, o_w_out=o_w_out, m_router=m_router, m_w13=m_w13, m_w2=m_w2,
             x_wq=x_wq, x_wkv=x_wkv, x_wo=x_wo, ln_g=ln_g, ln_b=ln_b)
    for name in _MXU_WEIGHTS:
        w[name] = w[name].astype(BF16)
    return (_trunk(x_prompt, mem_prompt, w), _trunk(x_sample, mem_sample, w))
```

```python
import functools
import math

import numpy as np
import jax
import jax.numpy as jnp
from jax import lax
from jax.experimental import pallas as pl
from jax.experimental.pallas import tpu as pltpu

F32 = jnp.float32
BF16 = jnp.bfloat16

D_MODEL = 1024
DEPTH = 2
GRID_W = 64
CONV_CH = D_MODEL // 2
CONV_WIDTH = 31
B_DILATIONS = (1, 4, 16)
B_GROUPS = len(B_DILATIONS)
B_HEADS = 4
B_HD = 128
B_WIDTH = B_HEADS * B_HD
B_HALF = 64
QKV_EVEN = 3 * B_GROUPS * B_WIDTH
C_HEADS = 8
C_KV_HEADS = 2
C_GROUP = C_HEADS // C_KV_HEADS
C_HD = 128
ROPE_THETA = 10000.0
X_HEADS = 4
X_HD = D_MODEL // X_HEADS
FF_DENSE = 2816
N_EXPERTS = 8
FF_EXPERT = 3584
DN_ALPHA = (2 * DEPTH) ** 0.25
LN_EPS = 1e-5
RMS_EPS = 1e-6
NEG = -1e30

LANES = 128
SUBLANES = 8
V7X_VMEM_LIMIT = 56 * 1024 * 1024


def _params(*semantics):
    return pltpu.CompilerParams(dimension_semantics=semantics,
                                vmem_limit_bytes=V7X_VMEM_LIMIT)


def _layer_norm(y, g, b):
    mu = jnp.mean(y, -1, keepdims=True)
    yc = y - mu
    var = jnp.mean(yc * yc, -1, keepdims=True)
    return yc * lax.rsqrt(var + LN_EPS) * g + b


def _dot(a, b):
    return jnp.dot(a, b, preferred_element_type=F32)


def _dot_nt(a, b):
    return lax.dot_general(a, b, (((1,), (1,)), ((), ())), preferred_element_type=F32)


def _matmul_kernel(x_ref, w_ref, *o_refs, splits, chunk):
    xb = x_ref[...].astype(BF16)
    col = 0
    for o_ref, width in zip(o_refs, splits):
        for c in range(0, width, chunk):
            o_ref[:, c:c + chunk] = _dot(xb, w_ref[:, col + c:col + c + chunk]).astype(o_ref.dtype)
        col += width


def _matmul(x, w, splits, dtypes, name, tm=512, chunk=512):
    n, k = x.shape
    return pl.pallas_call(
        functools.partial(_matmul_kernel, splits=splits, chunk=chunk),
        grid=(n // tm,),
        in_specs=[pl.BlockSpec((tm, k), lambda i: (i, 0)),
                  pl.BlockSpec(w.shape, lambda i: (0, 0))],
        out_specs=[pl.BlockSpec((tm, s), lambda i: (i, 0)) for s in splits],
        out_shape=[jax.ShapeDtypeStruct((n, s), dt) for s, dt in zip(splits, dtypes)],
        compiler_params=_params("parallel"),
        name=name,
    )(x, w)


CONV_HALO = 16
CONV_ROWS = 64


def _conv_kernel(prev_ref, cur_ref, next_ref, w_ref, b_ref, g_ref, be_ref, o_ref, hwin, *, ts):
    s = pl.program_id(1)
    last = pl.num_programs(1) - 1

    def glu(u):
        return u[:, :CONV_CH] * jax.nn.sigmoid(u[:, CONV_CH:])

    hwin[0, 0:CONV_HALO, :] = jnp.where(s > 0, glu(prev_ref[...]), 0.0)
    hwin[0, CONV_HALO:CONV_HALO + ts, :] = glu(cur_ref[...])
    hwin[0, CONV_HALO + ts:, :] = jnp.where(s < last, glu(next_ref[...]), 0.0)
    shifted_rows = ts + 2 * CONV_HALO - SUBLANES
    for r in range(1, SUBLANES):
        hwin[r, 0:shifted_rows, :] = hwin[0, r:r + shifted_rows, :]
    pad = CONV_WIDTH // 2
    for c in range(0, ts, CONV_ROWS):
        acc = jnp.broadcast_to(b_ref[...], (CONV_ROWS, CONV_CH))
        for j in range(CONV_WIDTH):
            start = CONV_HALO + c + j - pad
            r = start % SUBLANES
            acc = acc + w_ref[j:j + 1, :] * hwin[r, start - r:start - r + CONV_ROWS, :]
        y = _layer_norm(acc, g_ref[...], be_ref[...])
        o_ref[c:c + CONV_ROWS, :] = (y * jax.nn.sigmoid(y)).astype(o_ref.dtype)


def _conv_module(u, dw_w, dw_b, cn_g, cn_b, ts=512):
    b, s, _ = u.shape
    hb = ts // CONV_HALO
    nhalo = s // CONV_HALO
    row = lambda a: a.reshape(1, CONV_CH)
    return pl.pallas_call(
        functools.partial(_conv_kernel, ts=ts),
        grid=(b, s // ts),
        in_specs=[
            pl.BlockSpec((None, CONV_HALO, 2 * CONV_CH),
                         lambda i, j: (i, jnp.maximum(j * hb - 1, 0), 0)),
            pl.BlockSpec((None, ts, 2 * CONV_CH), lambda i, j: (i, j, 0)),
            pl.BlockSpec((None, CONV_HALO, 2 * CONV_CH),
                         lambda i, j: (i, jnp.minimum((j + 1) * hb, nhalo - 1), 0)),
            pl.BlockSpec((CONV_WIDTH, CONV_CH), lambda i, j: (0, 0)),
            pl.BlockSpec((1, CONV_CH), lambda i, j: (0, 0)),
            pl.BlockSpec((1, CONV_CH), lambda i, j: (0, 0)),
            pl.BlockSpec((1, CONV_CH), lambda i, j: (0, 0)),
        ],
        out_specs=pl.BlockSpec((None, ts, CONV_CH), lambda i, j: (i, j, 0)),
        out_shape=jax.ShapeDtypeStruct((b, s, CONV_CH), BF16),
        scratch_shapes=[pltpu.VMEM((SUBLANES, ts + 2 * CONV_HALO, CONV_CH), F32)],
        compiler_params=_params("parallel", "parallel"),
        name="conv_module",
    )(u, u, u, dw_w, row(dw_b), row(cn_g), row(cn_b))


BAND_Q = 128
BAND_UNROLL = 4
DIL_TILE = 2048


def _band_block(q, k, v, slope, band_bias, key_token0, key_step, seq):
    s = _dot_nt(q, k) * (1.0 / math.sqrt(B_HD)) + slope * band_bias
    col = lax.broadcasted_iota(jnp.int32, (1, 2 * BAND_Q), 1)
    token = key_token0 + col * key_step
    in_seq = token.astype(jnp.uint32) < jnp.uint32(seq)
    s = jnp.where(in_seq, s, NEG)
    m = jnp.max(s, -1, keepdims=True)
    p = jnp.exp(s - m)
    l = jnp.sum(p, -1, keepdims=True)
    return _dot(p.astype(BF16), v) / l, m + jnp.log(l)


def _dilated_kernel(slope_ref, *refs, seq):
    q_refs = refs[0:3]
    kv_refs = refs[3:21]
    o_ref = refs[21]
    kw0, vw0, qf1, kw1, vw1, qf2, kw2, vw2, og0, og1, og2, lg0, lg1, lg2 = refs[22:]
    head = pl.program_id(1)
    t0 = pl.program_id(2) * DIL_TILE

    row = lax.broadcasted_iota(jnp.int32, (BAND_Q, 2 * BAND_Q), 0)
    col = lax.broadcasted_iota(jnp.int32, (BAND_Q, 2 * BAND_Q), 1)
    dist = jnp.abs(col - row - B_HALF)
    band_bias = jnp.where(dist <= B_HALF, -dist.astype(F32), NEG)

    def stage(win, group, which, dtype):
        halo = B_HALF * B_DILATIONS[group]
        prev_ref, cur_ref, next_ref = kv_refs[group * 6 + which * 3:group * 6 + which * 3 + 3]
        win[0:halo, :] = prev_ref[...].astype(dtype)
        win[halo:halo + DIL_TILE, :] = cur_ref[...].astype(dtype)
        win[halo + DIL_TILE:, :] = next_ref[...].astype(dtype)

    def slope_of(group):
        return slope_ref[group * B_HEADS + head] * float(B_DILATIONS[group])

    stage(kw0, 0, 0, BF16)
    stage(vw0, 0, 1, BF16)
    slope0 = slope_of(0)

    def body0(i, carry):
        i0 = pl.multiple_of(i * BAND_Q, BAND_Q)
        o, lse = _band_block(q_refs[0][pl.ds(i0, BAND_Q), :], kw0[pl.ds(i0, 2 * BAND_Q), :],
                             vw0[pl.ds(i0, 2 * BAND_Q), :], slope0, band_bias,
                             t0 + i0 - B_HALF, 1, seq)
        og0[pl.ds(i0, BAND_Q), :] = o
        lg0[pl.ds(i0, BAND_Q), :] = jnp.broadcast_to(lse, (BAND_Q, B_HD))
        return carry

    lax.fori_loop(0, DIL_TILE // BAND_Q, body0, 0, unroll=BAND_UNROLL)

    for group, qf, kw, vw, og, lg in ((1, qf1, kw1, vw1, og1, lg1), (2, qf2, kw2, vw2, og2, lg2)):
        d = B_DILATIONS[group]
        nq = DIL_TILE // d
        qf[...] = q_refs[group][...].astype(F32)
        stage(kw, group, 0, F32)
        stage(vw, group, 1, F32)
        slope = slope_of(group)

        def body(r, carry, d=d, nq=nq, qf=qf, kw=kw, vw=vw, og=og, lg=lg, slope=slope):
            qr = qf[pl.ds(r, nq, stride=d), :].astype(BF16)
            kr = kw[pl.ds(r, nq + 2 * B_HALF, stride=d), :].astype(BF16)
            vr = vw[pl.ds(r, nq + 2 * B_HALF, stride=d), :].astype(BF16)
            for i0 in range(0, nq, BAND_Q):
                o, lse = _band_block(qr[i0:i0 + BAND_Q], kr[i0:i0 + 2 * BAND_Q],
                                     vr[i0:i0 + 2 * BAND_Q], slope, band_bias,
                                     t0 + r + (i0 - B_HALF) * d, d, seq)
                og[pl.ds(r + i0 * d, BAND_Q, stride=d), :] = o
                lg[pl.ds(r + i0 * d, BAND_Q, stride=d), :] = jnp.broadcast_to(lse, (BAND_Q, B_HD))
            return carry

        lax.fori_loop(0, d, body, 0, unroll=max(1, BAND_UNROLL * BAND_Q * d // DIL_TILE))

    l0, l1, l2 = lg0[...], lg1[...], lg2[...]
    top = jnp.maximum(jnp.maximum(l0, l1), l2)
    w0, w1, w2 = jnp.exp(l0 - top), jnp.exp(l1 - top), jnp.exp(l2 - top)
    mix = (w0 * og0[...] + w1 * og1[...] + w2 * og2[...]) / (w0 + w1 + w2)
    o_ref[...] = mix.astype(o_ref.dtype)


def _mixture_of_dilations(qkv):
    b, s, _ = qkv.shape
    n = B_GROUPS * B_HEADS
    slopes = jnp.asarray(np.array([2.0 ** (-8.0 * (i + 1) / n) for i in range(n)], np.float32))
    n_tiles = s // DIL_TILE

    def column(which, group):
        return (which * B_GROUPS + group) * B_HEADS

    def tile_spec(col):
        return pl.BlockSpec((None, DIL_TILE, B_HD), lambda i, h, t: (i, t, col + h))

    def halo_specs(col, group):
        halo = B_HALF * B_DILATIONS[group]
        per_tile = DIL_TILE // halo
        last = s // halo - 1
        return [pl.BlockSpec((None, halo, B_HD),
                             lambda i, h, t: (i, jnp.maximum(t * per_tile - 1, 0), col + h)),
                tile_spec(col),
                pl.BlockSpec((None, halo, B_HD),
                             lambda i, h, t: (i, jnp.minimum((t + 1) * per_tile, last), col + h))]

    in_specs = [pl.BlockSpec(memory_space=pltpu.SMEM)]
    in_specs += [tile_spec(column(0, g)) for g in range(B_GROUPS)]
    for g in range(B_GROUPS):
        in_specs += halo_specs(column(1, g), g) + halo_specs(column(2, g), g)
    scratch = []
    for g, d in enumerate(B_DILATIONS):
        win = (DIL_TILE + 2 * B_HALF * d, B_HD)
        if g == 0:
            scratch += [pltpu.VMEM(win, BF16)] * 2
        else:
            scratch += [pltpu.VMEM((DIL_TILE, B_HD), F32), pltpu.VMEM(win, F32), pltpu.VMEM(win, F32)]
    scratch += [pltpu.VMEM((DIL_TILE, B_HD), F32)] * (2 * B_GROUPS)
    return pl.pallas_call(
        functools.partial(_dilated_kernel, seq=s),
        grid=(b, B_HEADS, n_tiles),
        in_specs=in_specs,
        out_specs=pl.BlockSpec((None, DIL_TILE, B_HD), lambda i, h, t: (i, t, h)),
        out_shape=jax.ShapeDtypeStruct((b, s, B_WIDTH), BF16),
        scratch_shapes=scratch,
        compiler_params=_params("parallel", "parallel", "parallel"),
        name="dilated_mixture_attn",
    )(slopes, *([qkv] * (1 + 3 + 6 * B_GROUPS - 1)))


def _post_mixer_kernel(*refs, n_in):
    ins = refs[:n_in]
    ws = refs[n_in:2 * n_in]
    x_ref, g0_ref, b0_ref, k_ref, v_ref, wq_ref, wo_ref, g_ref, b_ref, o_ref = refs[2 * n_in:2 * n_in + 10]
    maybe_ob_ref = refs[2 * n_in + 10:]
    y = DN_ALPHA * x_ref[...]
    for a_ref, w_ref in zip(ins, ws):
        y = y + _dot(a_ref[...], w_ref[...])
    x = _layer_norm(y, g0_ref[...], b0_ref[...])
    q = (_dot(x.astype(BF16), wq_ref[...]) * (1.0 / math.sqrt(X_HD))).astype(BF16)
    heads = []
    for h in range(X_HEADS):
        sl = slice(h * X_HD, (h + 1) * X_HD)
        s = _dot_nt(q[:, sl], k_ref[:, sl])
        m = jnp.max(s, -1, keepdims=True)
        p = jnp.exp(s - m)
        l = jnp.sum(p, -1, keepdims=True)
        heads.append((_dot(p.astype(BF16), v_ref[:, sl]) / l).astype(BF16))
    o = jnp.concatenate(heads, -1)
    y = _layer_norm(DN_ALPHA * x + _dot(o, wo_ref[...]), g_ref[...], b_ref[...])
    o_ref[...] = y
    for ob_ref in maybe_ob_ref:
        ob_ref[...] = y.astype(BF16)


def _post_mixer(ins, ws, x, ln0, kv, wq, wo, ln1, with_bf16_copy, tm=512):
    bsz, s, dm = x.shape
    m = kv.shape[1]
    rows = lambda width: pl.BlockSpec((None, tm, width), lambda i, j: (i, j, 0))
    whole = lambda a: pl.BlockSpec(a.shape, lambda i, j: (0, 0))
    vec = pl.BlockSpec((1, dm), lambda i, j: (0, 0))
    out_specs = [rows(dm)]
    out_shape = [jax.ShapeDtypeStruct((bsz, s, dm), F32)]
    if with_bf16_copy:
        out_specs.append(rows(dm))
        out_shape.append(jax.ShapeDtypeStruct((bsz, s, dm), BF16))
    as_row = lambda t: t.reshape(1, dm)
    return pl.pallas_call(
        functools.partial(_post_mixer_kernel, n_in=len(ins)),
        grid=(bsz, s // tm),
        in_specs=([rows(a.shape[-1]) for a in ins] + [whole(w) for w in ws]
                  + [rows(dm), vec, vec,
                     pl.BlockSpec((None, m, dm), lambda i, j: (i, 0, 0)),
                     pl.BlockSpec((None, m, dm), lambda i, j: (i, 0, 1)),
                     whole(wq), whole(wo), vec, vec]),
        out_specs=out_specs,
        out_shape=out_shape,
        compiler_params=_params("parallel", "parallel"),
        name="mixer_out_xattn_ln",
    )(*ins, *ws, x, as_row(ln0[0]), as_row(ln0[1]), kv, kv, wq, wo, as_row(ln1[0]), as_row(ln1[1]))


def _ffn_kernel(x_ref, w13_ref, w2_ref, g_ref, b_ref, o_ref):
    x = x_ref[...]
    xb = x.astype(BF16)
    ff = w2_ref.shape[0]
    a = _dot(xb, w13_ref[:, :ff])
    gate = _dot(xb, w13_ref[:, ff:])
    h = (a * jax.nn.sigmoid(a) * gate).astype(BF16)
    o_ref[...] = _layer_norm(DN_ALPHA * x + _dot(h, w2_ref[...]), g_ref[...], b_ref[...])


def _ffn(x, w13, w2, g, b, tm=512):
    n, dm = x.shape
    resident = lambda w: pl.BlockSpec(w.shape, lambda i: (0, 0), pipeline_mode=pl.Buffered(1))
    return pl.pallas_call(
        _ffn_kernel,
        grid=(n // tm,),
        in_specs=[pl.BlockSpec((tm, dm), lambda i: (i, 0)),
                  resident(w13), resident(w2),
                  pl.BlockSpec((1, dm), lambda i: (0, 0)),
                  pl.BlockSpec((1, dm), lambda i: (0, 0))],
        out_specs=pl.BlockSpec((tm, dm), lambda i: (i, 0)),
        out_shape=jax.ShapeDtypeStruct((n, dm), F32),
        compiler_params=_params("parallel"),
        name="dense_swiglu_ln",
    )(x, w13, w2, g.reshape(1, dm), b.reshape(1, dm))


def _rope_tables(seq_len):
    rows = seq_len // GRID_W
    row = jnp.broadcast_to(jnp.arange(rows, dtype=F32)[:, None], (rows, GRID_W)).reshape(-1)
    col = jnp.broadcast_to(jnp.arange(GRID_W, dtype=F32)[None, :], (rows, GRID_W)).reshape(-1)
    axis_dim = C_HD // 2
    freqs = ROPE_THETA ** (-jnp.arange(0, axis_dim, 2, dtype=F32) / axis_dim)
    ang_r = row[:, None] * freqs[None, :]
    ang_c = col[:, None] * freqs[None, :]
    ang = jnp.concatenate([ang_r, ang_r, ang_c, ang_c], -1)
    cos, sin = jnp.cos(ang), jnp.sin(ang)
    first = (jnp.arange(C_HD) % axis_dim) < axis_dim // 2
    sin_up = jnp.where(first[None, :], -sin, 0.0)
    sin_dn = jnp.where(first[None, :], 0.0, sin)
    return cos, sin_up, sin_dn


def _odd_in_kernel(x_ref, w_ref, qn_ref, kn_ref, cos_ref, su_ref, sd_ref, qt_ref, k_ref, vt_ref):
    xb = x_ref[...].astype(BF16)
    cos, su, sd = cos_ref[...], su_ref[...], sd_ref[...]
    quarter = C_HD // 4
    q_scale = math.log2(math.e) / math.sqrt(C_HD)

    def norm_rope(h, gain):
        h = h * lax.rsqrt(jnp.mean(h * h, -1, keepdims=True) + RMS_EPS) * gain
        up = pltpu.roll(h, C_HD - quarter, 1)
        dn = pltpu.roll(h, quarter, 1)
        return h * cos + up * su + dn * sd

    nq, nk = C_HEADS * C_HD, C_KV_HEADS * C_HD
    pair = 2 * C_HD
    for i in range(0, C_HEADS, 2):
        h2 = _dot(xb, w_ref[:, i * C_HD:i * C_HD + pair])
        for j in range(2):
            h = norm_rope(h2[:, j * C_HD:(j + 1) * C_HD], qn_ref[...]) * q_scale
            qt_ref[(i + j) * C_HD:(i + j + 1) * C_HD, :] = h.T.astype(BF16)
    for i in range(0, C_KV_HEADS, 2):
        h2 = _dot(xb, w_ref[:, nq + i * C_HD:nq + i * C_HD + pair])
        for j in range(2):
            k_ref[:, (i + j) * C_HD:(i + j + 1) * C_HD] = norm_rope(
                h2[:, j * C_HD:(j + 1) * C_HD], kn_ref[...]).astype(BF16)
        v2 = _dot(xb, w_ref[:, nq + nk + i * C_HD:nq + nk + i * C_HD + pair])
        vt_ref[i * C_HD:i * C_HD + pair, :] = v2.T.astype(BF16)


def _odd_in_proj(x, w, qn, kn, tm=512):
    b, s, dm = x.shape
    nq, nk = C_HEADS * C_HD, C_KV_HEADS * C_HD
    cos, su, sd = _rope_tables(s)
    tab = pl.BlockSpec((tm, C_HD), lambda i, j: (j, 0))
    vec = pl.BlockSpec((1, C_HD), lambda i, j: (0, 0))
    return pl.pallas_call(
        _odd_in_kernel,
        grid=(b, s // tm),
        in_specs=[pl.BlockSpec((None, tm, dm), lambda i, j: (i, j, 0)),
                  pl.BlockSpec(w.shape, lambda i, j: (0, 0)),
                  vec, vec, tab, tab, tab],
        out_specs=[pl.BlockSpec((None, nq, tm), lambda i, j: (i, 0, j)),
                   pl.BlockSpec((None, tm, nk), lambda i, j: (i, j, 0)),
                   pl.BlockSpec((None, nk, tm), lambda i, j: (i, 0, j))],
        out_shape=[jax.ShapeDtypeStruct((b, nq, s), BF16),
                   jax.ShapeDtypeStruct((b, s, nk), BF16),
                   jax.ShapeDtypeStruct((b, nk, s), BF16)],
        compiler_params=_params("parallel", "parallel"),
        name="gqa_in_proj_norm_rope",
    )(x, w, qn.reshape(1, C_HD), kn.reshape(1, C_HD), cos, su, sd)


def _gqa_kernel(qt_ref, k_ref, vt_ref, o_ref, acc_sc, st_sc, *, seq, tq, tk):
    qt = jnp.concatenate([qt_ref[h * C_HD:(h + 1) * C_HD, :] for h in range(C_GROUP)], 1)
    cols = C_GROUP * tq
    n = seq // tk
    acc_sc[...] = jnp.zeros(acc_sc.shape, F32)

    def scores(kc):
        k0 = pl.multiple_of(kc * tk, tk)
        return _dot(k_ref[pl.ds(k0, tk), :], qt)

    def consume(kc, st, m_old, l_old):
        k0 = pl.multiple_of(kc * tk, tk)
        m_new = jnp.maximum(m_old, jnp.max(st, 0, keepdims=True))
        alpha = jnp.exp2(m_old - m_new)
        p = jnp.exp2(st - m_new)
        l_new = alpha * l_old + jnp.sum(p, 0, keepdims=True)
        acc_sc[...] = alpha * acc_sc[...] + _dot(vt_ref[:, pl.ds(k0, tk)], p.astype(BF16))
        return m_new, l_new

    st_a, st_b = st_sc.at[0], st_sc.at[1]
    st_a[...] = scores(0)

    def body(i, carry):
        kc = 2 * i
        st_b[...] = scores(kc + 1)
        m, l = consume(kc, st_a[...], *carry)
        st_a[...] = scores(jnp.minimum(kc + 2, n - 1))
        return consume(kc + 1, st_b[...], m, l)

    init = (jnp.full((1, cols), -jnp.inf, F32), jnp.zeros((1, cols), F32))
    _, l = lax.fori_loop(0, n // 2, body, init)
    out = acc_sc[...] / l
    for h in range(C_GROUP):
        o_ref[:, h * C_HD:(h + 1) * C_HD] = out[:, h * tq:(h + 1) * tq].T.astype(o_ref.dtype)


def _gqa(qt, k, vt, tq=512, tk=512):
    b, nq, s = qt.shape
    gw = C_GROUP * C_HD
    assert (s // tk) % 2 == 0
    return pl.pallas_call(
        functools.partial(_gqa_kernel, seq=s, tq=tq, tk=tk),
        grid=(b, C_KV_HEADS, s // tq),
        in_specs=[pl.BlockSpec((None, gw, tq), lambda i, h, j: (i, h, j)),
                  pl.BlockSpec((None, s, C_HD), lambda i, h, j: (i, 0, h)),
                  pl.BlockSpec((None, C_HD, s), lambda i, h, j: (i, h, 0))],
        out_specs=pl.BlockSpec((None, tq, gw), lambda i, h, j: (i, j, h)),
        out_shape=jax.ShapeDtypeStruct((b, s, nq), BF16),
        scratch_shapes=[pltpu.VMEM((C_HD, C_GROUP * tq), F32),
                        pltpu.VMEM((2, tk, C_GROUP * tq), F32)],
        compiler_params=_params("parallel", "parallel", "parallel"),
        name="gqa_flash",
    )(qt, k, vt)


TOP_K = 2
MOE_CHUNK = 512
MOE_TILE = 256
MOE_SPAN = MOE_CHUNK // MOE_TILE + 1
MOE_GROUP = 1024
UNSELECTED = -float(2 ** 26)


def _router_kernel(x_ref, w_ref, gate_ref, sel_ref):
    logits = jnp.dot(x_ref[...], w_ref[...], preferred_element_type=F32,
                     precision=lax.Precision.HIGHEST)
    lane = lax.broadcasted_iota(jnp.int32, logits.shape, 1)
    logits = jnp.where(lane < N_EXPERTS, logits, -jnp.inf)
    m1 = jnp.max(logits, -1, keepdims=True)
    i1 = jnp.min(jnp.where(logits == m1, lane, LANES), -1, keepdims=True)
    rest = jnp.where(lane == i1, -jnp.inf, logits)
    m2 = jnp.max(rest, -1, keepdims=True)
    i2 = jnp.min(jnp.where(rest == m2, lane, LANES), -1, keepdims=True)
    e = jnp.exp(m2 - m1)
    g1 = 1.0 / (1.0 + e)
    g2 = e / (1.0 + e)
    gate_ref[...] = jnp.where(lane == i1, g1, 0.0) + jnp.where(lane == i2, g2, 0.0)
    sel_ref[...] = jnp.where((lane == i1) | (lane == i2), 1.0, 0.0).astype(BF16)


def _router(x, w):
    n, dm = x.shape
    tm = MOE_CHUNK
    wp = jnp.zeros((dm, LANES), F32).at[:, :N_EXPERTS].set(w)
    return pl.pallas_call(
        _router_kernel,
        grid=(n // tm,),
        in_specs=[pl.BlockSpec((tm, dm), lambda i: (i, 0)),
                  pl.BlockSpec((dm, LANES), lambda i: (0, 0))],
        out_specs=[pl.BlockSpec((tm, LANES), lambda i: (i, 0))] * 2,
        out_shape=[jax.ShapeDtypeStruct((n, LANES), F32), jax.ShapeDtypeStruct((n, LANES), BF16)],
        compiler_params=_params("parallel"),
        name="moe_router_top2",
    )(x, wp)


def _rank_kernel(sel_ref, rank_ref, rank_t_ref, base_ref, cnt_ref, carry):
    @pl.when(pl.program_id(0) == 0)
    def _():
        carry[...] = jnp.zeros(carry.shape, F32)

    sel = sel_ref[...]
    n = sel.shape[0]
    row = lax.broadcasted_iota(jnp.int32, (n, n), 0)
    col = lax.broadcasted_iota(jnp.int32, (n, n), 1)
    earlier = jnp.where(row > col, 1.0, 0.0).astype(BF16)
    before = _dot(earlier, sel) + carry[...]
    chosen = sel.astype(F32)
    rank = jnp.where(chosen > 0.0, before, UNSELECTED)
    rank_ref[...] = rank
    rank_t_ref[...] = rank.T[:SUBLANES, :]
    cnt = jnp.sum(chosen, 0, keepdims=True)
    base_ref[...] = carry[...]
    cnt_ref[...] = cnt
    carry[...] += cnt


def _rank(sel):
    n = sel.shape[0]
    c = n // MOE_CHUNK
    stat = pl.BlockSpec((None, 1, LANES), lambda i: (i, 0, 0))
    return pl.pallas_call(
        _rank_kernel,
        grid=(c,),
        in_specs=[pl.BlockSpec((MOE_CHUNK, LANES), lambda i: (i, 0))],
        out_specs=[pl.BlockSpec((MOE_CHUNK, LANES), lambda i: (i, 0)),
                   pl.BlockSpec((SUBLANES, MOE_CHUNK), lambda i: (0, i)),
                   stat, stat],
        out_shape=[jax.ShapeDtypeStruct((n, LANES), F32),
                   jax.ShapeDtypeStruct((SUBLANES, n), F32),
                   jax.ShapeDtypeStruct((c, 1, LANES), F32),
                   jax.ShapeDtypeStruct((c, 1, LANES), F32)],
        scratch_shapes=[pltpu.VMEM((1, LANES), F32)],
        compiler_params=_params("arbitrary"),
        name="moe_rank",
    )(sel)


def _fill_forward(values, valid):
    idx = lax.cummax(jnp.where(valid, jnp.arange(values.shape[0]), -1))
    idx = jnp.where(idx < 0, jnp.argmax(valid), idx)
    return values[idx]


def _moe_plan(base, cnt, n_group_tiles):
    counts = base[-1] + cnt[-1]
    gsize = -(-counts // MOE_GROUP) * MOE_GROUP
    gend = jnp.cumsum(gsize)
    gstart = gend - gsize
    n_used = gend[-1] // MOE_GROUP
    group = jnp.minimum(jnp.arange(n_group_tiles), n_used - 1)
    group_expert = jnp.minimum(jnp.searchsorted(gend, group * MOE_GROUP, side='right'), N_EXPERTS - 1)
    experts = (group.astype(jnp.int32), group_expert.astype(jnp.int32),
               (jnp.arange(n_group_tiles) < n_used).astype(jnp.int32))
    n_chunks = base.shape[0]
    n_tiles = n_group_tiles * (MOE_GROUP // MOE_TILE)
    clip = lambda t: jnp.clip(t, 0, n_tiles - 1).astype(jnp.int32)
    as_i32 = lambda t: t.astype(jnp.int32)
    first_slot = gstart[None, :] + base
    t_first = first_slot // MOE_TILE
    t_last = (first_slot + cnt - 1) // MOE_TILE
    span = jnp.arange(MOE_SPAN)
    tiles = t_first[..., None] + span
    valid = (cnt[..., None] > 0) & (tiles <= t_last[..., None])
    td = tiles.transpose(1, 0, 2).reshape(-1)
    vd = valid.transpose(1, 0, 2).reshape(-1)
    seen = lax.cummax(jnp.where(vd, td, -1))
    prev = jnp.concatenate([jnp.full((1,), -1, seen.dtype), seen[:-1]])
    first = vd & (td != prev)
    chunk_d = jnp.broadcast_to(jnp.arange(n_chunks)[None, :, None], (N_EXPERTS, n_chunks, MOE_SPAN)).reshape(-1)
    expert_d = jnp.broadcast_to(jnp.arange(N_EXPERTS)[:, None, None], (N_EXPERTS, n_chunks, MOE_SPAN)).reshape(-1)
    n_items = N_EXPERTS * n_chunks + n_tiles
    n_valid = jnp.sum(vd)
    where = jnp.where(vd, jnp.cumsum(vd) - 1, n_items)
    take = jnp.minimum(jnp.arange(n_items), n_valid - 1)

    def compact(values):
        return jnp.zeros((n_items,), jnp.int32).at[where].set(as_i32(values), mode='drop')[take]

    live = as_i32(jnp.arange(n_items) < n_valid)
    dispatch = (clip(compact(td)), compact(chunk_d), compact(expert_d), compact(first) * live, live)
    combine = []
    for k in range(MOE_SPAN):
        tk, vk = tiles[..., k].reshape(-1), valid[..., k].reshape(-1)
        combine += [clip(_fill_forward(tk, vk)), as_i32(vk)]
    return gstart, experts, dispatch, tuple(combine)


def _dispatch_kernel(tile_ref, chunk_ref, expert_ref, first_ref, live_ref, slot_ref, x_ref, zero_ref, o_ref):
    del chunk_ref, zero_ref
    w = pl.program_id(0)

    @pl.when(live_ref[w] == 1)
    def _():
        local = slot_ref[pl.ds(expert_ref[w], 1), :] - (tile_ref[w] * MOE_TILE).astype(F32)
        sub = lax.broadcasted_iota(jnp.int32, (MOE_TILE, MOE_CHUNK), 0).astype(F32)
        onehot = jnp.where(sub == local, 1.0, 0.0).astype(BF16)
        rows = _dot(onehot, x_ref[...])

        @pl.when(first_ref[w] == 1)
        def _():
            o_ref[...] = rows.astype(BF16)

        @pl.when(first_ref[w] == 0)
        def _():
            o_ref[...] += rows.astype(BF16)


def _dispatch(plan, slot_t, xb, n_slots):
    n, dm = xb.shape
    grid_spec = pltpu.PrefetchScalarGridSpec(
        num_scalar_prefetch=5,
        grid=(plan[0].shape[0],),
        in_specs=[pl.BlockSpec((SUBLANES, MOE_CHUNK), lambda w, tile, chunk, *_: (0, chunk[w])),
                  pl.BlockSpec((MOE_CHUNK, dm), lambda w, tile, chunk, *_: (chunk[w], 0)),
                  pl.BlockSpec(memory_space=pl.ANY)],
        out_specs=pl.BlockSpec((MOE_TILE, dm), lambda w, tile, *_: (tile[w], 0)),
    )
    return pl.pallas_call(
        _dispatch_kernel,
        grid_spec=grid_spec,
        out_shape=jax.ShapeDtypeStruct((n_slots, dm), BF16),
        input_output_aliases={7: 0},
        compiler_params=_params("arbitrary"),
        name="moe_dispatch",
    )(*plan, slot_t, xb, jnp.zeros((n_slots, dm), BF16))


def _experts_kernel(tile_ref, expert_ref, live_ref, x_ref, w1_ref, w3_ref, w2_ref, o_ref, acc):
    del tile_ref, expert_ref
    j = pl.program_id(1)

    @pl.when(live_ref[pl.program_id(0)] == 1)
    def _():
        xb = x_ref[...]
        a = _dot(xb, w1_ref[...])
        gate = _dot(xb, w3_ref[...])
        y = _dot((a * jax.nn.sigmoid(a) * gate).astype(BF16), w2_ref[...])

        @pl.when(j == 0)
        def _():
            acc[...] = y

        @pl.when(j > 0)
        def _():
            acc[...] += y

        @pl.when(j == pl.num_programs(1) - 1)
        def _():
            o_ref[...] = acc[...].astype(BF16)


def _experts(plan, xs, w13, w2, tf=1792):
    n_slots, dm = xs.shape
    ff = w2.shape[1]
    nf = ff // tf

    def chunk(i, j, live):
        return j * live[i] + (nf - 1) * (1 - live[i])

    grid_spec = pltpu.PrefetchScalarGridSpec(
        num_scalar_prefetch=3,
        grid=(n_slots // MOE_GROUP, nf),
        in_specs=[pl.BlockSpec((MOE_GROUP, dm), lambda i, j, tile, te, live: (tile[i], 0)),
                  pl.BlockSpec((None, dm, tf), lambda i, j, tile, te, live: (te[i], 0, chunk(i, j, live))),
                  pl.BlockSpec((None, dm, tf), lambda i, j, tile, te, live: (te[i], 0, nf + chunk(i, j, live))),
                  pl.BlockSpec((None, tf, dm), lambda i, j, tile, te, live: (te[i], chunk(i, j, live), 0))],
        out_specs=pl.BlockSpec((MOE_GROUP, dm), lambda i, j, tile, te, live: (tile[i], 0)),
        scratch_shapes=[pltpu.VMEM((MOE_GROUP, dm), F32)],
    )
    return pl.pallas_call(
        _experts_kernel,
        grid_spec=grid_spec,
        out_shape=jax.ShapeDtypeStruct((n_slots, dm), BF16),
        compiler_params=_params("arbitrary", "arbitrary"),
        name="moe_experts_swiglu",
    )(*plan, xs, w13, w13, w2)


def _combine_kernel(*refs):
    plan = refs[:2 * MOE_SPAN]
    slot_ref, gate_ref = refs[2 * MOE_SPAN:2 * MOE_SPAN + 2]
    y_refs = refs[2 * MOE_SPAN + 2:3 * MOE_SPAN + 2]
    x_ref, g_ref, b_ref, o_ref, acc = refs[3 * MOE_SPAN + 2:]
    e = pl.program_id(1)
    w = pl.program_id(0) * N_EXPERTS + e

    @pl.when(e == 0)
    def _():
        acc[...] = DN_ALPHA * x_ref[...]

    def add_rows(tile, y_ref):
        lane = lax.broadcasted_iota(jnp.int32, (MOE_CHUNK, LANES), 1)
        slot = jnp.sum(jnp.where(lane == e, slot_ref[...], 0.0), -1, keepdims=True)
        gate = jnp.sum(jnp.where(lane == e, gate_ref[...], 0.0), -1, keepdims=True)
        col = lax.broadcasted_iota(jnp.int32, (MOE_CHUNK, MOE_TILE), 1).astype(F32)
        onehot = jnp.where(col == slot - (tile * MOE_TILE).astype(F32), 1.0, 0.0).astype(BF16)
        acc[...] += gate * _dot(onehot, y_ref[...])

    for k in range(MOE_SPAN):
        @pl.when(plan[2 * k + 1][w] == 1)
        def _(k=k):
            add_rows(plan[2 * k][w], y_refs[k])

    @pl.when(e == N_EXPERTS - 1)
    def _():
        o_ref[...] = _layer_norm(acc[...], g_ref[...], b_ref[...])


def _combine(plan, slot, gates, ys, x, g, b):
    n, dm = x.shape
    c = n // MOE_CHUNK
    tok = lambda width: pl.BlockSpec((MOE_CHUNK, width), lambda i, e, *_: (i, 0))
    vec = pl.BlockSpec((1, dm), lambda i, e, *_: (0, 0))

    def tile_spec(k):
        return pl.BlockSpec((MOE_TILE, dm), lambda i, e, *plan: (plan[2 * k][i * N_EXPERTS + e], 0))

    grid_spec = pltpu.PrefetchScalarGridSpec(
        num_scalar_prefetch=2 * MOE_SPAN,
        grid=(c, N_EXPERTS),
        in_specs=[tok(LANES), tok(LANES)] + [tile_spec(k) for k in range(MOE_SPAN)] + [tok(dm), vec, vec],
        out_specs=tok(dm),
        scratch_shapes=[pltpu.VMEM((MOE_CHUNK, dm), F32)],
    )
    return pl.pallas_call(
        _combine_kernel,
        grid_spec=grid_spec,
        out_shape=jax.ShapeDtypeStruct((n, dm), F32),
        compiler_params=_params("parallel", "arbitrary"),
        name="moe_combine_ln",
    )(*plan, slot, gates, *([ys] * MOE_SPAN), x, g.reshape(1, dm), b.reshape(1, dm))


def _moe(x, xb, router_w, w13, w2, g, b):
    n, dm = x.shape
    n_slots = TOP_K * n + N_EXPERTS * MOE_GROUP
    gates, sel = _router(x, router_w)
    rank, rank_t, base, cnt = _rank(sel)
    as_counts = lambda t: t[:, 0, :N_EXPERTS].astype(jnp.int32)
    gstart, experts_plan, dispatch_plan, combine_plan = _moe_plan(
        as_counts(base), as_counts(cnt), n_slots // MOE_GROUP)
    gstart = gstart.astype(F32)
    slot = rank + jnp.zeros((LANES,), F32).at[:N_EXPERTS].set(gstart)[None, :]
    slot_t = rank_t + gstart[:, None]
    xs = _dispatch(dispatch_plan, slot_t, xb, n_slots)
    ys = _experts(experts_plan, xs, w13, w2)
    return _combine(combine_plan, slot, gates, ys, x, g, b)


def _trunk(x, mem, w):
    b, s, dm = x.shape
    n = b * s
    mem2 = mem.reshape(-1, dm)
    for i in range(DEPTH):
        j = i // 2
        even = i % 2 == 0
        x2 = x.reshape(n, dm)
        if even:
            u, qkv = _matmul(x2, w['e_w_in'][j], (2 * CONV_CH, QKV_EVEN), (F32, BF16), "even_in_proj")
            a_out = _conv_module(u.reshape(b, s, 2 * CONV_CH), w['e_dw_w'][j], w['e_dw_b'][j],
                                 w['e_cn_g'][j], w['e_cn_b'][j])
            b_out = _mixture_of_dilations(qkv.reshape(b, s, QKV_EVEN))
            w_out = w['e_w_out'][j]
            mixed, w_mix = [a_out, b_out], [w_out[:CONV_CH], w_out[CONV_CH:]]
        else:
            qt, k, vt = _odd_in_proj(x, w['o_w_in'][j], w['o_q_norm'][j], w['o_k_norm'][j])
            mixed, w_mix = [_gqa(qt, k, vt)], [w['o_w_out'][j]]
        (kv,) = _matmul(mem2, w['x_wkv'][i], (2 * dm,), (BF16,), "memory_kv_proj", tm=256)
        outs = _post_mixer(mixed, w_mix, x, (w['ln_g'][i, 0], w['ln_b'][i, 0]),
                           kv.reshape(b, -1, 2 * dm), w['x_wq'][i], w['x_wo'][i],
                           (w['ln_g'][i, 1], w['ln_b'][i, 1]), with_bf16_copy=not even)
        x2 = outs[0].reshape(n, dm)
        if even:
            x2 = _ffn(x2, w['f_w13'][j], w['f_w2'][j], w['ln_g'][i, 2], w['ln_b'][i, 2])
        else:
            x2 = _moe(x2, outs[1].reshape(n, dm), w['m_router'][j], w['m_w13'][j], w['m_w2'][j],
                      w['ln_g'][i, 2], w['ln_b'][i, 2])
        x = x2.reshape(b, s, dm)
    return x


_MXU_WEIGHTS = ('e_w_in', 'e_w_out', 'f_w13', 'f_w2', 'o_w_in', 'o_w_out', 'm_w13', 'm_w2',
                'x_wq', 'x_wkv', 'x_wo')


def kernel(x_prompt, x_sample, mem_prompt, mem_sample, e_w_in, e_dw_w, e_dw_b, e_cn_g, e_cn_b,
           e_w_out, f_w13, f_w2, o_w_in, o_q_norm, o_k_norm, o_w_out, m_router, m_w13, m_w2,
           x_wq, x_wkv, x_wo, ln_g, ln_b):
    w = dict(e_w_in=e_w_in, e_dw_w=e_dw_w, e_dw_b=e_dw_b, e_cn_g=e_cn_g, e_cn_b=e_cn_b,
             e_w_out=e_w_out, f_w13=f_w13, f_w2=f_w2, o_w_in=o_w_in, o_q_norm=o_q_norm,
             o_k_norm=o_k_norm, o_w_out=o_w_out, m_router=m_router, m_w13=m_w13, m_w2=m_w2,
             x_wq=x_wq, x_wkv=x_wkv, x_wo=x_wo, ln_g=ln_g, ln_b=ln_b)
    for name in _MXU_WEIGHTS:
        w[name] = w[name].astype(BF16)
    return (_trunk(x_prompt, mem_prompt, w), _trunk(x_sample, mem_sample, w))
```

```python
import functools
import math

import numpy as np
import jax
import jax.numpy as jnp
from jax import lax
from jax.experimental import pallas as pl
from jax.experimental.pallas import tpu as pltpu

F32 = jnp.float32
BF16 = jnp.bfloat16

D_MODEL = 1024
DEPTH = 2
GRID_W = 64
CONV_CH = D_MODEL // 2
CONV_WIDTH = 31
B_DILATIONS = (1, 4, 16)
B_GROUPS = len(B_DILATIONS)
B_HEADS = 4
B_HD = 128
B_WIDTH = B_HEADS * B_HD
B_HALF = 64
QKV_EVEN = 3 * B_GROUPS * B_WIDTH
C_HEADS = 8
C_KV_HEADS = 2
C_GROUP = C_HEADS // C_KV_HEADS
C_HD = 128
ROPE_THETA = 10000.0
X_HEADS = 4
X_HD = D_MODEL // X_HEADS
FF_DENSE = 2816
N_EXPERTS = 8
FF_EXPERT = 3584
DN_ALPHA = (2 * DEPTH) ** 0.25
LN_EPS = 1e-5
RMS_EPS = 1e-6
NEG = -1e30

LANES = 128
SUBLANES = 8
V7X_VMEM_LIMIT = 56 * 1024 * 1024


def _params(*semantics):
    return pltpu.CompilerParams(dimension_semantics=semantics,
                                vmem_limit_bytes=V7X_VMEM_LIMIT)


def _layer_norm(y, g, b):
    mu = jnp.mean(y, -1, keepdims=True)
    yc = y - mu
    var = jnp.mean(yc * yc, -1, keepdims=True)
    return yc * lax.rsqrt(var + LN_EPS) * g + b


def _dot(a, b):
    return jnp.dot(a, b, preferred_element_type=F32)


def _dot_nt(a, b):
    return lax.dot_general(a, b, (((1,), (1,)), ((), ())), preferred_element_type=F32)


def _matmul_kernel(x_ref, w_ref, *o_refs, splits, chunk):
    xb = x_ref[...].astype(BF16)
    col = 0
    for o_ref, width in zip(o_refs, splits):
        for c in range(0, width, chunk):
            o_ref[:, c:c + chunk] = _dot(xb, w_ref[:, col + c:col + c + chunk]).astype(o_ref.dtype)
        col += width


def _matmul(x, w, splits, dtypes, name, tm=512, chunk=512):
    n, k = x.shape
    return pl.pallas_call(
        functools.partial(_matmul_kernel, splits=splits, chunk=chunk),
        grid=(n // tm,),
        in_specs=[pl.BlockSpec((tm, k), lambda i: (i, 0)),
                  pl.BlockSpec(w.shape, lambda i: (0, 0))],
        out_specs=[pl.BlockSpec((tm, s), lambda i: (i, 0)) for s in splits],
        out_shape=[jax.ShapeDtypeStruct((n, s), dt) for s, dt in zip(splits, dtypes)],
        compiler_params=_params("parallel"),
        name=name,
    )(x, w)


CONV_HALO = 16
CONV_ROWS = 64


def _conv_kernel(prev_ref, cur_ref, next_ref, w_ref, b_ref, g_ref, be_ref, o_ref, hwin, *, ts):
    s = pl.program_id(1)
    last = pl.num_programs(1) - 1

    def glu(u):
        return u[:, :CONV_CH] * jax.nn.sigmoid(u[:, CONV_CH:])

    hwin[0, 0:CONV_HALO, :] = jnp.where(s > 0, glu(prev_ref[...]), 0.0)
    hwin[0, CONV_HALO:CONV_HALO + ts, :] = glu(cur_ref[...])
    hwin[0, CONV_HALO + ts:, :] = jnp.where(s < last, glu(next_ref[...]), 0.0)
    shifted_rows = ts + 2 * CONV_HALO - SUBLANES
    for r in range(1, SUBLANES):
        hwin[r, 0:shifted_rows, :] = hwin[0, r:r + shifted_rows, :]
    pad = CONV_WIDTH // 2
    for c in range(0, ts, CONV_ROWS):
        acc = jnp.broadcast_to(b_ref[...], (CONV_ROWS, CONV_CH))
        for j in range(CONV_WIDTH):
            start = CONV_HALO + c + j - pad
            r = start % SUBLANES
            acc = acc + w_ref[j:j + 1, :] * hwin[r, start - r:start - r + CONV_ROWS, :]
        y = _layer_norm(acc, g_ref[...], be_ref[...])
        o_ref[c:c + CONV_ROWS, :] = (y * jax.nn.sigmoid(y)).astype(o_ref.dtype)


def _conv_module(u, dw_w, dw_b, cn_g, cn_b, ts=512):
    b, s, _ = u.shape
    hb = ts // CONV_HALO
    nhalo = s // CONV_HALO
    row = lambda a: a.reshape(1, CONV_CH)
    return pl.pallas_call(
        functools.partial(_conv_kernel, ts=ts),
        grid=(b, s // ts),
        in_specs=[
            pl.BlockSpec((None, CONV_HALO, 2 * CONV_CH),
                         lambda i, j: (i, jnp.maximum(j * hb - 1, 0), 0)),
            pl.BlockSpec((None, ts, 2 * CONV_CH), lambda i, j: (i, j, 0)),
            pl.BlockSpec((None, CONV_HALO, 2 * CONV_CH),
                         lambda i, j: (i, jnp.minimum((j + 1) * hb, nhalo - 1), 0)),
            pl.BlockSpec((CONV_WIDTH, CONV_CH), lambda i, j: (0, 0)),
            pl.BlockSpec((1, CONV_CH), lambda i, j: (0, 0)),
            pl.BlockSpec((1, CONV_CH), lambda i, j: (0, 0)),
            pl.BlockSpec((1, CONV_CH), lambda i, j: (0, 0)),
        ],
        out_specs=pl.BlockSpec((None, ts, CONV_CH), lambda i, j: (i, j, 0)),
        out_shape=jax.ShapeDtypeStruct((b, s, CONV_CH), BF16),
        scratch_shapes=[pltpu.VMEM((SUBLANES, ts + 2 * CONV_HALO, CONV_CH), F32)],
        compiler_params=_params("parallel", "parallel"),
        name="conv_module",
    )(u, u, u, dw_w, row(dw_b), row(cn_g), row(cn_b))


BAND_Q = 128
BAND_UNROLL = 16
DIL_TILE = 2048


def _band_block(q, k, v, slope, band_bias, key_token0, key_step, seq):
    s = _dot_nt(q, k) * (1.0 / math.sqrt(B_HD)) + slope * band_bias
    col = lax.broadcasted_iota(jnp.int32, (1, 2 * BAND_Q), 1)
    token = key_token0 + col * key_step
    in_seq = token.astype(jnp.uint32) < jnp.uint32(seq)
    s = jnp.where(in_seq, s, NEG)
    m = jnp.max(s, -1, keepdims=True)
    p = jnp.exp(s - m)
    l = jnp.sum(p, -1, keepdims=True)
    return _dot(p.astype(BF16), v) / l, m + jnp.log(l)


def _dilated_kernel(slope_ref, *refs, seq):
    q_refs = refs[0:3]
    kv_refs = refs[3:21]
    o_ref = refs[21]
    kw0, vw0, qf1, kw1, vw1, qf2, kw2, vw2, og0, og1, og2, lg0, lg1, lg2 = refs[22:]
    head = pl.program_id(1)
    t0 = pl.program_id(2) * DIL_TILE

    row = lax.broadcasted_iota(jnp.int32, (BAND_Q, 2 * BAND_Q), 0)
    col = lax.broadcasted_iota(jnp.int32, (BAND_Q, 2 * BAND_Q), 1)
    dist = jnp.abs(col - row - B_HALF)
    band_bias = jnp.where(dist <= B_HALF, -dist.astype(F32), NEG)

    def stage(win, group, which, dtype):
        halo = B_HALF * B_DILATIONS[group]
        prev_ref, cur_ref, next_ref = kv_refs[group * 6 + which * 3:group * 6 + which * 3 + 3]
        win[0:halo, :] = prev_ref[...].astype(dtype)
        win[halo:halo + DIL_TILE, :] = cur_ref[...].astype(dtype)
        win[halo + DIL_TILE:, :] = next_ref[...].astype(dtype)

    def slope_of(group):
        return slope_ref[group * B_HEADS + head] * float(B_DILATIONS[group])

    stage(kw0, 0, 0, BF16)
    stage(vw0, 0, 1, BF16)
    slope0 = slope_of(0)

    def body0(i, carry):
        i0 = pl.multiple_of(i * BAND_Q, BAND_Q)
        o, lse = _band_block(q_refs[0][pl.ds(i0, BAND_Q), :], kw0[pl.ds(i0, 2 * BAND_Q), :],
                             vw0[pl.ds(i0, 2 * BAND_Q), :], slope0, band_bias,
                             t0 + i0 - B_HALF, 1, seq)
        og0[pl.ds(i0, BAND_Q), :] = o
        lg0[pl.ds(i0, BAND_Q), :] = jnp.broadcast_to(lse, (BAND_Q, B_HD))
        return carry

    lax.fori_loop(0, DIL_TILE // BAND_Q, body0, 0, unroll=BAND_UNROLL)

    for group, qf, kw, vw, og, lg in ((1, qf1, kw1, vw1, og1, lg1), (2, qf2, kw2, vw2, og2, lg2)):
        d = B_DILATIONS[group]
        nq = DIL_TILE // d
        qf[...] = q_refs[group][...].astype(F32)
        stage(kw, group, 0, F32)
        stage(vw, group, 1, F32)
        slope = slope_of(group)

        def body(r, carry, d=d, nq=nq, qf=qf, kw=kw, vw=vw, og=og, lg=lg, slope=slope):
            qr = qf[pl.ds(r, nq, stride=d), :].astype(BF16)
            kr = kw[pl.ds(r, nq + 2 * B_HALF, stride=d), :].astype(BF16)
            vr = vw[pl.ds(r, nq + 2 * B_HALF, stride=d), :].astype(BF16)
            for i0 in range(0, nq, BAND_Q):
                o, lse = _band_block(qr[i0:i0 + BAND_Q], kr[i0:i0 + 2 * BAND_Q],
                                     vr[i0:i0 + 2 * BAND_Q], slope, band_bias,
                                     t0 + r + (i0 - B_HALF) * d, d, seq)
                og[pl.ds(r + i0 * d, BAND_Q, stride=d), :] = o
                lg[pl.ds(r + i0 * d, BAND_Q, stride=d), :] = jnp.broadcast_to(lse, (BAND_Q, B_HD))
            return carry

        lax.fori_loop(0, d, body, 0, unroll=max(1, BAND_UNROLL * BAND_Q * d // DIL_TILE))

    l0, l1, l2 = lg0[...], lg1[...], lg2[...]
    top = jnp.maximum(jnp.maximum(l0, l1), l2)
    w0, w1, w2 = jnp.exp(l0 - top), jnp.exp(l1 - top), jnp.exp(l2 - top)
    mix = (w0 * og0[...] + w1 * og1[...] + w2 * og2[...]) / (w0 + w1 + w2)
    o_ref[...] = mix.astype(o_ref.dtype)


def _mixture_of_dilations(qkv):
    b, s, _ = qkv.shape
    n = B_GROUPS * B_HEADS
    slopes = jnp.asarray(np.array([2.0 ** (-8.0 * (i + 1) / n) for i in range(n)], np.float32))
    n_tiles = s // DIL_TILE

    def column(which, group):
        return (which * B_GROUPS + group) * B_HEADS

    def tile_spec(col):
        return pl.BlockSpec((None, DIL_TILE, B_HD), lambda i, h, t: (i, t, col + h))

    def halo_specs(col, group):
        halo = B_HALF * B_DILATIONS[group]
        per_tile = DIL_TILE // halo
        last = s // halo - 1
        return [pl.BlockSpec((None, halo, B_HD),
                             lambda i, h, t: (i, jnp.maximum(t * per_tile - 1, 0), col + h)),
                tile_spec(col),
                pl.BlockSpec((None, halo, B_HD),
                             lambda i, h, t: (i, jnp.minimum((t + 1) * per_tile, last), col + h))]

    in_specs = [pl.BlockSpec(memory_space=pltpu.SMEM)]
    in_specs += [tile_spec(column(0, g)) for g in range(B_GROUPS)]
    for g in range(B_GROUPS):
        in_specs += halo_specs(column(1, g), g) + halo_specs(column(2, g), g)
    scratch = []
    for g, d in enumerate(B_DILATIONS):
        win = (DIL_TILE + 2 * B_HALF * d, B_HD)
        if g == 0:
            scratch += [pltpu.VMEM(win, BF16)] * 2
        else:
            scratch += [pltpu.VMEM((DIL_TILE, B_HD), F32), pltpu.VMEM(win, F32), pltpu.VMEM(win, F32)]
    scratch += [pltpu.VMEM((DIL_TILE, B_HD), F32)] * (2 * B_GROUPS)
    return pl.pallas_call(
        functools.partial(_dilated_kernel, seq=s),
        grid=(b, B_HEADS, n_tiles),
        in_specs=in_specs,
        out_specs=pl.BlockSpec((None, DIL_TILE, B_HD), lambda i, h, t: (i, t, h)),
        out_shape=jax.ShapeDtypeStruct((b, s, B_WIDTH), BF16),
        scratch_shapes=scratch,
        compiler_params=_params("parallel", "parallel", "parallel"),
        name="dilated_mixture_attn",
    )(slopes, *([qkv] * (1 + 3 + 6 * B_GROUPS - 1)))


def _post_mixer_kernel(*refs, n_in):
    ins = refs[:n_in]
    ws = refs[n_in:2 * n_in]
    x_ref, g0_ref, b0_ref, k_ref, v_ref, wq_ref, wo_ref, g_ref, b_ref, o_ref = refs[2 * n_in:2 * n_in + 10]
    maybe_ob_ref = refs[2 * n_in + 10:]
    y = DN_ALPHA * x_ref[...]
    for a_ref, w_ref in zip(ins, ws):
        y = y + _dot(a_ref[...], w_ref[...])
    x = _layer_norm(y, g0_ref[...], b0_ref[...])
    q = (_dot(x.astype(BF16), wq_ref[...]) * (1.0 / math.sqrt(X_HD))).astype(BF16)
    heads = []
    for h in range(X_HEADS):
        sl = slice(h * X_HD, (h + 1) * X_HD)
        s = _dot_nt(q[:, sl], k_ref[:, sl])
        m = jnp.max(s, -1, keepdims=True)
        p = jnp.exp(s - m)
        l = jnp.sum(p, -1, keepdims=True)
        heads.append((_dot(p.astype(BF16), v_ref[:, sl]) / l).astype(BF16))
    o = jnp.concatenate(heads, -1)
    y = _layer_norm(DN_ALPHA * x + _dot(o, wo_ref[...]), g_ref[...], b_ref[...])
    o_ref[...] = y
    for ob_ref in maybe_ob_ref:
        ob_ref[...] = y.astype(BF16)


def _post_mixer(ins, ws, x, ln0, kv, wq, wo, ln1, with_bf16_copy, tm=512):
    bsz, s, dm = x.shape
    m = kv.shape[1]
    rows = lambda width: pl.BlockSpec((None, tm, width), lambda i, j: (i, j, 0))
    whole = lambda a: pl.BlockSpec(a.shape, lambda i, j: (0, 0))
    vec = pl.BlockSpec((1, dm), lambda i, j: (0, 0))
    out_specs = [rows(dm)]
    out_shape = [jax.ShapeDtypeStruct((bsz, s, dm), F32)]
    if with_bf16_copy:
        out_specs.append(rows(dm))
        out_shape.append(jax.ShapeDtypeStruct((bsz, s, dm), BF16))
    as_row = lambda t: t.reshape(1, dm)
    return pl.pallas_call(
        functools.partial(_post_mixer_kernel, n_in=len(ins)),
        grid=(bsz, s // tm),
        in_specs=([rows(a.shape[-1]) for a in ins] + [whole(w) for w in ws]
                  + [rows(dm), vec, vec,
                     pl.BlockSpec((None, m, dm), lambda i, j: (i, 0, 0)),
                     pl.BlockSpec((None, m, dm), lambda i, j: (i, 0, 1)),
                     whole(wq), whole(wo), vec, vec]),
        out_specs=out_specs,
        out_shape=out_shape,
        compiler_params=_params("parallel", "parallel"),
        name="mixer_out_xattn_ln",
    )(*ins, *ws, x, as_row(ln0[0]), as_row(ln0[1]), kv, kv, wq, wo, as_row(ln1[0]), as_row(ln1[1]))


def _ffn_kernel(x_ref, w13_ref, w2_ref, g_ref, b_ref, o_ref):
    x = x_ref[...]
    xb = x.astype(BF16)
    ff = w2_ref.shape[0]
    a = _dot(xb, w13_ref[:, :ff])
    gate = _dot(xb, w13_ref[:, ff:])
    h = (a * jax.nn.sigmoid(a) * gate).astype(BF16)
    o_ref[...] = _layer_norm(DN_ALPHA * x + _dot(h, w2_ref[...]), g_ref[...], b_ref[...])


def _ffn(x, w13, w2, g, b, tm=512):
    n, dm = x.shape
    resident = lambda w: pl.BlockSpec(w.shape, lambda i: (0, 0), pipeline_mode=pl.Buffered(1))
    return pl.pallas_call(
        _ffn_kernel,
        grid=(n // tm,),
        in_specs=[pl.BlockSpec((tm, dm), lambda i: (i, 0)),
                  resident(w13), resident(w2),
                  pl.BlockSpec((1, dm), lambda i: (0, 0)),
                  pl.BlockSpec((1, dm), lambda i: (0, 0))],
        out_specs=pl.BlockSpec((tm, dm), lambda i: (i, 0)),
        out_shape=jax.ShapeDtypeStruct((n, dm), F32),
        compiler_params=_params("parallel"),
        name="dense_swiglu_ln",
    )(x, w13, w2, g.reshape(1, dm), b.reshape(1, dm))


def _rope_tables(seq_len):
    rows = seq_len // GRID_W
    row = jnp.broadcast_to(jnp.arange(rows, dtype=F32)[:, None], (rows, GRID_W)).reshape(-1)
    col = jnp.broadcast_to(jnp.arange(GRID_W, dtype=F32)[None, :], (rows, GRID_W)).reshape(-1)
    axis_dim = C_HD // 2
    freqs = ROPE_THETA ** (-jnp.arange(0, axis_dim, 2, dtype=F32) / axis_dim)
    ang_r = row[:, None] * freqs[None, :]
    ang_c = col[:, None] * freqs[None, :]
    ang = jnp.concatenate([ang_r, ang_r, ang_c, ang_c], -1)
    cos, sin = jnp.cos(ang), jnp.sin(ang)
    first = (jnp.arange(C_HD) % axis_dim) < axis_dim // 2
    sin_up = jnp.where(first[None, :], -sin, 0.0)
    sin_dn = jnp.where(first[None, :], 0.0, sin)
    return cos, sin_up, sin_dn


def _odd_in_kernel(x_ref, w_ref, qn_ref, kn_ref, cos_ref, su_ref, sd_ref, qt_ref, k_ref, vt_ref):
    xb = x_ref[...].astype(BF16)
    cos, su, sd = cos_ref[...], su_ref[...], sd_ref[...]
    quarter = C_HD // 4
    q_scale = math.log2(math.e) / math.sqrt(C_HD)

    def norm_rope(h, gain):
        h = h * lax.rsqrt(jnp.mean(h * h, -1, keepdims=True) + RMS_EPS) * gain
        up = pltpu.roll(h, C_HD - quarter, 1)
        dn = pltpu.roll(h, quarter, 1)
        return h * cos + up * su + dn * sd

    nq, nk = C_HEADS * C_HD, C_KV_HEADS * C_HD
    pair = 2 * C_HD
    for i in range(0, C_HEADS, 2):
        h2 = _dot(xb, w_ref[:, i * C_HD:i * C_HD + pair])
        for j in range(2):
            h = norm_rope(h2[:, j * C_HD:(j + 1) * C_HD], qn_ref[...]) * q_scale
            qt_ref[(i + j) * C_HD:(i + j + 1) * C_HD, :] = h.T.astype(BF16)
    for i in range(0, C_KV_HEADS, 2):
        h2 = _dot(xb, w_ref[:, nq + i * C_HD:nq + i * C_HD + pair])
        for j in range(2):
            k_ref[:, (i + j) * C_HD:(i + j + 1) * C_HD] = norm_rope(
                h2[:, j * C_HD:(j + 1) * C_HD], kn_ref[...]).astype(BF16)
        v2 = _dot(xb, w_ref[:, nq + nk + i * C_HD:nq + nk + i * C_HD + pair])
        vt_ref[i * C_HD:i * C_HD + pair, :] = v2.T.astype(BF16)


def _odd_in_proj(x, w, qn, kn, tm=512):
    b, s, dm = x.shape
    nq, nk = C_HEADS * C_HD, C_KV_HEADS * C_HD
    cos, su, sd = _rope_tables(s)
    tab = pl.BlockSpec((tm, C_HD), lambda i, j: (j, 0))
    vec = pl.BlockSpec((1, C_HD), lambda i, j: (0, 0))
    return pl.pallas_call(
        _odd_in_kernel,
        grid=(b, s // tm),
        in_specs=[pl.BlockSpec((None, tm, dm), lambda i, j: (i, j, 0)),
                  pl.BlockSpec(w.shape, lambda i, j: (0, 0)),
                  vec, vec, tab, tab, tab],
        out_specs=[pl.BlockSpec((None, nq, tm), lambda i, j: (i, 0, j)),
                   pl.BlockSpec((None, tm, nk), lambda i, j: (i, j, 0)),
                   pl.BlockSpec((None, nk, tm), lambda i, j: (i, 0, j))],
        out_shape=[jax.ShapeDtypeStruct((b, nq, s), BF16),
                   jax.ShapeDtypeStruct((b, s, nk), BF16),
                   jax.ShapeDtypeStruct((b, nk, s), BF16)],
        compiler_params=_params("parallel", "parallel"),
        name="gqa_in_proj_norm_rope",
    )(x, w, qn.reshape(1, C_HD), kn.reshape(1, C_HD), cos, su, sd)


def _gqa_kernel(qt_ref, k_ref, vt_ref, o_ref, acc_sc, st_sc, *, seq, tq, tk):
    qt = jnp.concatenate([qt_ref[h * C_HD:(h + 1) * C_HD, :] for h in range(C_GROUP)], 1)
    cols = C_GROUP * tq
    n = seq // tk
    acc_sc[...] = jnp.zeros(acc_sc.shape, F32)

    def scores(kc):
        k0 = pl.multiple_of(kc * tk, tk)
        return _dot(k_ref[pl.ds(k0, tk), :], qt)

    def consume(kc, st, m_old, l_old):
        k0 = pl.multiple_of(kc * tk, tk)
        m_new = jnp.maximum(m_old, jnp.max(st, 0, keepdims=True))
        alpha = jnp.exp2(m_old - m_new)
        p = jnp.exp2(st - m_new)
        l_new = alpha * l_old + jnp.sum(p, 0, keepdims=True)
        acc_sc[...] = alpha * acc_sc[...] + _dot(vt_ref[:, pl.ds(k0, tk)], p.astype(BF16))
        return m_new, l_new

    st_a, st_b = st_sc.at[0], st_sc.at[1]
    st_a[...] = scores(0)

    def body(i, carry):
        kc = 2 * i
        st_b[...] = scores(kc + 1)
        m, l = consume(kc, st_a[...], *carry)
        st_a[...] = scores(jnp.minimum(kc + 2, n - 1))
        return consume(kc + 1, st_b[...], m, l)

    init = (jnp.full((1, cols), -jnp.inf, F32), jnp.zeros((1, cols), F32))
    _, l = lax.fori_loop(0, n // 2, body, init)
    out = acc_sc[...] / l
    for h in range(C_GROUP):
        o_ref[:, h * C_HD:(h + 1) * C_HD] = out[:, h * tq:(h + 1) * tq].T.astype(o_ref.dtype)


def _gqa(qt, k, vt, tq=512, tk=512):
    b, nq, s = qt.shape
    gw = C_GROUP * C_HD
    assert (s // tk) % 2 == 0
    return pl.pallas_call(
        functools.partial(_gqa_kernel, seq=s, tq=tq, tk=tk),
        grid=(b, C_KV_HEADS, s // tq),
        in_specs=[pl.BlockSpec((None, gw, tq), lambda i, h, j: (i, h, j)),
                  pl.BlockSpec((None, s, C_HD), lambda i, h, j: (i, 0, h)),
                  pl.BlockSpec((None, C_HD, s), lambda i, h, j: (i, h, 0))],
        out_specs=pl.BlockSpec((None, tq, gw), lambda i, h, j: (i, j, h)),
        out_shape=jax.ShapeDtypeStruct((b, s, nq), BF16),
        scratch_shapes=[pltpu.VMEM((C_HD, C_GROUP * tq), F32),
                        pltpu.VMEM((2, tk, C_GROUP * tq), F32)],
        compiler_params=_params("parallel", "parallel", "parallel"),
        name="gqa_flash",
    )(qt, k, vt)


TOP_K = 2
MOE_CHUNK = 512
MOE_TILE = 256
MOE_SPAN = MOE_CHUNK // MOE_TILE + 1
MOE_GROUP = 1024
UNSELECTED = -float(2 ** 26)


def _split_bf16(a):
    hi = a.astype(BF16)
    return hi, (a - hi.astype(F32)).astype(BF16)


def _router_kernel(x_ref, wh_ref, wl_ref, gate_ref, sel_ref):
    xh, xl = _split_bf16(x_ref[...])
    logits = _dot(xh, wh_ref[...]) + (_dot(xh, wl_ref[...]) + _dot(xl, wh_ref[...]))
    lane = lax.broadcasted_iota(jnp.int32, logits.shape, 1)
    logits = jnp.where(lane < N_EXPERTS, logits, -jnp.inf)
    m1 = jnp.max(logits, -1, keepdims=True)
    i1 = jnp.min(jnp.where(logits == m1, lane, LANES), -1, keepdims=True)
    rest = jnp.where(lane == i1, -jnp.inf, logits)
    m2 = jnp.max(rest, -1, keepdims=True)
    i2 = jnp.min(jnp.where(rest == m2, lane, LANES), -1, keepdims=True)
    e = jnp.exp(m2 - m1)
    g1 = 1.0 / (1.0 + e)
    g2 = e / (1.0 + e)
    gate_ref[...] = jnp.where(lane == i1, g1, 0.0) + jnp.where(lane == i2, g2, 0.0)
    sel_ref[...] = jnp.where((lane == i1) | (lane == i2), 1.0, 0.0).astype(BF16)


def _router(x, w):
    n, dm = x.shape
    tm = MOE_CHUNK
    wh, wl = _split_bf16(jnp.zeros((dm, LANES), F32).at[:, :N_EXPERTS].set(w))
    return pl.pallas_call(
        _router_kernel,
        grid=(n // tm,),
        in_specs=[pl.BlockSpec((tm, dm), lambda i: (i, 0)),
                  pl.BlockSpec((dm, LANES), lambda i: (0, 0)),
                  pl.BlockSpec((dm, LANES), lambda i: (0, 0))],
        out_specs=[pl.BlockSpec((tm, LANES), lambda i: (i, 0))] * 2,
        out_shape=[jax.ShapeDtypeStruct((n, LANES), F32), jax.ShapeDtypeStruct((n, LANES), BF16)],
        compiler_params=_params("parallel"),
        name="moe_router_top2",
    )(x, wh, wl)


def _rank_kernel(sel_ref, rank_ref, rank_t_ref, base_ref, cnt_ref, carry):
    @pl.when(pl.program_id(0) == 0)
    def _():
        carry[...] = jnp.zeros(carry.shape, F32)

    sel = sel_ref[...]
    n = sel.shape[0]
    row = lax.broadcasted_iota(jnp.int32, (n, n), 0)
    col = lax.broadcasted_iota(jnp.int32, (n, n), 1)
    earlier = jnp.where(row > col, 1.0, 0.0).astype(BF16)
    before = _dot(earlier, sel) + carry[...]
    chosen = sel.astype(F32)
    rank = jnp.where(chosen > 0.0, before, UNSELECTED)
    rank_ref[...] = rank
    rank_t_ref[...] = rank.T[:SUBLANES, :]
    cnt = jnp.sum(chosen, 0, keepdims=True)
    base_ref[...] = carry[...]
    cnt_ref[...] = cnt
    carry[...] += cnt


def _rank(sel):
    n = sel.shape[0]
    c = n // MOE_CHUNK
    stat = pl.BlockSpec((None, 1, LANES), lambda i: (i, 0, 0))
    return pl.pallas_call(
        _rank_kernel,
        grid=(c,),
        in_specs=[pl.BlockSpec((MOE_CHUNK, LANES), lambda i: (i, 0))],
        out_specs=[pl.BlockSpec((MOE_CHUNK, LANES), lambda i: (i, 0)),
                   pl.BlockSpec((SUBLANES, MOE_CHUNK), lambda i: (0, i)),
                   stat, stat],
        out_shape=[jax.ShapeDtypeStruct((n, LANES), F32),
                   jax.ShapeDtypeStruct((SUBLANES, n), F32),
                   jax.ShapeDtypeStruct((c, 1, LANES), F32),
                   jax.ShapeDtypeStruct((c, 1, LANES), F32)],
        scratch_shapes=[pltpu.VMEM((1, LANES), F32)],
        compiler_params=_params("arbitrary"),
        name="moe_rank",
    )(sel)


def _fill_forward(values, valid):
    idx = lax.cummax(jnp.where(valid, jnp.arange(values.shape[0]), -1))
    idx = jnp.where(idx < 0, jnp.argmax(valid), idx)
    return values[idx]


def _moe_plan(base, cnt, n_group_tiles):
    counts = base[-1] + cnt[-1]
    gsize = -(-counts // MOE_GROUP) * MOE_GROUP
    gend = jnp.cumsum(gsize)
    gstart = gend - gsize
    n_used = gend[-1] // MOE_GROUP
    group = jnp.minimum(jnp.arange(n_group_tiles), n_used - 1)
    group_expert = jnp.minimum(jnp.searchsorted(gend, group * MOE_GROUP, side='right'), N_EXPERTS - 1)
    experts = (group.astype(jnp.int32), group_expert.astype(jnp.int32),
               (jnp.arange(n_group_tiles) < n_used).astype(jnp.int32))
    n_chunks = base.shape[0]
    n_tiles = n_group_tiles * (MOE_GROUP // MOE_TILE)
    clip = lambda t: jnp.clip(t, 0, n_tiles - 1).astype(jnp.int32)
    as_i32 = lambda t: t.astype(jnp.int32)
    first_slot = gstart[None, :] + base
    t_first = first_slot // MOE_TILE
    t_last = (first_slot + cnt - 1) // MOE_TILE
    span = jnp.arange(MOE_SPAN)
    tiles = t_first[..., None] + span
    valid = (cnt[..., None] > 0) & (tiles <= t_last[..., None])
    td = tiles.transpose(1, 0, 2).reshape(-1)
    vd = valid.transpose(1, 0, 2).reshape(-1)
    seen = lax.cummax(jnp.where(vd, td, -1))
    prev = jnp.concatenate([jnp.full((1,), -1, seen.dtype), seen[:-1]])
    first = vd & (td != prev)
    chunk_d = jnp.broadcast_to(jnp.arange(n_chunks)[None, :, None], (N_EXPERTS, n_chunks, MOE_SPAN)).reshape(-1)
    expert_d = jnp.broadcast_to(jnp.arange(N_EXPERTS)[:, None, None], (N_EXPERTS, n_chunks, MOE_SPAN)).reshape(-1)
    n_items = N_EXPERTS * n_chunks + n_tiles
    n_valid = jnp.sum(vd)
    where = jnp.where(vd, jnp.cumsum(vd) - 1, n_items)
    take = jnp.minimum(jnp.arange(n_items), n_valid - 1)

    def compact(values):
        return jnp.zeros((n_items,), jnp.int32).at[where].set(as_i32(values), mode='drop')[take]

    live = as_i32(jnp.arange(n_items) < n_valid)
    dispatch = (clip(compact(td)), compact(chunk_d), compact(expert_d), compact(first) * live, live)
    combine = []
    for k in range(MOE_SPAN):
        tk, vk = tiles[..., k].reshape(-1), valid[..., k].reshape(-1)
        combine += [clip(_fill_forward(tk, vk)), as_i32(vk)]
    return gstart, experts, dispatch, tuple(combine)


def _dispatch_kernel(tile_ref, chunk_ref, expert_ref, first_ref, live_ref, slot_ref, x_ref, zero_ref, o_ref):
    del chunk_ref, zero_ref
    w = pl.program_id(0)

    @pl.when(live_ref[w] == 1)
    def _():
        local = slot_ref[pl.ds(expert_ref[w], 1), :] - (tile_ref[w] * MOE_TILE).astype(F32)
        sub = lax.broadcasted_iota(jnp.int32, (MOE_TILE, MOE_CHUNK), 0).astype(F32)
        onehot = jnp.where(sub == local, 1.0, 0.0).astype(BF16)
        rows = _dot(onehot, x_ref[...])

        @pl.when(first_ref[w] == 1)
        def _():
            o_ref[...] = rows.astype(BF16)

        @pl.when(first_ref[w] == 0)
        def _():
            o_ref[...] += rows.astype(BF16)


def _dispatch(plan, slot_t, xb, n_slots):
    n, dm = xb.shape
    grid_spec = pltpu.PrefetchScalarGridSpec(
        num_scalar_prefetch=5,
        grid=(plan[0].shape[0],),
        in_specs=[pl.BlockSpec((SUBLANES, MOE_CHUNK), lambda w, tile, chunk, *_: (0, chunk[w])),
                  pl.BlockSpec((MOE_CHUNK, dm), lambda w, tile, chunk, *_: (chunk[w], 0)),
                  pl.BlockSpec(memory_space=pl.ANY)],
        out_specs=pl.BlockSpec((MOE_TILE, dm), lambda w, tile, *_: (tile[w], 0)),
    )
    return pl.pallas_call(
        _dispatch_kernel,
        grid_spec=grid_spec,
        out_shape=jax.ShapeDtypeStruct((n_slots, dm), BF16),
        input_output_aliases={7: 0},
        compiler_params=_params("arbitrary"),
        name="moe_dispatch",
    )(*plan, slot_t, xb, jnp.zeros((n_slots, dm), BF16))


def _experts_kernel(tile_ref, expert_ref, live_ref, x_ref, w1_ref, w3_ref, w2_ref, o_ref, acc):
    del tile_ref, expert_ref
    j = pl.program_id(1)

    @pl.when(live_ref[pl.program_id(0)] == 1)
    def _():
        xb = x_ref[...]
        a = _dot(xb, w1_ref[...])
        gate = _dot(xb, w3_ref[...])
        y = _dot((a * jax.nn.sigmoid(a) * gate).astype(BF16), w2_ref[...])

        @pl.when(j == 0)
        def _():
            acc[...] = y

        @pl.when(j > 0)
        def _():
            acc[...] += y

        @pl.when(j == pl.num_programs(1) - 1)
        def _():
            o_ref[...] = acc[...].astype(BF16)


def _experts(plan, xs, w13, w2, tf=1792):
    n_slots, dm = xs.shape
    ff = w2.shape[1]
    nf = ff // tf

    def chunk(i, j, live):
        return j * live[i] + (nf - 1) * (1 - live[i])

    grid_spec = pltpu.PrefetchScalarGridSpec(
        num_scalar_prefetch=3,
        grid=(n_slots // MOE_GROUP, nf),
        in_specs=[pl.BlockSpec((MOE_GROUP, dm), lambda i, j, tile, te, live: (tile[i], 0)),
                  pl.BlockSpec((None, dm, tf), lambda i, j, tile, te, live: (te[i], 0, chunk(i, j, live))),
                  pl.BlockSpec((None, dm, tf), lambda i, j, tile, te, live: (te[i], 0, nf + chunk(i, j, live))),
                  pl.BlockSpec((None, tf, dm), lambda i, j, tile, te, live: (te[i], chunk(i, j, live), 0))],
        out_specs=pl.BlockSpec((MOE_GROUP, dm), lambda i, j, tile, te, live: (tile[i], 0)),
        scratch_shapes=[pltpu.VMEM((MOE_GROUP, dm), F32)],
    )
    return pl.pallas_call(
        _experts_kernel,
        grid_spec=grid_spec,
        out_shape=jax.ShapeDtypeStruct((n_slots, dm), BF16),
        compiler_params=_params("arbitrary", "arbitrary"),
        name="moe_experts_swiglu",
    )(*plan, xs, w13, w13, w2)


def _combine_kernel(*refs):
    plan = refs[:2 * MOE_SPAN]
    slot_ref, gate_ref = refs[2 * MOE_SPAN:2 * MOE_SPAN + 2]
    y_refs = refs[2 * MOE_SPAN + 2:3 * MOE_SPAN + 2]
    x_ref, g_ref, b_ref, o_ref, acc = refs[3 * MOE_SPAN + 2:]
    e = pl.program_id(1)
    w = pl.program_id(0) * N_EXPERTS + e

    @pl.when(e == 0)
    def _():
        acc[...] = DN_ALPHA * x_ref[...]

    lane = lax.broadcasted_iota(jnp.int32, (MOE_CHUNK, LANES), 1)
    slot = jnp.sum(jnp.where(lane == e, slot_ref[...], 0.0), -1, keepdims=True)
    gate = jnp.sum(jnp.where(lane == e, gate_ref[...], 0.0), -1, keepdims=True)

    def add_rows(tile, y_ref):
        col = lax.broadcasted_iota(jnp.int32, (MOE_CHUNK, MOE_TILE), 1).astype(F32)
        onehot = jnp.where(col == slot - (tile * MOE_TILE).astype(F32), 1.0, 0.0).astype(BF16)
        acc[...] += gate * _dot(onehot, y_ref[...])

    for k in range(MOE_SPAN):
        @pl.when(plan[2 * k + 1][w] == 1)
        def _(k=k):
            add_rows(plan[2 * k][w], y_refs[k])

    @pl.when(e == N_EXPERTS - 1)
    def _():
        o_ref[...] = _layer_norm(acc[...], g_ref[...], b_ref[...])


def _combine(plan, slot, gates, ys, x, g, b):
    n, dm = x.shape
    c = n // MOE_CHUNK
    tok = lambda width: pl.BlockSpec((MOE_CHUNK, width), lambda i, e, *_: (i, 0))
    vec = pl.BlockSpec((1, dm), lambda i, e, *_: (0, 0))

    def tile_spec(k):
        return pl.BlockSpec((MOE_TILE, dm), lambda i, e, *plan: (plan[2 * k][i * N_EXPERTS + e], 0))

    grid_spec = pltpu.PrefetchScalarGridSpec(
        num_scalar_prefetch=2 * MOE_SPAN,
        grid=(c, N_EXPERTS),
        in_specs=[tok(LANES), tok(LANES)] + [tile_spec(k) for k in range(MOE_SPAN)] + [tok(dm), vec, vec],
        out_specs=tok(dm),
        scratch_shapes=[pltpu.VMEM((MOE_CHUNK, dm), F32)],
    )
    return pl.pallas_call(
        _combine_kernel,
        grid_spec=grid_spec,
        out_shape=jax.ShapeDtypeStruct((n, dm), F32),
        compiler_params=_params("parallel", "arbitrary"),
        name="moe_combine_ln",
    )(*plan, slot, gates, *([ys] * MOE_SPAN), x, g.reshape(1, dm), b.reshape(1, dm))


def _moe(x, xb, router_w, w13, w2, g, b):
    n, dm = x.shape
    n_slots = TOP_K * n + N_EXPERTS * MOE_GROUP
    gates, sel = _router(x, router_w)
    rank, rank_t, base, cnt = _rank(sel)
    as_counts = lambda t: t[:, 0, :N_EXPERTS].astype(jnp.int32)
    gstart, experts_plan, dispatch_plan, combine_plan = _moe_plan(
        as_counts(base), as_counts(cnt), n_slots // MOE_GROUP)
    gstart = gstart.astype(F32)
    slot = rank + jnp.zeros((LANES,), F32).at[:N_EXPERTS].set(gstart)[None, :]
    slot_t = rank_t + gstart[:, None]
    xs = _dispatch(dispatch_plan, slot_t, xb, n_slots)
    ys = _experts(experts_plan, xs, w13, w2)
    return _combine(combine_plan, slot, gates, ys, x, g, b)


def _trunk(x, mem, w):
    b, s, dm = x.shape
    n = b * s
    mem2 = mem.reshape(-1, dm)
    for i in range(DEPTH):
        j = i // 2
        even = i % 2 == 0
        x2 = x.reshape(n, dm)
        if even:
            u, qkv = _matmul(x2, w['e_w_in'][j], (2 * CONV_CH, QKV_EVEN), (F32, BF16), "even_in_proj")
            a_out = _conv_module(u.reshape(b, s, 2 * CONV_CH), w['e_dw_w'][j], w['e_dw_b'][j],
                                 w['e_cn_g'][j], w['e_cn_b'][j])
            b_out = _mixture_of_dilations(qkv.reshape(b, s, QKV_EVEN))
            w_out = w['e_w_out'][j]
            mixed, w_mix = [a_out, b_out], [w_out[:CONV_CH], w_out[CONV_CH:]]
        else:
            qt, k, vt = _odd_in_proj(x, w['o_w_in'][j], w['o_q_norm'][j], w['o_k_norm'][j])
            mixed, w_mix = [_gqa(qt, k, vt)], [w['o_w_out'][j]]
        (kv,) = _matmul(mem2, w['x_wkv'][i], (2 * dm,), (BF16,), "memory_kv_proj", tm=256)
        outs = _post_mixer(mixed, w_mix, x, (w['ln_g'][i, 0], w['ln_b'][i, 0]),
                           kv.reshape(b, -1, 2 * dm), w['x_wq'][i], w['x_wo'][i],
                           (w['ln_g'][i, 1], w['ln_b'][i, 1]), with_bf16_copy=not even)
        x2 = outs[0].reshape(n, dm)
        if even:
            x2 = _ffn(x2, w['f_w13'][j], w['f_w2'][j], w['ln_g'][i, 2], w['ln_b'][i, 2])
        else:
            x2 = _moe(x2, outs[1].reshape(n, dm), w['m_router'][j], w['m_w13'][j], w['m_w2'][j],
                      w['ln_g'][i, 2], w['ln_b'][i, 2])
        x = x2.reshape(b, s, dm)
    return x


_MXU_WEIGHTS = ('e_w_in', 'e_w_out', 'f_w13', 'f_w2', 'o_w_in', 'o_w_out', 'm_w13', 'm_w2',
                'x_wq', 'x_wkv', 'x_wo')


def kernel(x_prompt, x_sample, mem_prompt, mem_sample, e_w_in, e_dw_w, e_dw_b, e_cn_g, e_cn_b,
           e_w_out, f_w13, f_w2, o_w_in, o_q_norm, o_k_norm, o_w_out, m_router, m_w13, m_w2,
           x_wq, x_wkv, x_wo, ln_g, ln_b):
    w = dict(e_w_in=e_w_in, e_dw_w=e_dw_w, e_dw_b=e_dw_b, e_cn_g=e_cn_g, e_cn_b=e_cn_b,
             e_w_out=e_w_out, f_w13=f_w13, f_w2=f_w2, o_w_in=o_w_in, o_q_norm=o_q_norm,
             o_k_norm=o_k_norm, o_w_out=o_w_out, m_router=m_router, m_w13=m_w13, m_w2=m_w2,
             x_wq=x_wq, x_wkv=x_wkv, x_wo=x_wo, ln_g=ln_g, ln_b=ln_b)
    for name in _MXU_WEIGHTS:
        w[name] = w[name].astype(BF16)
    return (_trunk(x_prompt, mem_prompt, w), _trunk(x_sample, mem_sample, w))
```

```python
import functools
import math

import numpy as np
import jax
import jax.numpy as jnp
from jax import lax
from jax.experimental import pallas as pl
from jax.experimental.pallas import tpu as pltpu

F32 = jnp.float32
BF16 = jnp.bfloat16

D_MODEL = 1024
DEPTH = 2
GRID_W = 64
CONV_CH = D_MODEL // 2
CONV_WIDTH = 31
B_DILATIONS = (1, 4, 16)
B_GROUPS = len(B_DILATIONS)
B_HEADS = 4
B_HD = 128
B_WIDTH = B_HEADS * B_HD
B_HALF = 64
QKV_EVEN = 3 * B_GROUPS * B_WIDTH
C_HEADS = 8
C_KV_HEADS = 2
C_GROUP = C_HEADS // C_KV_HEADS
C_HD = 128
ROPE_THETA = 10000.0
X_HEADS = 4
X_HD = D_MODEL // X_HEADS
FF_DENSE = 2816
N_EXPERTS = 8
FF_EXPERT = 3584
DN_ALPHA = (2 * DEPTH) ** 0.25
LN_EPS = 1e-5
RMS_EPS = 1e-6
NEG = -1e30

LANES = 128
SUBLANES = 8
V7X_VMEM_LIMIT = 56 * 1024 * 1024


def _params(*semantics):
    return pltpu.CompilerParams(dimension_semantics=semantics,
                                vmem_limit_bytes=V7X_VMEM_LIMIT)


def _layer_norm(y, g, b):
    mu = jnp.mean(y, -1, keepdims=True)
    yc = y - mu
    var = jnp.mean(yc * yc, -1, keepdims=True)
    return yc * lax.rsqrt(var + LN_EPS) * g + b


def _dot(a, b):
    return jnp.dot(a, b, preferred_element_type=F32)


def _dot_nt(a, b):
    return lax.dot_general(a, b, (((1,), (1,)), ((), ())), preferred_element_type=F32)


def _matmul_kernel(x_ref, w_ref, *o_refs, splits, chunk):
    xb = x_ref[...].astype(BF16)
    col = 0
    for o_ref, width in zip(o_refs, splits):
        for c in range(0, width, chunk):
            o_ref[:, c:c + chunk] = _dot(xb, w_ref[:, col + c:col + c + chunk]).astype(o_ref.dtype)
        col += width


def _matmul(x, w, splits, dtypes, name, tm=512, chunk=512):
    n, k = x.shape
    return pl.pallas_call(
        functools.partial(_matmul_kernel, splits=splits, chunk=chunk),
        grid=(n // tm,),
        in_specs=[pl.BlockSpec((tm, k), lambda i: (i, 0)),
                  pl.BlockSpec(w.shape, lambda i: (0, 0))],
        out_specs=[pl.BlockSpec((tm, s), lambda i: (i, 0)) for s in splits],
        out_shape=[jax.ShapeDtypeStruct((n, s), dt) for s, dt in zip(splits, dtypes)],
        compiler_params=_params("parallel"),
        name=name,
    )(x, w)


CONV_HALO = 16
CONV_ROWS = 64


def _conv_kernel(prev_ref, cur_ref, next_ref, w_ref, b_ref, g_ref, be_ref, o_ref, hwin, *, ts):
    s = pl.program_id(1)
    last = pl.num_programs(1) - 1

    def glu(u):
        return u[:, :CONV_CH] * jax.nn.sigmoid(u[:, CONV_CH:])

    hwin[0, 0:CONV_HALO, :] = jnp.where(s > 0, glu(prev_ref[...]), 0.0)
    hwin[0, CONV_HALO:CONV_HALO + ts, :] = glu(cur_ref[...])
    hwin[0, CONV_HALO + ts:, :] = jnp.where(s < last, glu(next_ref[...]), 0.0)
    shifted_rows = ts + 2 * CONV_HALO - SUBLANES
    for r in range(1, SUBLANES):
        hwin[r, 0:shifted_rows, :] = hwin[0, r:r + shifted_rows, :]
    pad = CONV_WIDTH // 2
    for c in range(0, ts, CONV_ROWS):
        acc = jnp.broadcast_to(b_ref[...], (CONV_ROWS, CONV_CH))
        for j in range(CONV_WIDTH):
            start = CONV_HALO + c + j - pad
            r = start % SUBLANES
            acc = acc + w_ref[j:j + 1, :] * hwin[r, start - r:start - r + CONV_ROWS, :]
        y = _layer_norm(acc, g_ref[...], be_ref[...])
        o_ref[c:c + CONV_ROWS, :] = (y * jax.nn.sigmoid(y)).astype(o_ref.dtype)


def _conv_module(u, dw_w, dw_b, cn_g, cn_b, ts=512):
    b, s, _ = u.shape
    hb = ts // CONV_HALO
    nhalo = s // CONV_HALO
    row = lambda a: a.reshape(1, CONV_CH)
    return pl.pallas_call(
        functools.partial(_conv_kernel, ts=ts),
        grid=(b, s // ts),
        in_specs=[
            pl.BlockSpec((None, CONV_HALO, 2 * CONV_CH),
                         lambda i, j: (i, jnp.maximum(j * hb - 1, 0), 0)),
            pl.BlockSpec((None, ts, 2 * CONV_CH), lambda i, j: (i, j, 0)),
            pl.BlockSpec((None, CONV_HALO, 2 * CONV_CH),
                         lambda i, j: (i, jnp.minimum((j + 1) * hb, nhalo - 1), 0)),
            pl.BlockSpec((CONV_WIDTH, CONV_CH), lambda i, j: (0, 0)),
            pl.BlockSpec((1, CONV_CH), lambda i, j: (0, 0)),
            pl.BlockSpec((1, CONV_CH), lambda i, j: (0, 0)),
            pl.BlockSpec((1, CONV_CH), lambda i, j: (0, 0)),
        ],
        out_specs=pl.BlockSpec((None, ts, CONV_CH), lambda i, j: (i, j, 0)),
        out_shape=jax.ShapeDtypeStruct((b, s, CONV_CH), BF16),
        scratch_shapes=[pltpu.VMEM((SUBLANES, ts + 2 * CONV_HALO, CONV_CH), F32)],
        compiler_params=_params("parallel", "parallel"),
        name="conv_module",
    )(u, u, u, dw_w, row(dw_b), row(cn_g), row(cn_b))


BAND_Q = 128
BAND_UNROLL = 16
DIL_TILE = 2048


def _band_block(q, k, v, slope, band_bias, key_token0, key_step, seq):
    s = _dot_nt(q, k) * (1.0 / math.sqrt(B_HD)) + slope * band_bias
    col = lax.broadcasted_iota(jnp.int32, (1, 2 * BAND_Q), 1)
    token = key_token0 + col * key_step
    in_seq = token.astype(jnp.uint32) < jnp.uint32(seq)
    s = jnp.where(in_seq, s, NEG)
    m = jnp.max(s, -1, keepdims=True)
    p = jnp.exp(s - m)
    l = jnp.sum(p, -1, keepdims=True)
    return _dot(p.astype(BF16), v) / l, m + jnp.log(l)


def _dilated_kernel(slope_ref, *refs, seq):
    q_refs = refs[0:3]
    kv_refs = refs[3:21]
    o_ref = refs[21]
    kw0, vw0, qf1, kw1, vw1, qf2, kw2, vw2, og0, og1, og2, lg0, lg1, lg2 = refs[22:]
    head = pl.program_id(1)
    t0 = pl.program_id(2) * DIL_TILE

    row = lax.broadcasted_iota(jnp.int32, (BAND_Q, 2 * BAND_Q), 0)
    col = lax.broadcasted_iota(jnp.int32, (BAND_Q, 2 * BAND_Q), 1)
    dist = jnp.abs(col - row - B_HALF)
    band_bias = jnp.where(dist <= B_HALF, -dist.astype(F32), NEG)

    def stage(win, group, which, dtype):
        halo = B_HALF * B_DILATIONS[group]
        prev_ref, cur_ref, next_ref = kv_refs[group * 6 + which * 3:group * 6 + which * 3 + 3]
        win[0:halo, :] = prev_ref[...].astype(dtype)
        win[halo:halo + DIL_TILE, :] = cur_ref[...].astype(dtype)
        win[halo + DIL_TILE:, :] = next_ref[...].astype(dtype)

    def slope_of(group):
        return slope_ref[group * B_HEADS + head] * float(B_DILATIONS[group])

    stage(kw0, 0, 0, BF16)
    stage(vw0, 0, 1, BF16)
    slope0 = slope_of(0)

    def body0(i, carry):
        i0 = pl.multiple_of(i * BAND_Q, BAND_Q)
        o, lse = _band_block(q_refs[0][pl.ds(i0, BAND_Q), :], kw0[pl.ds(i0, 2 * BAND_Q), :],
                             vw0[pl.ds(i0, 2 * BAND_Q), :], slope0, band_bias,
                             t0 + i0 - B_HALF, 1, seq)
        og0[pl.ds(i0, BAND_Q), :] = o
        lg0[pl.ds(i0, BAND_Q), :] = jnp.broadcast_to(lse, (BAND_Q, B_HD))
        return carry

    lax.fori_loop(0, DIL_TILE // BAND_Q, body0, 0, unroll=BAND_UNROLL)

    for group, qf, kw, vw, og, lg in ((1, qf1, kw1, vw1, og1, lg1), (2, qf2, kw2, vw2, og2, lg2)):
        d = B_DILATIONS[group]
        nq = DIL_TILE // d
        qf[...] = q_refs[group][...].astype(F32)
        stage(kw, group, 0, F32)
        stage(vw, group, 1, F32)
        slope = slope_of(group)

        def body(r, carry, d=d, nq=nq, qf=qf, kw=kw, vw=vw, og=og, lg=lg, slope=slope):
            qr = qf[pl.ds(r, nq, stride=d), :].astype(BF16)
            kr = kw[pl.ds(r, nq + 2 * B_HALF, stride=d), :].astype(BF16)
            vr = vw[pl.ds(r, nq + 2 * B_HALF, stride=d), :].astype(BF16)
            for i0 in range(0, nq, BAND_Q):
                o, lse = _band_block(qr[i0:i0 + BAND_Q], kr[i0:i0 + 2 * BAND_Q],
                                     vr[i0:i0 + 2 * BAND_Q], slope, band_bias,
                                     t0 + r + (i0 - B_HALF) * d, d, seq)
                og[pl.ds(r + i0 * d, BAND_Q, stride=d), :] = o
                lg[pl.ds(r + i0 * d, BAND_Q, stride=d), :] = jnp.broadcast_to(lse, (BAND_Q, B_HD))
            return carry

        lax.fori_loop(0, d, body, 0, unroll=max(1, BAND_UNROLL * BAND_Q * d // DIL_TILE))

    l0, l1, l2 = lg0[...], lg1[...], lg2[...]
    top = jnp.maximum(jnp.maximum(l0, l1), l2)
    w0, w1, w2 = jnp.exp(l0 - top), jnp.exp(l1 - top), jnp.exp(l2 - top)
    mix = (w0 * og0[...] + w1 * og1[...] + w2 * og2[...]) / (w0 + w1 + w2)
    o_ref[...] = mix.astype(o_ref.dtype)


def _mixture_of_dilations(qkv):
    b, s, _ = qkv.shape
    n = B_GROUPS * B_HEADS
    slopes = jnp.asarray(np.array([2.0 ** (-8.0 * (i + 1) / n) for i in range(n)], np.float32))
    n_tiles = s // DIL_TILE

    def column(which, group):
        return (which * B_GROUPS + group) * B_HEADS

    def tile_spec(col):
        return pl.BlockSpec((None, DIL_TILE, B_HD), lambda i, h, t: (i, t, col + h))

    def halo_specs(col, group):
        halo = B_HALF * B_DILATIONS[group]
        per_tile = DIL_TILE // halo
        last = s // halo - 1
        return [pl.BlockSpec((None, halo, B_HD),
                             lambda i, h, t: (i, jnp.maximum(t * per_tile - 1, 0), col + h)),
                tile_spec(col),
                pl.BlockSpec((None, halo, B_HD),
                             lambda i, h, t: (i, jnp.minimum((t + 1) * per_tile, last), col + h))]

    in_specs = [pl.BlockSpec(memory_space=pltpu.SMEM)]
    in_specs += [tile_spec(column(0, g)) for g in range(B_GROUPS)]
    for g in range(B_GROUPS):
        in_specs += halo_specs(column(1, g), g) + halo_specs(column(2, g), g)
    scratch = []
    for g, d in enumerate(B_DILATIONS):
        win = (DIL_TILE + 2 * B_HALF * d, B_HD)
        if g == 0:
            scratch += [pltpu.VMEM(win, BF16)] * 2
        else:
            scratch += [pltpu.VMEM((DIL_TILE, B_HD), F32), pltpu.VMEM(win, F32), pltpu.VMEM(win, F32)]
    scratch += [pltpu.VMEM((DIL_TILE, B_HD), F32)] * (2 * B_GROUPS)
    return pl.pallas_call(
        functools.partial(_dilated_kernel, seq=s),
        grid=(b, B_HEADS, n_tiles),
        in_specs=in_specs,
        out_specs=pl.BlockSpec((None, DIL_TILE, B_HD), lambda i, h, t: (i, t, h)),
        out_shape=jax.ShapeDtypeStruct((b, s, B_WIDTH), BF16),
        scratch_shapes=scratch,
        compiler_params=_params("parallel", "parallel", "parallel"),
        name="dilated_mixture_attn",
    )(slopes, *([qkv] * (1 + 3 + 6 * B_GROUPS - 1)))


def _post_mixer_kernel(*refs, n_in):
    ins = refs[:n_in]
    ws = refs[n_in:2 * n_in]
    x_ref, g0_ref, b0_ref, k_ref, v_ref, wq_ref, wo_ref, g_ref, b_ref, o_ref = refs[2 * n_in:2 * n_in + 10]
    maybe_ob_ref = refs[2 * n_in + 10:]
    y = DN_ALPHA * x_ref[...]
    for a_ref, w_ref in zip(ins, ws):
        y = y + _dot(a_ref[...], w_ref[...])
    x = _layer_norm(y, g0_ref[...], b0_ref[...])
    q = (_dot(x.astype(BF16), wq_ref[...]) * (1.0 / math.sqrt(X_HD))).astype(BF16)
    heads = []
    for h in range(X_HEADS):
        sl = slice(h * X_HD, (h + 1) * X_HD)
        s = _dot_nt(q[:, sl], k_ref[:, sl])
        m = jnp.max(s, -1, keepdims=True)
        p = jnp.exp(s - m)
        l = jnp.sum(p, -1, keepdims=True)
        heads.append((_dot(p.astype(BF16), v_ref[:, sl]) / l).astype(BF16))
    o = jnp.concatenate(heads, -1)
    y = _layer_norm(DN_ALPHA * x + _dot(o, wo_ref[...]), g_ref[...], b_ref[...])
    o_ref[...] = y
    for ob_ref in maybe_ob_ref:
        ob_ref[...] = y.astype(BF16)


def _post_mixer(ins, ws, x, ln0, kv, wq, wo, ln1, with_bf16_copy, tm=1024):
    bsz, s, dm = x.shape
    m = kv.shape[1]
    rows = lambda width: pl.BlockSpec((None, tm, width), lambda i, j: (i, j, 0))
    whole = lambda a: pl.BlockSpec(a.shape, lambda i, j: (0, 0))
    vec = pl.BlockSpec((1, dm), lambda i, j: (0, 0))
    out_specs = [rows(dm)]
    out_shape = [jax.ShapeDtypeStruct((bsz, s, dm), F32)]
    if with_bf16_copy:
        out_specs.append(rows(dm))
        out_shape.append(jax.ShapeDtypeStruct((bsz, s, dm), BF16))
    as_row = lambda t: t.reshape(1, dm)
    return pl.pallas_call(
        functools.partial(_post_mixer_kernel, n_in=len(ins)),
        grid=(bsz, s // tm),
        in_specs=([rows(a.shape[-1]) for a in ins] + [whole(w) for w in ws]
                  + [rows(dm), vec, vec,
                     pl.BlockSpec((None, m, dm), lambda i, j: (i, 0, 0)),
                     pl.BlockSpec((None, m, dm), lambda i, j: (i, 0, 1)),
                     whole(wq), whole(wo), vec, vec]),
        out_specs=out_specs,
        out_shape=out_shape,
        compiler_params=_params("parallel", "parallel"),
        name="mixer_out_xattn_ln",
    )(*ins, *ws, x, as_row(ln0[0]), as_row(ln0[1]), kv, kv, wq, wo, as_row(ln1[0]), as_row(ln1[1]))


def _ffn_kernel(x_ref, w13_ref, w2_ref, g_ref, b_ref, o_ref):
    x = x_ref[...]
    xb = x.astype(BF16)
    ff = w2_ref.shape[0]
    a = _dot(xb, w13_ref[:, :ff])
    gate = _dot(xb, w13_ref[:, ff:])
    h = (a * jax.nn.sigmoid(a) * gate).astype(BF16)
    o_ref[...] = _layer_norm(DN_ALPHA * x + _dot(h, w2_ref[...]), g_ref[...], b_ref[...])


def _ffn(x, w13, w2, g, b, tm=512):
    n, dm = x.shape
    resident = lambda w: pl.BlockSpec(w.shape, lambda i: (0, 0), pipeline_mode=pl.Buffered(1))
    return pl.pallas_call(
        _ffn_kernel,
        grid=(n // tm,),
        in_specs=[pl.BlockSpec((tm, dm), lambda i: (i, 0)),
                  resident(w13), resident(w2),
                  pl.BlockSpec((1, dm), lambda i: (0, 0)),
                  pl.BlockSpec((1, dm), lambda i: (0, 0))],
        out_specs=pl.BlockSpec((tm, dm), lambda i: (i, 0)),
        out_shape=jax.ShapeDtypeStruct((n, dm), F32),
        compiler_params=_params("parallel"),
        name="dense_swiglu_ln",
    )(x, w13, w2, g.reshape(1, dm), b.reshape(1, dm))


def _rope_tables(seq_len):
    rows = seq_len // GRID_W
    row = jnp.broadcast_to(jnp.arange(rows, dtype=F32)[:, None], (rows, GRID_W)).reshape(-1)
    col = jnp.broadcast_to(jnp.arange(GRID_W, dtype=F32)[None, :], (rows, GRID_W)).reshape(-1)
    axis_dim = C_HD // 2
    freqs = ROPE_THETA ** (-jnp.arange(0, axis_dim, 2, dtype=F32) / axis_dim)
    ang_r = row[:, None] * freqs[None, :]
    ang_c = col[:, None] * freqs[None, :]
    ang = jnp.concatenate([ang_r, ang_r, ang_c, ang_c], -1)
    cos, sin = jnp.cos(ang), jnp.sin(ang)
    first = (jnp.arange(C_HD) % axis_dim) < axis_dim // 2
    sin_up = jnp.where(first[None, :], -sin, 0.0)
    sin_dn = jnp.where(first[None, :], 0.0, sin)
    return cos, sin_up, sin_dn


def _odd_in_kernel(x_ref, w_ref, qn_ref, kn_ref, cos_ref, su_ref, sd_ref, qt_ref, k_ref, vt_ref):
    xb = x_ref[...].astype(BF16)
    cos, su, sd = cos_ref[...], su_ref[...], sd_ref[...]
    quarter = C_HD // 4
    q_scale = math.log2(math.e) / math.sqrt(C_HD)

    def norm_rope(h, gain):
        h = h * lax.rsqrt(jnp.mean(h * h, -1, keepdims=True) + RMS_EPS) * gain
        up = pltpu.roll(h, C_HD - quarter, 1)
        dn = pltpu.roll(h, quarter, 1)
        return h * cos + up * su + dn * sd

    nq, nk = C_HEADS * C_HD, C_KV_HEADS * C_HD
    pair = 2 * C_HD
    for i in range(0, C_HEADS, 2):
        h2 = _dot(xb, w_ref[:, i * C_HD:i * C_HD + pair])
        for j in range(2):
            h = norm_rope(h2[:, j * C_HD:(j + 1) * C_HD], qn_ref[...]) * q_scale
            qt_ref[(i + j) * C_HD:(i + j + 1) * C_HD, :] = h.T.astype(BF16)
    for i in range(0, C_KV_HEADS, 2):
        h2 = _dot(xb, w_ref[:, nq + i * C_HD:nq + i * C_HD + pair])
        for j in range(2):
            k_ref[:, (i + j) * C_HD:(i + j + 1) * C_HD] = norm_rope(
                h2[:, j * C_HD:(j + 1) * C_HD], kn_ref[...]).astype(BF16)
        v2 = _dot(xb, w_ref[:, nq + nk + i * C_HD:nq + nk + i * C_HD + pair])
        vt_ref[i * C_HD:i * C_HD + pair, :] = v2.T.astype(BF16)


def _odd_in_proj(x, w, qn, kn, tm=512):
    b, s, dm = x.shape
    nq, nk = C_HEADS * C_HD, C_KV_HEADS * C_HD
    cos, su, sd = _rope_tables(s)
    tab = pl.BlockSpec((tm, C_HD), lambda i, j: (j, 0))
    vec = pl.BlockSpec((1, C_HD), lambda i, j: (0, 0))
    return pl.pallas_call(
        _odd_in_kernel,
        grid=(b, s // tm),
        in_specs=[pl.BlockSpec((None, tm, dm), lambda i, j: (i, j, 0)),
                  pl.BlockSpec(w.shape, lambda i, j: (0, 0)),
                  vec, vec, tab, tab, tab],
        out_specs=[pl.BlockSpec((None, nq, tm), lambda i, j: (i, 0, j)),
                   pl.BlockSpec((None, tm, nk), lambda i, j: (i, j, 0)),
                   pl.BlockSpec((None, nk, tm), lambda i, j: (i, 0, j))],
        out_shape=[jax.ShapeDtypeStruct((b, nq, s), BF16),
                   jax.ShapeDtypeStruct((b, s, nk), BF16),
                   jax.ShapeDtypeStruct((b, nk, s), BF16)],
        compiler_params=_params("parallel", "parallel"),
        name="gqa_in_proj_norm_rope",
    )(x, w, qn.reshape(1, C_HD), kn.reshape(1, C_HD), cos, su, sd)


def _gqa_kernel(qt_ref, k_ref, vt_ref, o_ref, acc_sc, st_sc, *, seq, tq, tk):
    qt = jnp.concatenate([qt_ref[h * C_HD:(h + 1) * C_HD, :] for h in range(C_GROUP)], 1)
    cols = C_GROUP * tq
    n = seq // tk
    acc_sc[...] = jnp.zeros(acc_sc.shape, F32)

    def scores(kc):
        k0 = pl.multiple_of(kc * tk, tk)
        return _dot(k_ref[pl.ds(k0, tk), :], qt)

    def consume(kc, st, m_old, l_old):
        k0 = pl.multiple_of(kc * tk, tk)
        m_new = jnp.maximum(m_old, jnp.max(st, 0, keepdims=True))
        alpha = jnp.exp2(m_old - m_new)
        p = jnp.exp2(st - m_new)
        l_new = alpha * l_old + jnp.sum(p, 0, keepdims=True)
        acc_sc[...] = alpha * acc_sc[...] + _dot(vt_ref[:, pl.ds(k0, tk)], p.astype(BF16))
        return m_new, l_new

    st_a, st_b = st_sc.at[0], st_sc.at[1]
    st_a[...] = scores(0)

    def body(i, carry):
        kc = 2 * i
        st_b[...] = scores(kc + 1)
        m, l = consume(kc, st_a[...], *carry)
        st_a[...] = scores(jnp.minimum(kc + 2, n - 1))
        return consume(kc + 1, st_b[...], m, l)

    init = (jnp.full((1, cols), -jnp.inf, F32), jnp.zeros((1, cols), F32))
    _, l = lax.fori_loop(0, n // 2, body, init)
    out = acc_sc[...] / l
    for h in range(C_GROUP):
        o_ref[:, h * C_HD:(h + 1) * C_HD] = out[:, h * tq:(h + 1) * tq].T.astype(o_ref.dtype)


def _gqa(qt, k, vt, tq=512, tk=512):
    b, nq, s = qt.shape
    gw = C_GROUP * C_HD
    assert (s // tk) % 2 == 0
    return pl.pallas_call(
        functools.partial(_gqa_kernel, seq=s, tq=tq, tk=tk),
        grid=(b, C_KV_HEADS, s // tq),
        in_specs=[pl.BlockSpec((None, gw, tq), lambda i, h, j: (i, h, j)),
                  pl.BlockSpec((None, s, C_HD), lambda i, h, j: (i, 0, h)),
                  pl.BlockSpec((None, C_HD, s), lambda i, h, j: (i, h, 0))],
        out_specs=pl.BlockSpec((None, tq, gw), lambda i, h, j: (i, j, h)),
        out_shape=jax.ShapeDtypeStruct((b, s, nq), BF16),
        scratch_shapes=[pltpu.VMEM((C_HD, C_GROUP * tq), F32),
                        pltpu.VMEM((2, tk, C_GROUP * tq), F32)],
        compiler_params=_params("parallel", "parallel", "parallel"),
        name="gqa_flash",
    )(qt, k, vt)


TOP_K = 2
MOE_CHUNK = 512
MOE_TILE = 256
MOE_SPAN = MOE_CHUNK // MOE_TILE + 1
MOE_GROUP = 1024
UNSELECTED = -float(2 ** 26)


def _split_bf16(a):
    hi = a.astype(BF16)
    return hi, (a - hi.astype(F32)).astype(BF16)


def _router_kernel(x_ref, wh_ref, wl_ref, gate_ref, sel_ref):
    xh, xl = _split_bf16(x_ref[...])
    logits = _dot(xh, wh_ref[...]) + (_dot(xh, wl_ref[...]) + _dot(xl, wh_ref[...]))
    lane = lax.broadcasted_iota(jnp.int32, logits.shape, 1)
    logits = jnp.where(lane < N_EXPERTS, logits, -jnp.inf)
    m1 = jnp.max(logits, -1, keepdims=True)
    i1 = jnp.min(jnp.where(logits == m1, lane, LANES), -1, keepdims=True)
    rest = jnp.where(lane == i1, -jnp.inf, logits)
    m2 = jnp.max(rest, -1, keepdims=True)
    i2 = jnp.min(jnp.where(rest == m2, lane, LANES), -1, keepdims=True)
    e = jnp.exp(m2 - m1)
    g1 = 1.0 / (1.0 + e)
    g2 = e / (1.0 + e)
    gate_ref[...] = jnp.where(lane == i1, g1, 0.0) + jnp.where(lane == i2, g2, 0.0)
    sel_ref[...] = jnp.where((lane == i1) | (lane == i2), 1.0, 0.0).astype(BF16)


def _router(x, w):
    n, dm = x.shape
    tm = MOE_CHUNK
    wh, wl = _split_bf16(jnp.zeros((dm, LANES), F32).at[:, :N_EXPERTS].set(w))
    return pl.pallas_call(
        _router_kernel,
        grid=(n // tm,),
        in_specs=[pl.BlockSpec((tm, dm), lambda i: (i, 0)),
                  pl.BlockSpec((dm, LANES), lambda i: (0, 0)),
                  pl.BlockSpec((dm, LANES), lambda i: (0, 0))],
        out_specs=[pl.BlockSpec((tm, LANES), lambda i: (i, 0))] * 2,
        out_shape=[jax.ShapeDtypeStruct((n, LANES), F32), jax.ShapeDtypeStruct((n, LANES), BF16)],
        compiler_params=_params("parallel"),
        name="moe_router_top2",
    )(x, wh, wl)


def _rank_kernel(sel_ref, rank_ref, rank_t_ref, base_ref, cnt_ref, carry):
    @pl.when(pl.program_id(0) == 0)
    def _():
        carry[...] = jnp.zeros(carry.shape, F32)

    sel = sel_ref[...]
    n = sel.shape[0]
    row = lax.broadcasted_iota(jnp.int32, (n, n), 0)
    col = lax.broadcasted_iota(jnp.int32, (n, n), 1)
    earlier = jnp.where(row > col, 1.0, 0.0).astype(BF16)
    before = _dot(earlier, sel) + carry[...]
    chosen = sel.astype(F32)
    rank = jnp.where(chosen > 0.0, before, UNSELECTED)
    rank_ref[...] = rank
    rank_t_ref[...] = rank.T[:SUBLANES, :]
    cnt = jnp.sum(chosen, 0, keepdims=True)
    base_ref[...] = carry[...]
    cnt_ref[...] = cnt
    carry[...] += cnt


def _rank(sel):
    n = sel.shape[0]
    c = n // MOE_CHUNK
    stat = pl.BlockSpec((None, 1, LANES), lambda i: (i, 0, 0))
    return pl.pallas_call(
        _rank_kernel,
        grid=(c,),
        in_specs=[pl.BlockSpec((MOE_CHUNK, LANES), lambda i: (i, 0))],
        out_specs=[pl.BlockSpec((MOE_CHUNK, LANES), lambda i: (i, 0)),
                   pl.BlockSpec((SUBLANES, MOE_CHUNK), lambda i: (0, i)),
                   stat, stat],
        out_shape=[jax.ShapeDtypeStruct((n, LANES), F32),
                   jax.ShapeDtypeStruct((SUBLANES, n), F32),
                   jax.ShapeDtypeStruct((c, 1, LANES), F32),
                   jax.ShapeDtypeStruct((c, 1, LANES), F32)],
        scratch_shapes=[pltpu.VMEM((1, LANES), F32)],
        compiler_params=_params("arbitrary"),
        name="moe_rank",
    )(sel)


def _fill_forward(values, valid):
    idx = lax.cummax(jnp.where(valid, jnp.arange(values.shape[0]), -1))
    idx = jnp.where(idx < 0, jnp.argmax(valid), idx)
    return values[idx]


def _moe_plan(base, cnt, n_group_tiles):
    counts = base[-1] + cnt[-1]
    gsize = -(-counts // MOE_GROUP) * MOE_GROUP
    gend = jnp.cumsum(gsize)
    gstart = gend - gsize
    n_used = gend[-1] // MOE_GROUP
    group = jnp.minimum(jnp.arange(n_group_tiles), n_used - 1)
    group_expert = jnp.minimum(jnp.searchsorted(gend, group * MOE_GROUP, side='right'), N_EXPERTS - 1)
    experts = (group.astype(jnp.int32), group_expert.astype(jnp.int32),
               (jnp.arange(n_group_tiles) < n_used).astype(jnp.int32))
    n_chunks = base.shape[0]
    n_tiles = n_group_tiles * (MOE_GROUP // MOE_TILE)
    clip = lambda t: jnp.clip(t, 0, n_tiles - 1).astype(jnp.int32)
    as_i32 = lambda t: t.astype(jnp.int32)
    first_slot = gstart[None, :] + base
    t_first = first_slot // MOE_TILE
    t_last = (first_slot + cnt - 1) // MOE_TILE
    span = jnp.arange(MOE_SPAN)
    tiles = t_first[..., None] + span
    valid = (cnt[..., None] > 0) & (tiles <= t_last[..., None])
    td = tiles.transpose(1, 0, 2).reshape(-1)
    vd = valid.transpose(1, 0, 2).reshape(-1)
    seen = lax.cummax(jnp.where(vd, td, -1))
    prev = jnp.concatenate([jnp.full((1,), -1, seen.dtype), seen[:-1]])
    first = vd & (td != prev)
    chunk_d = jnp.broadcast_to(jnp.arange(n_chunks)[None, :, None], (N_EXPERTS, n_chunks, MOE_SPAN)).reshape(-1)
    expert_d = jnp.broadcast_to(jnp.arange(N_EXPERTS)[:, None, None], (N_EXPERTS, n_chunks, MOE_SPAN)).reshape(-1)
    n_items = N_EXPERTS * n_chunks + n_tiles
    n_valid = jnp.sum(vd)
    where = jnp.where(vd, jnp.cumsum(vd) - 1, n_items)
    take = jnp.minimum(jnp.arange(n_items), n_valid - 1)

    def compact(values):
        return jnp.zeros((n_items,), jnp.int32).at[where].set(as_i32(values), mode='drop')[take]

    live = as_i32(jnp.arange(n_items) < n_valid)
    dispatch = (clip(compact(td)), compact(chunk_d), compact(expert_d), compact(first) * live, live)
    tiles_ek = tiles.transpose(1, 2, 0).reshape(N_EXPERTS * MOE_SPAN, n_chunks)
    valid_ek = valid.transpose(1, 2, 0).reshape(N_EXPERTS * MOE_SPAN, n_chunks)
    combine = (clip(jax.vmap(_fill_forward)(tiles_ek, valid_ek)).reshape(-1), as_i32(valid_ek).reshape(-1))
    return gstart, experts, dispatch, combine


def _dispatch_kernel(tile_ref, chunk_ref, expert_ref, first_ref, live_ref, slot_ref, x_ref, zero_ref, o_ref):
    del chunk_ref, zero_ref
    w = pl.program_id(0)

    @pl.when(live_ref[w] == 1)
    def _():
        local = slot_ref[pl.ds(expert_ref[w], 1), :] - (tile_ref[w] * MOE_TILE).astype(F32)
        sub = lax.broadcasted_iota(jnp.int32, (MOE_TILE, MOE_CHUNK), 0).astype(F32)
        onehot = jnp.where(sub == local, 1.0, 0.0).astype(BF16)
        rows = _dot(onehot, x_ref[...])

        @pl.when(first_ref[w] == 1)
        def _():
            o_ref[...] = rows.astype(BF16)

        @pl.when(first_ref[w] == 0)
        def _():
            o_ref[...] += rows.astype(BF16)


def _dispatch(plan, slot_t, xb, n_slots):
    n, dm = xb.shape
    grid_spec = pltpu.PrefetchScalarGridSpec(
        num_scalar_prefetch=5,
        grid=(plan[0].shape[0],),
        in_specs=[pl.BlockSpec((SUBLANES, MOE_CHUNK), lambda w, tile, chunk, *_: (0, chunk[w])),
                  pl.BlockSpec((MOE_CHUNK, dm), lambda w, tile, chunk, *_: (chunk[w], 0)),
                  pl.BlockSpec(memory_space=pl.ANY)],
        out_specs=pl.BlockSpec((MOE_TILE, dm), lambda w, tile, *_: (tile[w], 0)),
    )
    return pl.pallas_call(
        _dispatch_kernel,
        grid_spec=grid_spec,
        out_shape=jax.ShapeDtypeStruct((n_slots, dm), BF16),
        input_output_aliases={7: 0},
        compiler_params=_params("arbitrary"),
        name="moe_dispatch",
    )(*plan, slot_t, xb, jnp.zeros((n_slots, dm), BF16))


def _experts_kernel(tile_ref, expert_ref, live_ref, x_ref, w1_ref, w3_ref, w2_ref, o_ref, acc):
    del tile_ref, expert_ref
    j = pl.program_id(1)

    @pl.when(live_ref[pl.program_id(0)] == 1)
    def _():
        xb = x_ref[...]
        a = _dot(xb, w1_ref[...])
        gate = _dot(xb, w3_ref[...])
        y = _dot((a * jax.nn.sigmoid(a) * gate).astype(BF16), w2_ref[...])

        @pl.when(j == 0)
        def _():
            acc[...] = y

        @pl.when(j > 0)
        def _():
            acc[...] += y

        @pl.when(j == pl.num_programs(1) - 1)
        def _():
            o_ref[...] = acc[...].astype(BF16)


def _experts(plan, xs, w13, w2, tf=1792):
    n_slots, dm = xs.shape
    ff = w2.shape[1]
    nf = ff // tf

    def chunk(i, j, live):
        return j * live[i] + (nf - 1) * (1 - live[i])

    grid_spec = pltpu.PrefetchScalarGridSpec(
        num_scalar_prefetch=3,
        grid=(n_slots // MOE_GROUP, nf),
        in_specs=[pl.BlockSpec((MOE_GROUP, dm), lambda i, j, tile, te, live: (tile[i], 0)),
                  pl.BlockSpec((None, dm, tf), lambda i, j, tile, te, live: (te[i], 0, chunk(i, j, live))),
                  pl.BlockSpec((None, dm, tf), lambda i, j, tile, te, live: (te[i], 0, nf + chunk(i, j, live))),
                  pl.BlockSpec((None, tf, dm), lambda i, j, tile, te, live: (te[i], chunk(i, j, live), 0))],
        out_specs=pl.BlockSpec((MOE_GROUP, dm), lambda i, j, tile, te, live: (tile[i], 0)),
        scratch_shapes=[pltpu.VMEM((MOE_GROUP, dm), F32)],
    )
    return pl.pallas_call(
        _experts_kernel,
        grid_spec=grid_spec,
        out_shape=jax.ShapeDtypeStruct((n_slots, dm), BF16),
        compiler_params=_params("arbitrary", "arbitrary"),
        name="moe_experts_swiglu",
    )(*plan, xs, w13, w13, w2)


def _combine_kernel(tile_ref, valid_ref, slot_ref, gate_ref, *rest, n_chunks):
    n_y = N_EXPERTS * MOE_SPAN
    y_refs = rest[:n_y]
    x_ref, g_ref, b_ref, o_ref, acc = rest[n_y:]
    c = pl.program_id(0)
    acc[...] = DN_ALPHA * x_ref[...]
    col = lax.broadcasted_iota(jnp.int32, (MOE_CHUNK, MOE_TILE), 1).astype(F32)
    for e in range(N_EXPERTS):
        slot = slot_ref[:, e:e + 1]
        gate = gate_ref[:, e:e + 1]
        for k in range(MOE_SPAN):
            w = (e * MOE_SPAN + k) * n_chunks + c

            @pl.when(valid_ref[w] == 1)
            def _(w=w, slot=slot, gate=gate, y_ref=y_refs[e * MOE_SPAN + k]):
                local = slot - (tile_ref[w] * MOE_TILE).astype(F32)
                onehot = jnp.where(col == local, 1.0, 0.0).astype(BF16)
                acc[...] += gate * _dot(onehot, y_ref[...])

    o_ref[...] = _layer_norm(acc[...], g_ref[...], b_ref[...])


def _combine(plan, slot, gates, ys, x, g, b):
    n, dm = x.shape
    c = n // MOE_CHUNK
    n_y = N_EXPERTS * MOE_SPAN
    tok = lambda width: pl.BlockSpec((MOE_CHUNK, width), lambda i, *_: (i, 0))
    vec = pl.BlockSpec((1, dm), lambda i, *_: (0, 0))

    def tile_spec(ek):
        return pl.BlockSpec((MOE_TILE, dm), lambda i, tile, valid: (tile[ek * c + i], 0))

    grid_spec = pltpu.PrefetchScalarGridSpec(
        num_scalar_prefetch=2,
        grid=(c,),
        in_specs=[tok(LANES), tok(LANES)] + [tile_spec(ek) for ek in range(n_y)] + [tok(dm), vec, vec],
        out_specs=tok(dm),
        scratch_shapes=[pltpu.VMEM((MOE_CHUNK, dm), F32)],
    )
    return pl.pallas_call(
        functools.partial(_combine_kernel, n_chunks=c),
        grid_spec=grid_spec,
        out_shape=jax.ShapeDtypeStruct((n, dm), F32),
        compiler_params=_params("parallel"),
        name="moe_combine_ln",
    )(*plan, slot, gates, *([ys] * n_y), x, g.reshape(1, dm), b.reshape(1, dm))


def _moe(x, xb, router_w, w13, w2, g, b):
    n, dm = x.shape
    n_slots = TOP_K * n + N_EXPERTS * MOE_GROUP
    gates, sel = _router(x, router_w)
    rank, rank_t, base, cnt = _rank(sel)
    as_counts = lambda t: t[:, 0, :N_EXPERTS].astype(jnp.int32)
    gstart, experts_plan, dispatch_plan, combine_plan = _moe_plan(
        as_counts(base), as_counts(cnt), n_slots // MOE_GROUP)
    gstart = gstart.astype(F32)
    slot = rank + jnp.zeros((LANES,), F32).at[:N_EXPERTS].set(gstart)[None, :]
    slot_t = rank_t + gstart[:, None]
    xs = _dispatch(dispatch_plan, slot_t, xb, n_slots)
    ys = _experts(experts_plan, xs, w13, w2)
    return _combine(combine_plan, slot, gates, ys, x, g, b)


def _trunk(x, mem, w):
    b, s, dm = x.shape
    n = b * s
    mem2 = mem.reshape(-1, dm)
    for i in range(DEPTH):
        j = i // 2
        even = i % 2 == 0
        x2 = x.reshape(n, dm)
        if even:
            u, qkv = _matmul(x2, w['e_w_in'][j], (2 * CONV_CH, QKV_EVEN), (F32, BF16), "even_in_proj")
            a_out = _conv_module(u.reshape(b, s, 2 * CONV_CH), w['e_dw_w'][j], w['e_dw_b'][j],
                                 w['e_cn_g'][j], w['e_cn_b'][j])
            b_out = _mixture_of_dilations(qkv.reshape(b, s, QKV_EVEN))
            w_out = w['e_w_out'][j]
            mixed, w_mix = [a_out, b_out], [w_out[:CONV_CH], w_out[CONV_CH:]]
        else:
            qt, k, vt = _odd_in_proj(x, w['o_w_in'][j], w['o_q_norm'][j], w['o_k_norm'][j])
            mixed, w_mix = [_gqa(qt, k, vt)], [w['o_w_out'][j]]
        (kv,) = _matmul(mem2, w['x_wkv'][i], (2 * dm,), (BF16,), "memory_kv_proj", tm=256)
        outs = _post_mixer(mixed, w_mix, x, (w['ln_g'][i, 0], w['ln_b'][i, 0]),
                           kv.reshape(b, -1, 2 * dm), w['x_wq'][i], w['x_wo'][i],
                           (w['ln_g'][i, 1], w['ln_b'][i, 1]), with_bf16_copy=not even)
        x2 = outs[0].reshape(n, dm)
        if even:
            x2 = _ffn(x2, w['f_w13'][j], w['f_w2'][j], w['ln_g'][i, 2], w['ln_b'][i, 2])
        else:
            x2 = _moe(x2, outs[1].reshape(n, dm), w['m_router'][j], w['m_w13'][j], w['m_w2'][j],
                      w['ln_g'][i, 2], w['ln_b'][i, 2])
        x = x2.reshape(b, s, dm)
    return x


_MXU_WEIGHTS = ('e_w_in', 'e_w_out', 'f_w13', 'f_w2', 'o_w_in', 'o_w_out', 'm_w13', 'm_w2',
                'x_wq', 'x_wkv', 'x_wo')


def kernel(x_prompt, x_sample, mem_prompt, mem_sample, e_w_in, e_dw_w, e_dw_b, e_cn_g, e_cn_b,
           e_w_out, f_w13, f_w2, o_w_in, o_q_norm, o_k_norm, o_w_out, m_router, m_w13, m_w2,
           x_wq, x_wkv, x_wo, ln_g, ln_b):
    w = dict(e_w_in=e_w_in, e_dw_w=e_dw_w, e_dw_b=e_dw_b, e_cn_g=e_cn_g, e_cn_b=e_cn_b,
             e_w_out=e_w_out, f_w13=f_w13, f_w2=f_w2, o_w_in=o_w_in, o_q_norm=o_q_norm,
             o_k_norm=o_k_norm, o_w_out=o_w_out, m_router=m_router, m_w13=m_w13, m_w2=m_w2,
             x_wq=x_wq, x_wkv=x_wkv, x_wo=x_wo, ln_g=ln_g, ln_b=ln_b)
    for name in _MXU_WEIGHTS:
        w[name] = w[name].astype(BF16)
    return (_trunk(x_prompt, mem_prompt, w), _trunk(x_sample, mem_sample, w))
```

```python
import functools
import math

import numpy as np
import jax
import jax.numpy as jnp
from jax import lax
from jax.experimental import pallas as pl
from jax.experimental.pallas import tpu as pltpu

F32 = jnp.float32
BF16 = jnp.bfloat16

D_MODEL = 1024
DEPTH = 2
GRID_W = 64
CONV_CH = D_MODEL // 2
CONV_WIDTH = 31
B_DILATIONS = (1, 4, 16)
B_GROUPS = len(B_DILATIONS)
B_HEADS = 4
B_HD = 128
B_WIDTH = B_HEADS * B_HD
B_HALF = 64
QKV_EVEN = 3 * B_GROUPS * B_WIDTH
C_HEADS = 8
C_KV_HEADS = 2
C_GROUP = C_HEADS // C_KV_HEADS
C_HD = 128
ROPE_THETA = 10000.0
X_HEADS = 4
X_HD = D_MODEL // X_HEADS
FF_DENSE = 2816
N_EXPERTS = 8
FF_EXPERT = 3584
DN_ALPHA = (2 * DEPTH) ** 0.25
LN_EPS = 1e-5
RMS_EPS = 1e-6
NEG = -1e30

LANES = 128
SUBLANES = 8
V7X_VMEM_LIMIT = 56 * 1024 * 1024


def _params(*semantics):
    return pltpu.CompilerParams(dimension_semantics=semantics,
                                vmem_limit_bytes=V7X_VMEM_LIMIT)


def _layer_norm(y, g, b):
    mu = jnp.mean(y, -1, keepdims=True)
    yc = y - mu
    var = jnp.mean(yc * yc, -1, keepdims=True)
    return yc * lax.rsqrt(var + LN_EPS) * g + b


def _dot(a, b):
    return jnp.dot(a, b, preferred_element_type=F32)


def _dot_nt(a, b):
    return lax.dot_general(a, b, (((1,), (1,)), ((), ())), preferred_element_type=F32)


def _matmul_kernel(x_ref, w_ref, *o_refs, splits, chunk):
    xb = x_ref[...].astype(BF16)
    col = 0
    for o_ref, width in zip(o_refs, splits):
        for c in range(0, width, chunk):
            o_ref[:, c:c + chunk] = _dot(xb, w_ref[:, col + c:col + c + chunk]).astype(o_ref.dtype)
        col += width


def _matmul(x, w, splits, dtypes, name, tm=512, chunk=512):
    n, k = x.shape
    return pl.pallas_call(
        functools.partial(_matmul_kernel, splits=splits, chunk=chunk),
        grid=(n // tm,),
        in_specs=[pl.BlockSpec((tm, k), lambda i: (i, 0)),
                  pl.BlockSpec(w.shape, lambda i: (0, 0))],
        out_specs=[pl.BlockSpec((tm, s), lambda i: (i, 0)) for s in splits],
        out_shape=[jax.ShapeDtypeStruct((n, s), dt) for s, dt in zip(splits, dtypes)],
        compiler_params=_params("parallel"),
        name=name,
    )(x, w)


CONV_HALO = 16
CONV_ROWS = 64


def _conv_kernel(prev_ref, cur_ref, next_ref, w_ref, b_ref, g_ref, be_ref, o_ref, hwin, *, ts):
    s = pl.program_id(1)
    last = pl.num_programs(1) - 1

    def glu(u):
        return u[:, :CONV_CH] * jax.nn.sigmoid(u[:, CONV_CH:])

    hwin[0, 0:CONV_HALO, :] = jnp.where(s > 0, glu(prev_ref[...]), 0.0)
    hwin[0, CONV_HALO:CONV_HALO + ts, :] = glu(cur_ref[...])
    hwin[0, CONV_HALO + ts:, :] = jnp.where(s < last, glu(next_ref[...]), 0.0)
    shifted_rows = ts + 2 * CONV_HALO - SUBLANES
    for r in range(1, SUBLANES):
        hwin[r, 0:shifted_rows, :] = hwin[0, r:r + shifted_rows, :]
    pad = CONV_WIDTH // 2
    for c in range(0, ts, CONV_ROWS):
        acc = jnp.broadcast_to(b_ref[...], (CONV_ROWS, CONV_CH))
        for j in range(CONV_WIDTH):
            start = CONV_HALO + c + j - pad
            r = start % SUBLANES
            acc = acc + w_ref[j:j + 1, :] * hwin[r, start - r:start - r + CONV_ROWS, :]
        y = _layer_norm(acc, g_ref[...], be_ref[...])
        o_ref[c:c + CONV_ROWS, :] = (y * jax.nn.sigmoid(y)).astype(o_ref.dtype)


def _conv_module(u, dw_w, dw_b, cn_g, cn_b, ts=512):
    b, s, _ = u.shape
    hb = ts // CONV_HALO
    nhalo = s // CONV_HALO
    row = lambda a: a.reshape(1, CONV_CH)
    return pl.pallas_call(
        functools.partial(_conv_kernel, ts=ts),
        grid=(b, s // ts),
        in_specs=[
            pl.BlockSpec((None, CONV_HALO, 2 * CONV_CH),
                         lambda i, j: (i, jnp.maximum(j * hb - 1, 0), 0)),
            pl.BlockSpec((None, ts, 2 * CONV_CH), lambda i, j: (i, j, 0)),
            pl.BlockSpec((None, CONV_HALO, 2 * CONV_CH),
                         lambda i, j: (i, jnp.minimum((j + 1) * hb, nhalo - 1), 0)),
            pl.BlockSpec((CONV_WIDTH, CONV_CH), lambda i, j: (0, 0)),
            pl.BlockSpec((1, CONV_CH), lambda i, j: (0, 0)),
            pl.BlockSpec((1, CONV_CH), lambda i, j: (0, 0)),
            pl.BlockSpec((1, CONV_CH), lambda i, j: (0, 0)),
        ],
        out_specs=pl.BlockSpec((None, ts, CONV_CH), lambda i, j: (i, j, 0)),
        out_shape=jax.ShapeDtypeStruct((b, s, CONV_CH), BF16),
        scratch_shapes=[pltpu.VMEM((SUBLANES, ts + 2 * CONV_HALO, CONV_CH), F32)],
        compiler_params=_params("parallel", "parallel"),
        name="conv_module",
    )(u, u, u, dw_w, row(dw_b), row(cn_g), row(cn_b))


BAND_Q = 128
BAND_UNROLL = 16
DIL_TILE = 2048


def _band_block(q, k, v, slope, band_bias, key_token0, key_step, seq):
    s = _dot_nt(q, k) * (1.0 / math.sqrt(B_HD)) + slope * band_bias
    col = lax.broadcasted_iota(jnp.int32, (1, 2 * BAND_Q), 1)
    token = key_token0 + col * key_step
    in_seq = token.astype(jnp.uint32) < jnp.uint32(seq)
    s = jnp.where(in_seq, s, NEG)
    m = jnp.max(s, -1, keepdims=True)
    p = jnp.exp(s - m)
    l = jnp.sum(p, -1, keepdims=True)
    return _dot(p.astype(BF16), v) / l, m + jnp.log(l)


def _dilated_kernel(slope_ref, *refs, seq):
    q_refs = refs[0:3]
    kv_refs = refs[3:21]
    o_ref = refs[21]
    kw0, vw0, qf1, kw1, vw1, qf2, kw2, vw2, og0, og1, og2, lg0, lg1, lg2 = refs[22:]
    head = pl.program_id(1)
    t0 = pl.program_id(2) * DIL_TILE

    row = lax.broadcasted_iota(jnp.int32, (BAND_Q, 2 * BAND_Q), 0)
    col = lax.broadcasted_iota(jnp.int32, (BAND_Q, 2 * BAND_Q), 1)
    dist = jnp.abs(col - row - B_HALF)
    band_bias = jnp.where(dist <= B_HALF, -dist.astype(F32), NEG)

    def stage(win, group, which, dtype):
        halo = B_HALF * B_DILATIONS[group]
        prev_ref, cur_ref, next_ref = kv_refs[group * 6 + which * 3:group * 6 + which * 3 + 3]
        win[0:halo, :] = prev_ref[...].astype(dtype)
        win[halo:halo + DIL_TILE, :] = cur_ref[...].astype(dtype)
        win[halo + DIL_TILE:, :] = next_ref[...].astype(dtype)

    def slope_of(group):
        return slope_ref[group * B_HEADS + head] * float(B_DILATIONS[group])

    stage(kw0, 0, 0, BF16)
    stage(vw0, 0, 1, BF16)
    slope0 = slope_of(0)

    def body0(i, carry):
        i0 = pl.multiple_of(i * BAND_Q, BAND_Q)
        o, lse = _band_block(q_refs[0][pl.ds(i0, BAND_Q), :], kw0[pl.ds(i0, 2 * BAND_Q), :],
                             vw0[pl.ds(i0, 2 * BAND_Q), :], slope0, band_bias,
                             t0 + i0 - B_HALF, 1, seq)
        og0[pl.ds(i0, BAND_Q), :] = o
        lg0[pl.ds(i0, BAND_Q), :] = jnp.broadcast_to(lse, (BAND_Q, B_HD))
        return carry

    lax.fori_loop(0, DIL_TILE // BAND_Q, body0, 0, unroll=BAND_UNROLL)

    for group, qf, kw, vw, og, lg in ((1, qf1, kw1, vw1, og1, lg1), (2, qf2, kw2, vw2, og2, lg2)):
        d = B_DILATIONS[group]
        nq = DIL_TILE // d
        qf[...] = q_refs[group][...].astype(F32)
        stage(kw, group, 0, F32)
        stage(vw, group, 1, F32)
        slope = slope_of(group)

        def body(r, carry, d=d, nq=nq, qf=qf, kw=kw, vw=vw, og=og, lg=lg, slope=slope):
            qr = qf[pl.ds(r, nq, stride=d), :].astype(BF16)
            kr = kw[pl.ds(r, nq + 2 * B_HALF, stride=d), :].astype(BF16)
            vr = vw[pl.ds(r, nq + 2 * B_HALF, stride=d), :].astype(BF16)
            for i0 in range(0, nq, BAND_Q):
                o, lse = _band_block(qr[i0:i0 + BAND_Q], kr[i0:i0 + 2 * BAND_Q],
                                     vr[i0:i0 + 2 * BAND_Q], slope, band_bias,
                                     t0 + r + (i0 - B_HALF) * d, d, seq)
                og[pl.ds(r + i0 * d, BAND_Q, stride=d), :] = o
                lg[pl.ds(r + i0 * d, BAND_Q, stride=d), :] = jnp.broadcast_to(lse, (BAND_Q, B_HD))
            return carry

        lax.fori_loop(0, d, body, 0, unroll=max(1, BAND_UNROLL * BAND_Q * d // DIL_TILE))

    l0, l1, l2 = lg0[...], lg1[...], lg2[...]
    top = jnp.maximum(jnp.maximum(l0, l1), l2)
    w0, w1, w2 = jnp.exp(l0 - top), jnp.exp(l1 - top), jnp.exp(l2 - top)
    mix = (w0 * og0[...] + w1 * og1[...] + w2 * og2[...]) / (w0 + w1 + w2)
    o_ref[...] = mix.astype(o_ref.dtype)


def _mixture_of_dilations(qkv):
    b, s, _ = qkv.shape
    n = B_GROUPS * B_HEADS
    slopes = jnp.asarray(np.array([2.0 ** (-8.0 * (i + 1) / n) for i in range(n)], np.float32))
    n_tiles = s // DIL_TILE

    def column(which, group):
        return (which * B_GROUPS + group) * B_HEADS

    def tile_spec(col):
        return pl.BlockSpec((None, DIL_TILE, B_HD), lambda i, h, t: (i, t, col + h))

    def halo_specs(col, group):
        halo = B_HALF * B_DILATIONS[group]
        per_tile = DIL_TILE // halo
        last = s // halo - 1
        return [pl.BlockSpec((None, halo, B_HD),
                             lambda i, h, t: (i, jnp.maximum(t * per_tile - 1, 0), col + h)),
                tile_spec(col),
                pl.BlockSpec((None, halo, B_HD),
                             lambda i, h, t: (i, jnp.minimum((t + 1) * per_tile, last), col + h))]

    in_specs = [pl.BlockSpec(memory_space=pltpu.SMEM)]
    in_specs += [tile_spec(column(0, g)) for g in range(B_GROUPS)]
    for g in range(B_GROUPS):
        in_specs += halo_specs(column(1, g), g) + halo_specs(column(2, g), g)
    scratch = []
    for g, d in enumerate(B_DILATIONS):
        win = (DIL_TILE + 2 * B_HALF * d, B_HD)
        if g == 0:
            scratch += [pltpu.VMEM(win, BF16)] * 2
        else:
            scratch += [pltpu.VMEM((DIL_TILE, B_HD), F32), pltpu.VMEM(win, F32), pltpu.VMEM(win, F32)]
    scratch += [pltpu.VMEM((DIL_TILE, B_HD), F32)] * (2 * B_GROUPS)
    return pl.pallas_call(
        functools.partial(_dilated_kernel, seq=s),
        grid=(b, B_HEADS, n_tiles),
        in_specs=in_specs,
        out_specs=pl.BlockSpec((None, DIL_TILE, B_HD), lambda i, h, t: (i, t, h)),
        out_shape=jax.ShapeDtypeStruct((b, s, B_WIDTH), BF16),
        scratch_shapes=scratch,
        compiler_params=_params("parallel", "parallel", "parallel"),
        name="dilated_mixture_attn",
    )(slopes, *([qkv] * (1 + 3 + 6 * B_GROUPS - 1)))


def _post_mixer_kernel(*refs, n_in):
    ins = refs[:n_in]
    ws = refs[n_in:2 * n_in]
    x_ref, g0_ref, b0_ref, k_ref, v_ref, wq_ref, wo_ref, g_ref, b_ref, o_ref = refs[2 * n_in:2 * n_in + 10]
    maybe_ob_ref = refs[2 * n_in + 10:]
    y = DN_ALPHA * x_ref[...]
    for a_ref, w_ref in zip(ins, ws):
        y = y + _dot(a_ref[...], w_ref[...])
    x = _layer_norm(y, g0_ref[...], b0_ref[...])
    q = (_dot(x.astype(BF16), wq_ref[...]) * (1.0 / math.sqrt(X_HD))).astype(BF16)
    heads = []
    for h in range(X_HEADS):
        sl = slice(h * X_HD, (h + 1) * X_HD)
        s = _dot_nt(q[:, sl], k_ref[:, sl])
        m = jnp.max(s, -1, keepdims=True)
        p = jnp.exp(s - m)
        l = jnp.sum(p, -1, keepdims=True)
        heads.append((_dot(p.astype(BF16), v_ref[:, sl]) / l).astype(BF16))
    o = jnp.concatenate(heads, -1)
    y = _layer_norm(DN_ALPHA * x + _dot(o, wo_ref[...]), g_ref[...], b_ref[...])
    o_ref[...] = y
    for ob_ref in maybe_ob_ref:
        ob_ref[...] = y.astype(BF16)


def _post_mixer(ins, ws, x, ln0, kv, wq, wo, ln1, with_bf16_copy, tm=1024):
    bsz, s, dm = x.shape
    m = kv.shape[1]
    rows = lambda width: pl.BlockSpec((None, tm, width), lambda i, j: (i, j, 0))
    whole = lambda a: pl.BlockSpec(a.shape, lambda i, j: (0, 0))
    vec = pl.BlockSpec((1, dm), lambda i, j: (0, 0))
    out_specs = [rows(dm)]
    out_shape = [jax.ShapeDtypeStruct((bsz, s, dm), F32)]
    if with_bf16_copy:
        out_specs.append(rows(dm))
        out_shape.append(jax.ShapeDtypeStruct((bsz, s, dm), BF16))
    as_row = lambda t: t.reshape(1, dm)
    return pl.pallas_call(
        functools.partial(_post_mixer_kernel, n_in=len(ins)),
        grid=(bsz, s // tm),
        in_specs=([rows(a.shape[-1]) for a in ins] + [whole(w) for w in ws]
                  + [rows(dm), vec, vec,
                     pl.BlockSpec((None, m, dm), lambda i, j: (i, 0, 0)),
                     pl.BlockSpec((None, m, dm), lambda i, j: (i, 0, 1)),
                     whole(wq), whole(wo), vec, vec]),
        out_specs=out_specs,
        out_shape=out_shape,
        compiler_params=_params("parallel", "parallel"),
        name="mixer_out_xattn_ln",
    )(*ins, *ws, x, as_row(ln0[0]), as_row(ln0[1]), kv, kv, wq, wo, as_row(ln1[0]), as_row(ln1[1]))


def _ffn_kernel(x_ref, w13_ref, w2_ref, g_ref, b_ref, o_ref):
    x = x_ref[...]
    xb = x.astype(BF16)
    ff = w2_ref.shape[0]
    a = _dot(xb, w13_ref[:, :ff])
    gate = _dot(xb, w13_ref[:, ff:])
    h = (a * jax.nn.sigmoid(a) * gate).astype(BF16)
    o_ref[...] = _layer_norm(DN_ALPHA * x + _dot(h, w2_ref[...]), g_ref[...], b_ref[...])


def _ffn(x, w13, w2, g, b, tm=512):
    n, dm = x.shape
    resident = lambda w: pl.BlockSpec(w.shape, lambda i: (0, 0), pipeline_mode=pl.Buffered(1))
    return pl.pallas_call(
        _ffn_kernel,
        grid=(n // tm,),
        in_specs=[pl.BlockSpec((tm, dm), lambda i: (i, 0)),
                  resident(w13), resident(w2),
                  pl.BlockSpec((1, dm), lambda i: (0, 0)),
                  pl.BlockSpec((1, dm), lambda i: (0, 0))],
        out_specs=pl.BlockSpec((tm, dm), lambda i: (i, 0)),
        out_shape=jax.ShapeDtypeStruct((n, dm), F32),
        compiler_params=_params("parallel"),
        name="dense_swiglu_ln",
    )(x, w13, w2, g.reshape(1, dm), b.reshape(1, dm))


def _rope_tables(seq_len):
    rows = seq_len // GRID_W
    row = jnp.broadcast_to(jnp.arange(rows, dtype=F32)[:, None], (rows, GRID_W)).reshape(-1)
    col = jnp.broadcast_to(jnp.arange(GRID_W, dtype=F32)[None, :], (rows, GRID_W)).reshape(-1)
    axis_dim = C_HD // 2
    freqs = ROPE_THETA ** (-jnp.arange(0, axis_dim, 2, dtype=F32) / axis_dim)
    ang_r = row[:, None] * freqs[None, :]
    ang_c = col[:, None] * freqs[None, :]
    ang = jnp.concatenate([ang_r, ang_r, ang_c, ang_c], -1)
    cos, sin = jnp.cos(ang), jnp.sin(ang)
    first = (jnp.arange(C_HD) % axis_dim) < axis_dim // 2
    sin_up = jnp.where(first[None, :], -sin, 0.0)
    sin_dn = jnp.where(first[None, :], 0.0, sin)
    return cos, sin_up, sin_dn


def _odd_in_kernel(x_ref, w_ref, kn_ref, cos_ref, su_ref, sd_ref, cq_ref, uq_ref, dq_ref,
                   qt_ref, k_ref, vt_ref):
    xb = x_ref[...].astype(BF16)
    cos, su, sd = cos_ref[...], su_ref[...], sd_ref[...]
    cq, uq, dq = cq_ref[...], uq_ref[...], dq_ref[...]
    quarter = C_HD // 4

    def norm_rope(h, gain):
        h = h * lax.rsqrt(jnp.mean(h * h, -1, keepdims=True) + RMS_EPS) * gain
        up = pltpu.roll(h, C_HD - quarter, 1)
        dn = pltpu.roll(h, quarter, 1)
        return h * cos + up * su + dn * sd

    def norm_rope_t(ht):
        inv = lax.rsqrt(jnp.mean(ht * ht, 0, keepdims=True) + RMS_EPS)
        up = jnp.concatenate([ht[quarter:], ht[:quarter]], 0)
        dn = jnp.concatenate([ht[C_HD - quarter:], ht[:C_HD - quarter]], 0)
        return (ht * cq + up * uq + dn * dq) * inv

    nq, nk = C_HEADS * C_HD, C_KV_HEADS * C_HD
    pair = 2 * C_HD
    for i in range(0, C_HEADS, 2):
        h2 = _dot(xb, w_ref[:, i * C_HD:i * C_HD + pair])
        for j in range(2):
            ht = h2[:, j * C_HD:(j + 1) * C_HD].T
            qt_ref[(i + j) * C_HD:(i + j + 1) * C_HD, :] = norm_rope_t(ht).astype(BF16)
    for i in range(0, C_KV_HEADS, 2):
        h2 = _dot(xb, w_ref[:, nq + i * C_HD:nq + i * C_HD + pair])
        for j in range(2):
            k_ref[:, (i + j) * C_HD:(i + j + 1) * C_HD] = norm_rope(
                h2[:, j * C_HD:(j + 1) * C_HD], kn_ref[...]).astype(BF16)
        v2 = _dot(xb, w_ref[:, nq + nk + i * C_HD:nq + nk + i * C_HD + pair])
        vt_ref[i * C_HD:i * C_HD + pair, :] = v2.T.astype(BF16)


def _odd_in_proj(x, w, qn, kn, tm=512):
    b, s, dm = x.shape
    nq, nk = C_HEADS * C_HD, C_KV_HEADS * C_HD
    cos, su, sd = _rope_tables(s)
    q_scale = math.log2(math.e) / math.sqrt(C_HD)
    quarter = C_HD // 4
    cq = (cos * qn[None, :] * q_scale).T
    uq = (su * jnp.roll(qn, -quarter)[None, :] * q_scale).T
    dq = (sd * jnp.roll(qn, quarter)[None, :] * q_scale).T
    tab = pl.BlockSpec((tm, C_HD), lambda i, j: (j, 0))
    tab_t = pl.BlockSpec((C_HD, tm), lambda i, j: (0, j))
    vec = pl.BlockSpec((1, C_HD), lambda i, j: (0, 0))
    return pl.pallas_call(
        _odd_in_kernel,
        grid=(b, s // tm),
        in_specs=[pl.BlockSpec((None, tm, dm), lambda i, j: (i, j, 0)),
                  pl.BlockSpec(w.shape, lambda i, j: (0, 0)),
                  vec, tab, tab, tab, tab_t, tab_t, tab_t],
        out_specs=[pl.BlockSpec((None, nq, tm), lambda i, j: (i, 0, j)),
                   pl.BlockSpec((None, tm, nk), lambda i, j: (i, j, 0)),
                   pl.BlockSpec((None, nk, tm), lambda i, j: (i, 0, j))],
        out_shape=[jax.ShapeDtypeStruct((b, nq, s), BF16),
                   jax.ShapeDtypeStruct((b, s, nk), BF16),
                   jax.ShapeDtypeStruct((b, nk, s), BF16)],
        compiler_params=_params("parallel", "parallel"),
        name="gqa_in_proj_norm_rope",
    )(x, w, kn.reshape(1, C_HD), cos, su, sd, cq, uq, dq)


def _gqa_kernel(qt_ref, k_ref, vt_ref, o_ref, acc_sc, st_sc, *, seq, tq, tk):
    qt = jnp.concatenate([qt_ref[h * C_HD:(h + 1) * C_HD, :] for h in range(C_GROUP)], 1)
    cols = C_GROUP * tq
    n = seq // tk
    acc_sc[...] = jnp.zeros(acc_sc.shape, F32)

    def scores(kc):
        k0 = pl.multiple_of(kc * tk, tk)
        return _dot(k_ref[pl.ds(k0, tk), :], qt)

    def consume(kc, st, m_old, l_old):
        k0 = pl.multiple_of(kc * tk, tk)
        m_new = jnp.maximum(m_old, jnp.max(st, 0, keepdims=True))
        alpha = jnp.exp2(m_old - m_new)
        p = jnp.exp2(st - m_new)
        l_new = alpha * l_old + jnp.sum(p, 0, keepdims=True)
        acc_sc[...] = alpha * acc_sc[...] + _dot(vt_ref[:, pl.ds(k0, tk)], p.astype(BF16))
        return m_new, l_new

    st_a, st_b = st_sc.at[0], st_sc.at[1]
    st_a[...] = scores(0)

    def body(i, carry):
        kc = 2 * i
        st_b[...] = scores(kc + 1)
        m, l = consume(kc, st_a[...], *carry)
        st_a[...] = scores(jnp.minimum(kc + 2, n - 1))
        return consume(kc + 1, st_b[...], m, l)

    init = (jnp.full((1, cols), -jnp.inf, F32), jnp.zeros((1, cols), F32))
    _, l = lax.fori_loop(0, n // 2, body, init)
    out = acc_sc[...] / l
    for h in range(C_GROUP):
        o_ref[:, h * C_HD:(h + 1) * C_HD] = out[:, h * tq:(h + 1) * tq].T.astype(o_ref.dtype)


def _gqa(qt, k, vt, tq=512, tk=512):
    b, nq, s = qt.shape
    gw = C_GROUP * C_HD
    assert (s // tk) % 2 == 0
    return pl.pallas_call(
        functools.partial(_gqa_kernel, seq=s, tq=tq, tk=tk),
        grid=(b, C_KV_HEADS, s // tq),
        in_specs=[pl.BlockSpec((None, gw, tq), lambda i, h, j: (i, h, j)),
                  pl.BlockSpec((None, s, C_HD), lambda i, h, j: (i, 0, h)),
                  pl.BlockSpec((None, C_HD, s), lambda i, h, j: (i, h, 0))],
        out_specs=pl.BlockSpec((None, tq, gw), lambda i, h, j: (i, j, h)),
        out_shape=jax.ShapeDtypeStruct((b, s, nq), BF16),
        scratch_shapes=[pltpu.VMEM((C_HD, C_GROUP * tq), F32),
                        pltpu.VMEM((2, tk, C_GROUP * tq), F32)],
        compiler_params=_params("parallel", "parallel", "parallel"),
        name="gqa_flash",
    )(qt, k, vt)


TOP_K = 2
MOE_CHUNK = 512
MOE_TILE = 256
MOE_SPAN = MOE_CHUNK // MOE_TILE + 1
MOE_PACK = 3
MOE_GROUP = 1024
UNSELECTED = -float(2 ** 26)


def _split_bf16(a):
    hi = a.astype(BF16)
    return hi, (a - hi.astype(F32)).astype(BF16)


def _router_kernel(x_ref, wh_ref, wl_ref, gate_ref, sel_ref):
    xh, xl = _split_bf16(x_ref[...])
    logits = _dot(xh, wh_ref[...]) + (_dot(xh, wl_ref[...]) + _dot(xl, wh_ref[...]))
    lane = lax.broadcasted_iota(jnp.int32, logits.shape, 1)
    logits = jnp.where(lane < N_EXPERTS, logits, -jnp.inf)
    m1 = jnp.max(logits, -1, keepdims=True)
    i1 = jnp.min(jnp.where(logits == m1, lane, LANES), -1, keepdims=True)
    rest = jnp.where(lane == i1, -jnp.inf, logits)
    m2 = jnp.max(rest, -1, keepdims=True)
    i2 = jnp.min(jnp.where(rest == m2, lane, LANES), -1, keepdims=True)
    e = jnp.exp(m2 - m1)
    g1 = 1.0 / (1.0 + e)
    g2 = e / (1.0 + e)
    gate_ref[...] = jnp.where(lane == i1, g1, 0.0) + jnp.where(lane == i2, g2, 0.0)
    sel_ref[...] = jnp.where((lane == i1) | (lane == i2), 1.0, 0.0).astype(BF16)


def _router(x, w):
    n, dm = x.shape
    tm = MOE_CHUNK
    wh, wl = _split_bf16(jnp.zeros((dm, LANES), F32).at[:, :N_EXPERTS].set(w))
    return pl.pallas_call(
        _router_kernel,
        grid=(n // tm,),
        in_specs=[pl.BlockSpec((tm, dm), lambda i: (i, 0)),
                  pl.BlockSpec((dm, LANES), lambda i: (0, 0)),
                  pl.BlockSpec((dm, LANES), lambda i: (0, 0))],
        out_specs=[pl.BlockSpec((tm, LANES), lambda i: (i, 0))] * 2,
        out_shape=[jax.ShapeDtypeStruct((n, LANES), F32), jax.ShapeDtypeStruct((n, LANES), BF16)],
        compiler_params=_params("parallel"),
        name="moe_router_top2",
    )(x, wh, wl)


def _rank_kernel(sel_ref, rank_ref, rank_t_ref, base_ref, cnt_ref, carry):
    @pl.when(pl.program_id(0) == 0)
    def _():
        carry[...] = jnp.zeros(carry.shape, F32)

    sel = sel_ref[...]
    n = sel.shape[0]
    row = lax.broadcasted_iota(jnp.int32, (n, n), 0)
    col = lax.broadcasted_iota(jnp.int32, (n, n), 1)
    earlier = jnp.where(row > col, 1.0, 0.0).astype(BF16)
    before = _dot(earlier, sel) + carry[...]
    chosen = sel.astype(F32)
    rank = jnp.where(chosen > 0.0, before, UNSELECTED)
    rank_ref[...] = rank
    rank_t_ref[...] = rank.T[:SUBLANES, :]
    cnt = jnp.sum(chosen, 0, keepdims=True)
    base_ref[...] = carry[...]
    cnt_ref[...] = cnt
    carry[...] += cnt


def _rank(sel):
    n = sel.shape[0]
    c = n // MOE_CHUNK
    stat = pl.BlockSpec((None, 1, LANES), lambda i: (i, 0, 0))
    return pl.pallas_call(
        _rank_kernel,
        grid=(c,),
        in_specs=[pl.BlockSpec((MOE_CHUNK, LANES), lambda i: (i, 0))],
        out_specs=[pl.BlockSpec((MOE_CHUNK, LANES), lambda i: (i, 0)),
                   pl.BlockSpec((SUBLANES, MOE_CHUNK), lambda i: (0, i)),
                   stat, stat],
        out_shape=[jax.ShapeDtypeStruct((n, LANES), F32),
                   jax.ShapeDtypeStruct((SUBLANES, n), F32),
                   jax.ShapeDtypeStruct((c, 1, LANES), F32),
                   jax.ShapeDtypeStruct((c, 1, LANES), F32)],
        scratch_shapes=[pltpu.VMEM((1, LANES), F32)],
        compiler_params=_params("arbitrary"),
        name="moe_rank",
    )(sel)


def _fill_forward(values, valid):
    idx = lax.cummax(jnp.where(valid, jnp.arange(values.shape[0]), -1))
    idx = jnp.where(idx < 0, jnp.argmax(valid), idx)
    return values[idx]


def _moe_plan(base, cnt, n_group_tiles):
    counts = base[-1] + cnt[-1]
    gsize = -(-counts // MOE_GROUP) * MOE_GROUP
    gend = jnp.cumsum(gsize)
    gstart = gend - gsize
    n_used = gend[-1] // MOE_GROUP
    group = jnp.minimum(jnp.arange(n_group_tiles), n_used - 1)
    group_expert = jnp.minimum(jnp.searchsorted(gend, group * MOE_GROUP, side='right'), N_EXPERTS - 1)
    experts = (group.astype(jnp.int32), group_expert.astype(jnp.int32),
               (jnp.arange(n_group_tiles) < n_used).astype(jnp.int32))
    n_chunks = base.shape[0]
    n_tiles = n_group_tiles * (MOE_GROUP // MOE_TILE)
    clip = lambda t: jnp.clip(t, 0, n_tiles - 1).astype(jnp.int32)
    as_i32 = lambda t: t.astype(jnp.int32)
    first_slot = gstart[None, :] + base
    t_first = first_slot // MOE_TILE
    t_last = (first_slot + cnt - 1) // MOE_TILE
    span = jnp.arange(MOE_SPAN)
    tiles = t_first[..., None] + span
    valid = (cnt[..., None] > 0) & (tiles <= t_last[..., None])
    td = tiles.transpose(1, 0, 2).reshape(-1)
    vd = valid.transpose(1, 0, 2).reshape(-1)
    seen = lax.cummax(jnp.where(vd, td, -1))
    prev = jnp.concatenate([jnp.full((1,), -1, seen.dtype), seen[:-1]])
    first = vd & (td != prev)
    chunk_d = jnp.broadcast_to(jnp.arange(n_chunks)[None, :, None], (N_EXPERTS, n_chunks, MOE_SPAN)).reshape(-1)
    expert_d = jnp.broadcast_to(jnp.arange(N_EXPERTS)[:, None, None], (N_EXPERTS, n_chunks, MOE_SPAN)).reshape(-1)
    n_items = N_EXPERTS * n_chunks + n_tiles
    n_valid = jnp.sum(vd)
    where = jnp.where(vd, jnp.cumsum(vd) - 1, n_items)
    take = jnp.minimum(jnp.arange(n_items), n_valid - 1)

    def compact(values):
        return jnp.zeros((n_items,), jnp.int32).at[where].set(as_i32(values), mode='drop')[take]

    live = jnp.arange(n_items) < n_valid
    item_tile, item_chunk, item_expert = clip(compact(td)), compact(chunk_d), compact(expert_d)
    item_first = (compact(first) == 1) & live
    idx = jnp.arange(n_items)
    rank_in_run = idx - lax.cummax(jnp.where(item_first, idx, 0))
    pos = rank_in_run % MOE_PACK
    opens = (pos == 0) & live
    pack = jnp.cumsum(opens) - 1
    n_packs = N_EXPERTS * n_chunks // MOE_PACK + n_tiles + 1
    n_open = jnp.sum(opens)
    keep = jnp.minimum(jnp.arange(n_packs), n_open - 1)
    pack_live = jnp.arange(n_packs) < n_open

    def per_pack(values, cond):
        where_p = jnp.where(cond & live, pack, n_packs)
        return jnp.zeros((n_packs,), jnp.int32).at[where_p].set(as_i32(values), mode='drop')

    chunks, valids = [], []
    for k in range(MOE_PACK):
        vk = (per_pack(jnp.ones_like(idx), pos == k) == 1) & pack_live
        chunks.append(_fill_forward(per_pack(item_chunk, pos == k), vk))
        valids.append(as_i32(vk))
    dispatch = (per_pack(item_tile, pos == 0)[keep], per_pack(item_expert, pos == 0)[keep],
                per_pack(rank_in_run == 0, pos == 0) * as_i32(pack_live), as_i32(pack_live),
                jnp.concatenate(chunks), jnp.concatenate(valids))
    tiles_ek = tiles.transpose(1, 2, 0).reshape(N_EXPERTS * MOE_SPAN, n_chunks)
    valid_ek = valid.transpose(1, 2, 0).reshape(N_EXPERTS * MOE_SPAN, n_chunks)
    combine = (clip(jax.vmap(_fill_forward)(tiles_ek, valid_ek)).reshape(-1), as_i32(valid_ek).reshape(-1))
    return gstart, experts, dispatch, combine


def _dispatch_kernel(tile_ref, expert_ref, first_ref, live_ref, chunk_ref, valid_ref, *refs):
    del chunk_ref
    slot_refs, x_refs = refs[:MOE_PACK], refs[MOE_PACK:2 * MOE_PACK]
    o_ref = refs[2 * MOE_PACK + 1]
    w = pl.program_id(0)
    n_packs = pl.num_programs(0)

    @pl.when(live_ref[w] == 1)
    def _():
        base = (tile_ref[w] * MOE_TILE).astype(F32)
        sub = lax.broadcasted_iota(jnp.int32, (MOE_TILE, MOE_CHUNK), 0).astype(F32)

        def rows(k):
            local = slot_refs[k][pl.ds(expert_ref[w], 1), :] - base
            onehot = jnp.where(sub == local, 1.0, 0.0).astype(BF16)
            return _dot(onehot, x_refs[k][...]).astype(BF16)

        @pl.when(first_ref[w] == 1)
        def _():
            o_ref[...] = rows(0)

        @pl.when(first_ref[w] == 0)
        def _():
            o_ref[...] += rows(0)

        for k in range(1, MOE_PACK):
            @pl.when(valid_ref[k * n_packs + w] == 1)
            def _(k=k):
                o_ref[...] += rows(k)


def _dispatch(plan, slot_t, xb, n_slots):
    n, dm = xb.shape
    n_packs = plan[0].shape[0]

    def chunk_of(k):
        return lambda w, tile, expert, first, live, chunk, valid: chunk[k * n_packs + w]

    slot_specs = [pl.BlockSpec((SUBLANES, MOE_CHUNK), lambda w, *p, f=chunk_of(k): (0, f(w, *p)))
                  for k in range(MOE_PACK)]
    x_specs = [pl.BlockSpec((MOE_CHUNK, dm), lambda w, *p, f=chunk_of(k): (f(w, *p), 0))
               for k in range(MOE_PACK)]
    grid_spec = pltpu.PrefetchScalarGridSpec(
        num_scalar_prefetch=6,
        grid=(n_packs,),
        in_specs=slot_specs + x_specs + [pl.BlockSpec(memory_space=pl.ANY)],
        out_specs=pl.BlockSpec((MOE_TILE, dm), lambda w, tile, *_: (tile[w], 0)),
    )
    return pl.pallas_call(
        _dispatch_kernel,
        grid_spec=grid_spec,
        out_shape=jax.ShapeDtypeStruct((n_slots, dm), BF16),
        input_output_aliases={6 + 2 * MOE_PACK: 0},
        compiler_params=_params("arbitrary"),
        name="moe_dispatch",
    )(*plan, *([slot_t] * MOE_PACK), *([xb] * MOE_PACK), jnp.zeros((n_slots, dm), BF16))


def _experts_kernel(tile_ref, expert_ref, live_ref, x_ref, w1_ref, w3_ref, w2_ref, o_ref, acc):
    del tile_ref, expert_ref
    j = pl.program_id(1)

    @pl.when(live_ref[pl.program_id(0)] == 1)
    def _():
        xb = x_ref[...]
        a = _dot(xb, w1_ref[...])
        gate = _dot(xb, w3_ref[...])
        y = _dot((a * jax.nn.sigmoid(a) * gate).astype(BF16), w2_ref[...])

        @pl.when(j == 0)
        def _():
            acc[...] = y

        @pl.when(j > 0)
        def _():
            acc[...] += y

        @pl.when(j == pl.num_programs(1) - 1)
        def _():
            o_ref[...] = acc[...].astype(BF16)


def _experts(plan, xs, w13, w2, tf=1792):
    n_slots, dm = xs.shape
    ff = w2.shape[1]
    nf = ff // tf

    def chunk(i, j, live):
        return j * live[i] + (nf - 1) * (1 - live[i])

    grid_spec = pltpu.PrefetchScalarGridSpec(
        num_scalar_prefetch=3,
        grid=(n_slots // MOE_GROUP, nf),
        in_specs=[pl.BlockSpec((MOE_GROUP, dm), lambda i, j, tile, te, live: (tile[i], 0)),
                  pl.BlockSpec((None, dm, tf), lambda i, j, tile, te, live: (te[i], 0, chunk(i, j, live))),
                  pl.BlockSpec((None, dm, tf), lambda i, j, tile, te, live: (te[i], 0, nf + chunk(i, j, live))),
                  pl.BlockSpec((None, tf, dm), lambda i, j, tile, te, live: (te[i], chunk(i, j, live), 0))],
        out_specs=pl.BlockSpec((MOE_GROUP, dm), lambda i, j, tile, te, live: (tile[i], 0)),
        scratch_shapes=[pltpu.VMEM((MOE_GROUP, dm), F32)],
    )
    return pl.pallas_call(
        _experts_kernel,
        grid_spec=grid_spec,
        out_shape=jax.ShapeDtypeStruct((n_slots, dm), BF16),
        compiler_params=_params("arbitrary", "arbitrary"),
        name="moe_experts_swiglu",
    )(*plan, xs, w13, w13, w2)


def _combine_kernel(tile_ref, valid_ref, slot_ref, gate_ref, *rest, n_chunks):
    n_y = N_EXPERTS * MOE_SPAN
    y_refs = rest[:n_y]
    x_ref, g_ref, b_ref, o_ref, acc = rest[n_y:]
    c = pl.program_id(0)
    acc[...] = DN_ALPHA * x_ref[...]
    col = lax.broadcasted_iota(jnp.int32, (MOE_CHUNK, MOE_TILE), 1).astype(F32)
    for e in range(N_EXPERTS):
        slot = slot_ref[:, e:e + 1]
        gate = gate_ref[:, e:e + 1]
        for k in range(MOE_SPAN):
            w = (e * MOE_SPAN + k) * n_chunks + c

            @pl.when(valid_ref[w] == 1)
            def _(w=w, slot=slot, gate=gate, y_ref=y_refs[e * MOE_SPAN + k]):
                local = slot - (tile_ref[w] * MOE_TILE).astype(F32)
                onehot = jnp.where(col == local, 1.0, 0.0).astype(BF16)
                acc[...] += gate * _dot(onehot, y_ref[...])

    o_ref[...] = _layer_norm(acc[...], g_ref[...], b_ref[...])


def _combine(plan, slot, gates, ys, x, g, b):
    n, dm = x.shape
    c = n // MOE_CHUNK
    n_y = N_EXPERTS * MOE_SPAN
    tok = lambda width: pl.BlockSpec((MOE_CHUNK, width), lambda i, *_: (i, 0))
    vec = pl.BlockSpec((1, dm), lambda i, *_: (0, 0))

    def tile_spec(ek):
        return pl.BlockSpec((MOE_TILE, dm), lambda i, tile, valid: (tile[ek * c + i], 0))

    grid_spec = pltpu.PrefetchScalarGridSpec(
        num_scalar_prefetch=2,
        grid=(c,),
        in_specs=[tok(LANES), tok(LANES)] + [tile_spec(ek) for ek in range(n_y)] + [tok(dm), vec, vec],
        out_specs=tok(dm),
        scratch_shapes=[pltpu.VMEM((MOE_CHUNK, dm), F32)],
    )
    return pl.pallas_call(
        functools.partial(_combine_kernel, n_chunks=c),
        grid_spec=grid_spec,
        out_shape=jax.ShapeDtypeStruct((n, dm), F32),
        compiler_params=_params("parallel"),
        name="moe_combine_ln",
    )(*plan, slot, gates, *([ys] * n_y), x, g.reshape(1, dm), b.reshape(1, dm))


def _moe(x, xb, router_w, w13, w2, g, b):
    n, dm = x.shape
    n_slots = TOP_K * n + N_EXPERTS * MOE_GROUP
    gates, sel = _router(x, router_w)
    rank, rank_t, base, cnt = _rank(sel)
    as_counts = lambda t: t[:, 0, :N_EXPERTS].astype(jnp.int32)
    gstart, experts_plan, dispatch_plan, combine_plan = _moe_plan(
        as_counts(base), as_counts(cnt), n_slots // MOE_GROUP)
    gstart = gstart.astype(F32)
    slot = rank + jnp.zeros((LANES,), F32).at[:N_EXPERTS].set(gstart)[None, :]
    slot_t = rank_t + gstart[:, None]
    xs = _dispatch(dispatch_plan, slot_t, xb, n_slots)
    ys = _experts(experts_plan, xs, w13, w2)
    return _combine(combine_plan, slot, gates, ys, x, g, b)


def _trunk(x, mem, w):
    b, s, dm = x.shape
    n = b * s
    mem2 = mem.reshape(-1, dm)
    for i in range(DEPTH):
        j = i // 2
        even = i % 2 == 0
        x2 = x.reshape(n, dm)
        if even:
            u, qkv = _matmul(x2, w['e_w_in'][j], (2 * CONV_CH, QKV_EVEN), (F32, BF16), "even_in_proj")
            a_out = _conv_module(u.reshape(b, s, 2 * CONV_CH), w['e_dw_w'][j], w['e_dw_b'][j],
                                 w['e_cn_g'][j], w['e_cn_b'][j])
            b_out = _mixture_of_dilations(qkv.reshape(b, s, QKV_EVEN))
            w_out = w['e_w_out'][j]
            mixed, w_mix = [a_out, b_out], [w_out[:CONV_CH], w_out[CONV_CH:]]
        else:
            qt, k, vt = _odd_in_proj(x, w['o_w_in'][j], w['o_q_norm'][j], w['o_k_norm'][j])
            mixed, w_mix = [_gqa(qt, k, vt)], [w['o_w_out'][j]]
        (kv,) = _matmul(mem2, w['x_wkv'][i], (2 * dm,), (BF16,), "memory_kv_proj", tm=256)
        outs = _post_mixer(mixed, w_mix, x, (w['ln_g'][i, 0], w['ln_b'][i, 0]),
                           kv.reshape(b, -1, 2 * dm), w['x_wq'][i], w['x_wo'][i],
                           (w['ln_g'][i, 1], w['ln_b'][i, 1]), with_bf16_copy=not even)
        x2 = outs[0].reshape(n, dm)
        if even:
            x2 = _ffn(x2, w['f_w13'][j], w['f_w2'][j], w['ln_g'][i, 2], w['ln_b'][i, 2])
        else:
            x2 = _moe(x2, outs[1].reshape(n, dm), w['m_router'][j], w['m_w13'][j], w['m_w2'][j],
                      w['ln_g'][i, 2], w['ln_b'][i, 2])
        x = x2.reshape(b, s, dm)
    return x


_MXU_WEIGHTS = ('e_w_in', 'e_w_out', 'f_w13', 'f_w2', 'o_w_in', 'o_w_out', 'm_w13', 'm_w2',
                'x_wq', 'x_wkv', 'x_wo')


def kernel(x_prompt, x_sample, mem_prompt, mem_sample, e_w_in, e_dw_w, e_dw_b, e_cn_g, e_cn_b,
           e_w_out, f_w13, f_w2, o_w_in, o_q_norm, o_k_norm, o_w_out, m_router, m_w13, m_w2,
           x_wq, x_wkv, x_wo, ln_g, ln_b):
    w = dict(e_w_in=e_w_in, e_dw_w=e_dw_w, e_dw_b=e_dw_b, e_cn_g=e_cn_g, e_cn_b=e_cn_b,
             e_w_out=e_w_out, f_w13=f_w13, f_w2=f_w2, o_w_in=o_w_in, o_q_norm=o_q_norm,
             o_k_norm=o_k_norm, o_w_out=o_w_out, m_router=m_router, m_w13=m_w13, m_w2=m_w2,
             x_wq=x_wq, x_wkv=x_wkv, x_wo=x_wo, ln_g=ln_g, ln_b=ln_b)
    for name in _MXU_WEIGHTS:
        w[name] = w[name].astype(BF16)
    return (_trunk(x_prompt, mem_prompt, w), _trunk(x_sample, mem_sample, w))
```

```python
import functools
import math

import numpy as np
import jax
import jax.numpy as jnp
from jax import lax
from jax.experimental import pallas as pl
from jax.experimental.pallas import tpu as pltpu

F32 = jnp.float32
BF16 = jnp.bfloat16

D_MODEL = 1024
DEPTH = 2
GRID_W = 64
CONV_CH = D_MODEL // 2
CONV_WIDTH = 31
B_DILATIONS = (1, 4, 16)
B_GROUPS = len(B_DILATIONS)
B_HEADS = 4
B_HD = 128
B_WIDTH = B_HEADS * B_HD
B_HALF = 64
QKV_EVEN = 3 * B_GROUPS * B_WIDTH
C_HEADS = 8
C_KV_HEADS = 2
C_GROUP = C_HEADS // C_KV_HEADS
C_HD = 128
ROPE_THETA = 10000.0
X_HEADS = 4
X_HD = D_MODEL // X_HEADS
FF_DENSE = 2816
N_EXPERTS = 8
FF_EXPERT = 3584
DN_ALPHA = (2 * DEPTH) ** 0.25
LN_EPS = 1e-5
RMS_EPS = 1e-6
NEG = -1e30

LANES = 128
SUBLANES = 8
V7X_VMEM_LIMIT = 56 * 1024 * 1024


def _params(*semantics):
    return pltpu.CompilerParams(dimension_semantics=semantics,
                                vmem_limit_bytes=V7X_VMEM_LIMIT)


def _layer_norm(y, g, b):
    mu = jnp.mean(y, -1, keepdims=True)
    yc = y - mu
    var = jnp.mean(yc * yc, -1, keepdims=True)
    return yc * lax.rsqrt(var + LN_EPS) * g + b


def _dot(a, b):
    return jnp.dot(a, b, preferred_element_type=F32)


def _dot_nt(a, b):
    return lax.dot_general(a, b, (((1,), (1,)), ((), ())), preferred_element_type=F32)


def _matmul_kernel(x_ref, w_ref, *o_refs, splits, chunk, glu_first):
    xb = x_ref[...].astype(BF16)
    col = 0
    for idx, (o_ref, width) in enumerate(zip(o_refs, splits)):
        if glu_first and idx == 0:
            half = width // 2
            for c in range(0, half, chunk):
                a = _dot(xb, w_ref[:, col + c:col + c + chunk])
                gate = _dot(xb, w_ref[:, col + half + c:col + half + c + chunk])
                o_ref[:, c:c + chunk] = (a * jax.nn.sigmoid(gate)).astype(o_ref.dtype)
        else:
            for c in range(0, width, chunk):
                o_ref[:, c:c + chunk] = _dot(xb, w_ref[:, col + c:col + c + chunk]).astype(o_ref.dtype)
        col += width


def _matmul(x, w, splits, dtypes, name, tm=512, chunk=512, glu_first=False):
    n, k = x.shape
    widths = [s // 2 if glu_first and i == 0 else s for i, s in enumerate(splits)]
    return pl.pallas_call(
        functools.partial(_matmul_kernel, splits=splits, chunk=chunk, glu_first=glu_first),
        grid=(n // tm,),
        in_specs=[pl.BlockSpec((tm, k), lambda i: (i, 0)),
                  pl.BlockSpec(w.shape, lambda i: (0, 0))],
        out_specs=[pl.BlockSpec((tm, s), lambda i: (i, 0)) for s in widths],
        out_shape=[jax.ShapeDtypeStruct((n, s), dt) for s, dt in zip(widths, dtypes)],
        compiler_params=_params("parallel"),
        name=name,
    )(x, w)


CONV_HALO = 16
CONV_ROWS = 64


def _conv_kernel(prev_ref, cur_ref, next_ref, w_ref, b_ref, g_ref, be_ref, o_ref, hwin, *, ts):
    s = pl.program_id(1)
    last = pl.num_programs(1) - 1

    hwin[0, 0:CONV_HALO, :] = jnp.where(s > 0, prev_ref[...], 0.0)
    hwin[0, CONV_HALO:CONV_HALO + ts, :] = cur_ref[...]
    hwin[0, CONV_HALO + ts:, :] = jnp.where(s < last, next_ref[...], 0.0)
    shifted_rows = ts + 2 * CONV_HALO - SUBLANES
    for r in range(1, SUBLANES):
        hwin[r, 0:shifted_rows, :] = hwin[0, r:r + shifted_rows, :]
    pad = CONV_WIDTH // 2
    for c in range(0, ts, CONV_ROWS):
        acc = jnp.broadcast_to(b_ref[...], (CONV_ROWS, CONV_CH))
        for j in range(CONV_WIDTH):
            start = CONV_HALO + c + j - pad
            r = start % SUBLANES
            acc = acc + w_ref[j:j + 1, :] * hwin[r, start - r:start - r + CONV_ROWS, :]
        y = _layer_norm(acc, g_ref[...], be_ref[...])
        o_ref[c:c + CONV_ROWS, :] = (y * jax.nn.sigmoid(y)).astype(o_ref.dtype)


def _conv_module(u, dw_w, dw_b, cn_g, cn_b, ts=512):
    b, s, _ = u.shape
    hb = ts // CONV_HALO
    nhalo = s // CONV_HALO
    row = lambda a: a.reshape(1, CONV_CH)
    return pl.pallas_call(
        functools.partial(_conv_kernel, ts=ts),
        grid=(b, s // ts),
        in_specs=[
            pl.BlockSpec((None, CONV_HALO, CONV_CH),
                         lambda i, j: (i, jnp.maximum(j * hb - 1, 0), 0)),
            pl.BlockSpec((None, ts, CONV_CH), lambda i, j: (i, j, 0)),
            pl.BlockSpec((None, CONV_HALO, CONV_CH),
                         lambda i, j: (i, jnp.minimum((j + 1) * hb, nhalo - 1), 0)),
            pl.BlockSpec((CONV_WIDTH, CONV_CH), lambda i, j: (0, 0)),
            pl.BlockSpec((1, CONV_CH), lambda i, j: (0, 0)),
            pl.BlockSpec((1, CONV_CH), lambda i, j: (0, 0)),
            pl.BlockSpec((1, CONV_CH), lambda i, j: (0, 0)),
        ],
        out_specs=pl.BlockSpec((None, ts, CONV_CH), lambda i, j: (i, j, 0)),
        out_shape=jax.ShapeDtypeStruct((b, s, CONV_CH), BF16),
        scratch_shapes=[pltpu.VMEM((SUBLANES, ts + 2 * CONV_HALO, CONV_CH), F32)],
        compiler_params=_params("parallel", "parallel"),
        name="conv_module",
    )(u, u, u, dw_w, row(dw_b), row(cn_g), row(cn_b))


BAND_Q = 128
BAND_UNROLL = 16
DIL_TILE = 2048


def _band_block(q, k, v, slope, band_bias, key_token0, key_step, seq):
    s = _dot_nt(q, k) * (1.0 / math.sqrt(B_HD)) + slope * band_bias
    col = lax.broadcasted_iota(jnp.int32, (1, 2 * BAND_Q), 1)
    token = key_token0 + col * key_step
    in_seq = token.astype(jnp.uint32) < jnp.uint32(seq)
    s = jnp.where(in_seq, s, NEG)
    m = jnp.max(s, -1, keepdims=True)
    p = jnp.exp(s - m)
    l = jnp.sum(p, -1, keepdims=True)
    return _dot(p.astype(BF16), v) / l, m + jnp.log(l)


def _dilated_kernel(slope_ref, *refs, seq):
    q_refs = refs[0:3]
    kv_refs = refs[3:21]
    o_ref = refs[21]
    kw0, vw0, qf1, kw1, vw1, qf2, kw2, vw2, og0, og1, og2, lg0, lg1, lg2 = refs[22:]
    head = pl.program_id(1)
    t0 = pl.program_id(2) * DIL_TILE

    row = lax.broadcasted_iota(jnp.int32, (BAND_Q, 2 * BAND_Q), 0)
    col = lax.broadcasted_iota(jnp.int32, (BAND_Q, 2 * BAND_Q), 1)
    dist = jnp.abs(col - row - B_HALF)
    band_bias = jnp.where(dist <= B_HALF, -dist.astype(F32), NEG)

    def stage(win, group, which, dtype):
        halo = B_HALF * B_DILATIONS[group]
        prev_ref, cur_ref, next_ref = kv_refs[group * 6 + which * 3:group * 6 + which * 3 + 3]
        win[0:halo, :] = prev_ref[...].astype(dtype)
        win[halo:halo + DIL_TILE, :] = cur_ref[...].astype(dtype)
        win[halo + DIL_TILE:, :] = next_ref[...].astype(dtype)

    def slope_of(group):
        return slope_ref[group * B_HEADS + head] * float(B_DILATIONS[group])

    stage(kw0, 0, 0, BF16)
    stage(vw0, 0, 1, BF16)
    slope0 = slope_of(0)

    def body0(i, carry):
        i0 = pl.multiple_of(i * BAND_Q, BAND_Q)
        o, lse = _band_block(q_refs[0][pl.ds(i0, BAND_Q), :], kw0[pl.ds(i0, 2 * BAND_Q), :],
                             vw0[pl.ds(i0, 2 * BAND_Q), :], slope0, band_bias,
                             t0 + i0 - B_HALF, 1, seq)
        og0[pl.ds(i0, BAND_Q), :] = o
        lg0[pl.ds(i0, BAND_Q), :] = jnp.broadcast_to(lse, (BAND_Q, B_HD))
        return carry

    lax.fori_loop(0, DIL_TILE // BAND_Q, body0, 0, unroll=BAND_UNROLL)

    for group, qf, kw, vw, og, lg in ((1, qf1, kw1, vw1, og1, lg1), (2, qf2, kw2, vw2, og2, lg2)):
        d = B_DILATIONS[group]
        nq = DIL_TILE // d
        qf[...] = q_refs[group][...].astype(F32)
        stage(kw, group, 0, F32)
        stage(vw, group, 1, F32)
        slope = slope_of(group)

        def body(r, carry, d=d, nq=nq, qf=qf, kw=kw, vw=vw, og=og, lg=lg, slope=slope):
            qr = qf[pl.ds(r, nq, stride=d), :].astype(BF16)
            kr = kw[pl.ds(r, nq + 2 * B_HALF, stride=d), :].astype(BF16)
            vr = vw[pl.ds(r, nq + 2 * B_HALF, stride=d), :].astype(BF16)
            for i0 in range(0, nq, BAND_Q):
                o, lse = _band_block(qr[i0:i0 + BAND_Q], kr[i0:i0 + 2 * BAND_Q],
                                     vr[i0:i0 + 2 * BAND_Q], slope, band_bias,
                                     t0 + r + (i0 - B_HALF) * d, d, seq)
                og[pl.ds(r + i0 * d, BAND_Q, stride=d), :] = o
                lg[pl.ds(r + i0 * d, BAND_Q, stride=d), :] = jnp.broadcast_to(lse, (BAND_Q, B_HD))
            return carry

        lax.fori_loop(0, d, body, 0, unroll=max(1, BAND_UNROLL * BAND_Q * d // DIL_TILE))

    l0, l1, l2 = lg0[...], lg1[...], lg2[...]
    top = jnp.maximum(jnp.maximum(l0, l1), l2)
    w0, w1, w2 = jnp.exp(l0 - top), jnp.exp(l1 - top), jnp.exp(l2 - top)
    mix = (w0 * og0[...] + w1 * og1[...] + w2 * og2[...]) / (w0 + w1 + w2)
    o_ref[...] = mix.astype(o_ref.dtype)


def _mixture_of_dilations(qkv):
    b, s, _ = qkv.shape
    n = B_GROUPS * B_HEADS
    slopes = jnp.asarray(np.array([2.0 ** (-8.0 * (i + 1) / n) for i in range(n)], np.float32))
    n_tiles = s // DIL_TILE

    def column(which, group):
        return (which * B_GROUPS + group) * B_HEADS

    def tile_spec(col):
        return pl.BlockSpec((None, DIL_TILE, B_HD), lambda i, h, t: (i, t, col + h))

    def halo_specs(col, group):
        halo = B_HALF * B_DILATIONS[group]
        per_tile = DIL_TILE // halo
        last = s // halo - 1
        return [pl.BlockSpec((None, halo, B_HD),
                             lambda i, h, t: (i, jnp.maximum(t * per_tile - 1, 0), col + h)),
                tile_spec(col),
                pl.BlockSpec((None, halo, B_HD),
                             lambda i, h, t: (i, jnp.minimum((t + 1) * per_tile, last), col + h))]

    in_specs = [pl.BlockSpec(memory_space=pltpu.SMEM)]
    in_specs += [tile_spec(column(0, g)) for g in range(B_GROUPS)]
    for g in range(B_GROUPS):
        in_specs += halo_specs(column(1, g), g) + halo_specs(column(2, g), g)
    scratch = []
    for g, d in enumerate(B_DILATIONS):
        win = (DIL_TILE + 2 * B_HALF * d, B_HD)
        if g == 0:
            scratch += [pltpu.VMEM(win, BF16)] * 2
        else:
            scratch += [pltpu.VMEM((DIL_TILE, B_HD), F32), pltpu.VMEM(win, F32), pltpu.VMEM(win, F32)]
    scratch += [pltpu.VMEM((DIL_TILE, B_HD), F32)] * (2 * B_GROUPS)
    return pl.pallas_call(
        functools.partial(_dilated_kernel, seq=s),
        grid=(b, B_HEADS, n_tiles),
        in_specs=in_specs,
        out_specs=pl.BlockSpec((None, DIL_TILE, B_HD), lambda i, h, t: (i, t, h)),
        out_shape=jax.ShapeDtypeStruct((b, s, B_WIDTH), BF16),
        scratch_shapes=scratch,
        compiler_params=_params("parallel", "parallel", "parallel"),
        name="dilated_mixture_attn",
    )(slopes, *([qkv] * (1 + 3 + 6 * B_GROUPS - 1)))


def _split_bf16(a):
    hi = a.astype(BF16)
    return hi, (a - hi.astype(F32)).astype(BF16)


def _route_top2(x, wh, wl):
    xh, xl = _split_bf16(x)
    logits = _dot(xh, wh) + (_dot(xh, wl) + _dot(xl, wh))
    lane = lax.broadcasted_iota(jnp.int32, logits.shape, 1)
    logits = jnp.where(lane < N_EXPERTS, logits, -jnp.inf)
    m1 = jnp.max(logits, -1, keepdims=True)
    i1 = jnp.min(jnp.where(logits == m1, lane, LANES), -1, keepdims=True)
    rest = jnp.where(lane == i1, -jnp.inf, logits)
    m2 = jnp.max(rest, -1, keepdims=True)
    i2 = jnp.min(jnp.where(rest == m2, lane, LANES), -1, keepdims=True)
    e = jnp.exp(m2 - m1)
    g1 = 1.0 / (1.0 + e)
    g2 = e / (1.0 + e)
    gates = jnp.where(lane == i1, g1, 0.0) + jnp.where(lane == i2, g2, 0.0)
    return gates, jnp.where((lane == i1) | (lane == i2), 1.0, 0.0).astype(BF16)


def _post_mixer_kernel(*refs, n_in, route):
    ins = refs[:n_in]
    ws = refs[n_in:2 * n_in]
    n_fixed = 2 * n_in + 9
    x_ref, g0_ref, b0_ref, k_ref, v_ref, wq_ref, wo_ref, g_ref, b_ref = refs[2 * n_in:n_fixed]
    if route:
        wh_ref, wl_ref, o_ref, ob_ref, gate_ref, sel_ref = refs[n_fixed:]
    else:
        (o_ref,) = refs[n_fixed:]
    y = DN_ALPHA * x_ref[...]
    for a_ref, w_ref in zip(ins, ws):
        y = y + _dot(a_ref[...], w_ref[...])
    x = _layer_norm(y, g0_ref[...], b0_ref[...])
    q = (_dot(x.astype(BF16), wq_ref[...]) * (1.0 / math.sqrt(X_HD))).astype(BF16)
    heads = []
    for h in range(X_HEADS):
        sl = slice(h * X_HD, (h + 1) * X_HD)
        s = _dot_nt(q[:, sl], k_ref[:, sl])
        m = jnp.max(s, -1, keepdims=True)
        p = jnp.exp(s - m)
        l = jnp.sum(p, -1, keepdims=True)
        heads.append((_dot(p.astype(BF16), v_ref[:, sl]) / l).astype(BF16))
    o = jnp.concatenate(heads, -1)
    y = _layer_norm(DN_ALPHA * x + _dot(o, wo_ref[...]), g_ref[...], b_ref[...])
    o_ref[...] = y
    if route:
        ob_ref[...] = y.astype(BF16)
        gate_ref[...], sel_ref[...] = _route_top2(y, wh_ref[...], wl_ref[...])


def _post_mixer(ins, ws, x, ln0, kv, wq, wo, ln1, router_w=None, tm=1024):
    bsz, s, dm = x.shape
    m = kv.shape[1]
    rows = lambda width: pl.BlockSpec((None, tm, width), lambda i, j: (i, j, 0))
    whole = lambda a: pl.BlockSpec(a.shape, lambda i, j: (0, 0))
    vec = pl.BlockSpec((1, dm), lambda i, j: (0, 0))
    out_specs = [rows(dm)]
    out_shape = [jax.ShapeDtypeStruct((bsz, s, dm), F32)]
    route_args = []
    if router_w is not None:
        route_args = list(_split_bf16(jnp.zeros((dm, LANES), F32).at[:, :N_EXPERTS].set(router_w)))
        out_specs += [rows(dm), rows(LANES), rows(LANES)]
        out_shape += [jax.ShapeDtypeStruct((bsz, s, dm), BF16),
                      jax.ShapeDtypeStruct((bsz, s, LANES), F32),
                      jax.ShapeDtypeStruct((bsz, s, LANES), BF16)]
    as_row = lambda t: t.reshape(1, dm)
    return pl.pallas_call(
        functools.partial(_post_mixer_kernel, n_in=len(ins), route=router_w is not None),
        grid=(bsz, s // tm),
        in_specs=([rows(a.shape[-1]) for a in ins] + [whole(w) for w in ws]
                  + [rows(dm), vec, vec,
                     pl.BlockSpec((None, m, dm), lambda i, j: (i, 0, 0)),
                     pl.BlockSpec((None, m, dm), lambda i, j: (i, 0, 1)),
                     whole(wq), whole(wo), vec, vec] + [whole(w) for w in route_args]),
        out_specs=out_specs,
        out_shape=out_shape,
        compiler_params=_params("parallel", "parallel"),
        name="mixer_out_xattn_ln",
    )(*ins, *ws, x, as_row(ln0[0]), as_row(ln0[1]), kv, kv, wq, wo, as_row(ln1[0]), as_row(ln1[1]),
      *route_args)


def _ffn_kernel(x_ref, w13_ref, w2_ref, g_ref, b_ref, o_ref):
    x = x_ref[...]
    xb = x.astype(BF16)
    ff = w2_ref.shape[0]
    a = _dot(xb, w13_ref[:, :ff])
    gate = _dot(xb, w13_ref[:, ff:])
    h = (a * jax.nn.sigmoid(a) * gate).astype(BF16)
    o_ref[...] = _layer_norm(DN_ALPHA * x + _dot(h, w2_ref[...]), g_ref[...], b_ref[...])


def _ffn(x, w13, w2, g, b, tm=512):
    n, dm = x.shape
    resident = lambda w: pl.BlockSpec(w.shape, lambda i: (0, 0), pipeline_mode=pl.Buffered(1))
    return pl.pallas_call(
        _ffn_kernel,
        grid=(n // tm,),
        in_specs=[pl.BlockSpec((tm, dm), lambda i: (i, 0)),
                  resident(w13), resident(w2),
                  pl.BlockSpec((1, dm), lambda i: (0, 0)),
                  pl.BlockSpec((1, dm), lambda i: (0, 0))],
        out_specs=pl.BlockSpec((tm, dm), lambda i: (i, 0)),
        out_shape=jax.ShapeDtypeStruct((n, dm), F32),
        compiler_params=_params("parallel"),
        name="dense_swiglu_ln",
    )(x, w13, w2, g.reshape(1, dm), b.reshape(1, dm))


def _rope_tables(seq_len):
    rows = seq_len // GRID_W
    row = jnp.broadcast_to(jnp.arange(rows, dtype=F32)[:, None], (rows, GRID_W)).reshape(-1)
    col = jnp.broadcast_to(jnp.arange(GRID_W, dtype=F32)[None, :], (rows, GRID_W)).reshape(-1)
    axis_dim = C_HD // 2
    freqs = ROPE_THETA ** (-jnp.arange(0, axis_dim, 2, dtype=F32) / axis_dim)
    ang_r = row[:, None] * freqs[None, :]
    ang_c = col[:, None] * freqs[None, :]
    ang = jnp.concatenate([ang_r, ang_r, ang_c, ang_c], -1)
    cos, sin = jnp.cos(ang), jnp.sin(ang)
    first = (jnp.arange(C_HD) % axis_dim) < axis_dim // 2
    sin_up = jnp.where(first[None, :], -sin, 0.0)
    sin_dn = jnp.where(first[None, :], 0.0, sin)
    return cos, sin_up, sin_dn


def _odd_in_kernel(x_ref, w_ref, kn_ref, cos_ref, su_ref, sd_ref, cq_ref, uq_ref, dq_ref,
                   qt_ref, k_ref, vt_ref):
    xb = x_ref[...].astype(BF16)
    cos, su, sd = cos_ref[...], su_ref[...], sd_ref[...]
    cq, uq, dq = cq_ref[...], uq_ref[...], dq_ref[...]
    quarter = C_HD // 4

    def norm_rope(h, gain):
        h = h * lax.rsqrt(jnp.mean(h * h, -1, keepdims=True) + RMS_EPS) * gain
        up = pltpu.roll(h, C_HD - quarter, 1)
        dn = pltpu.roll(h, quarter, 1)
        return h * cos + up * su + dn * sd

    def norm_rope_t(ht):
        inv = lax.rsqrt(jnp.mean(ht * ht, 0, keepdims=True) + RMS_EPS)
        up = jnp.concatenate([ht[quarter:], ht[:quarter]], 0)
        dn = jnp.concatenate([ht[C_HD - quarter:], ht[:C_HD - quarter]], 0)
        return (ht * cq + up * uq + dn * dq) * inv

    nq, nk = C_HEADS * C_HD, C_KV_HEADS * C_HD
    pair = 2 * C_HD
    for i in range(0, C_HEADS, 2):
        h2 = _dot(xb, w_ref[:, i * C_HD:i * C_HD + pair])
        for j in range(2):
            ht = h2[:, j * C_HD:(j + 1) * C_HD].T
            qt_ref[(i + j) * C_HD:(i + j + 1) * C_HD, :] = norm_rope_t(ht).astype(BF16)
    for i in range(0, C_KV_HEADS, 2):
        h2 = _dot(xb, w_ref[:, nq + i * C_HD:nq + i * C_HD + pair])
        for j in range(2):
            k_ref[:, (i + j) * C_HD:(i + j + 1) * C_HD] = norm_rope(
                h2[:, j * C_HD:(j + 1) * C_HD], kn_ref[...]).astype(BF16)
        v2 = _dot(xb, w_ref[:, nq + nk + i * C_HD:nq + nk + i * C_HD + pair])
        vt_ref[i * C_HD:i * C_HD + pair, :] = v2.T.astype(BF16)


def _odd_in_proj(x, w, qn, kn, tm=512):
    b, s, dm = x.shape
    nq, nk = C_HEADS * C_HD, C_KV_HEADS * C_HD
    cos, su, sd = _rope_tables(s)
    q_scale = math.log2(math.e) / math.sqrt(C_HD)
    quarter = C_HD // 4
    cq = (cos * qn[None, :] * q_scale).T
    uq = (su * jnp.roll(qn, -quarter)[None, :] * q_scale).T
    dq = (sd * jnp.roll(qn, quarter)[None, :] * q_scale).T
    tab = pl.BlockSpec((tm, C_HD), lambda i, j: (j, 0))
    tab_t = pl.BlockSpec((C_HD, tm), lambda i, j: (0, j))
    vec = pl.BlockSpec((1, C_HD), lambda i, j: (0, 0))
    return pl.pallas_call(
        _odd_in_kernel,
        grid=(b, s // tm),
        in_specs=[pl.BlockSpec((None, tm, dm), lambda i, j: (i, j, 0)),
                  pl.BlockSpec(w.shape, lambda i, j: (0, 0)),
                  vec, tab, tab, tab, tab_t, tab_t, tab_t],
        out_specs=[pl.BlockSpec((None, nq, tm), lambda i, j: (i, 0, j)),
                   pl.BlockSpec((None, tm, nk), lambda i, j: (i, j, 0)),
                   pl.BlockSpec((None, nk, tm), lambda i, j: (i, 0, j))],
        out_shape=[jax.ShapeDtypeStruct((b, nq, s), BF16),
                   jax.ShapeDtypeStruct((b, s, nk), BF16),
                   jax.ShapeDtypeStruct((b, nk, s), BF16)],
        compiler_params=_params("parallel", "parallel"),
        name="gqa_in_proj_norm_rope",
    )(x, w, kn.reshape(1, C_HD), cos, su, sd, cq, uq, dq)


def _gqa_kernel(qt_ref, k_ref, vt_ref, o_ref, acc_sc, st_sc, *, seq, tq, tk):
    qt = jnp.concatenate([qt_ref[h * C_HD:(h + 1) * C_HD, :] for h in range(C_GROUP)], 1)
    cols = C_GROUP * tq
    n = seq // tk
    acc_sc[...] = jnp.zeros(acc_sc.shape, F32)

    def scores(kc):
        k0 = pl.multiple_of(kc * tk, tk)
        return _dot(k_ref[pl.ds(k0, tk), :], qt)

    def consume(kc, st, m_old, l_old):
        k0 = pl.multiple_of(kc * tk, tk)
        m_new = jnp.maximum(m_old, jnp.max(st, 0, keepdims=True))
        alpha = jnp.exp2(m_old - m_new)
        p = jnp.exp2(st - m_new)
        l_new = alpha * l_old + jnp.sum(p, 0, keepdims=True)
        acc_sc[...] = alpha * acc_sc[...] + _dot(vt_ref[:, pl.ds(k0, tk)], p.astype(BF16))
        return m_new, l_new

    st_a, st_b = st_sc.at[0], st_sc.at[1]
    st_a[...] = scores(0)

    def body(i, carry):
        kc = 2 * i
        st_b[...] = scores(kc + 1)
        m, l = consume(kc, st_a[...], *carry)
        st_a[...] = scores(jnp.minimum(kc + 2, n - 1))
        return consume(kc + 1, st_b[...], m, l)

    init = (jnp.full((1, cols), -jnp.inf, F32), jnp.zeros((1, cols), F32))
    _, l = lax.fori_loop(0, n // 2, body, init)
    out = acc_sc[...] / l
    for h in range(C_GROUP):
        o_ref[:, h * C_HD:(h + 1) * C_HD] = out[:, h * tq:(h + 1) * tq].T.astype(o_ref.dtype)


def _gqa(qt, k, vt, tq=512, tk=512):
    b, nq, s = qt.shape
    gw = C_GROUP * C_HD
    assert (s // tk) % 2 == 0
    return pl.pallas_call(
        functools.partial(_gqa_kernel, seq=s, tq=tq, tk=tk),
        grid=(b, C_KV_HEADS, s // tq),
        in_specs=[pl.BlockSpec((None, gw, tq), lambda i, h, j: (i, h, j)),
                  pl.BlockSpec((None, s, C_HD), lambda i, h, j: (i, 0, h)),
                  pl.BlockSpec((None, C_HD, s), lambda i, h, j: (i, h, 0))],
        out_specs=pl.BlockSpec((None, tq, gw), lambda i, h, j: (i, j, h)),
        out_shape=jax.ShapeDtypeStruct((b, s, nq), BF16),
        scratch_shapes=[pltpu.VMEM((C_HD, C_GROUP * tq), F32),
                        pltpu.VMEM((2, tk, C_GROUP * tq), F32)],
        compiler_params=_params("parallel", "parallel", "parallel"),
        name="gqa_flash",
    )(qt, k, vt)


TOP_K = 2
MOE_CHUNK = 512
MOE_TILE = 256
MOE_SPAN = MOE_CHUNK // MOE_TILE + 1
MOE_PACK = 3
MOE_GROUP = 1024
UNSELECTED = -float(2 ** 26)


def _rank_kernel(sel_ref, rank_ref, rank_t_ref, base_ref, cnt_ref, carry):
    @pl.when(pl.program_id(0) == 0)
    def _():
        carry[...] = jnp.zeros(carry.shape, F32)

    sel = sel_ref[...]
    n = sel.shape[0]
    row = lax.broadcasted_iota(jnp.int32, (n, n), 0)
    col = lax.broadcasted_iota(jnp.int32, (n, n), 1)
    earlier = jnp.where(row > col, 1.0, 0.0).astype(BF16)
    before = _dot(earlier, sel) + carry[...]
    chosen = sel.astype(F32)
    rank = jnp.where(chosen > 0.0, before, UNSELECTED)
    rank_ref[...] = rank
    rank_t_ref[...] = rank.T[:SUBLANES, :]
    cnt = jnp.sum(chosen, 0, keepdims=True)
    base_ref[...] = carry[...]
    cnt_ref[...] = cnt
    carry[...] += cnt


def _rank(sel):
    n = sel.shape[0]
    c = n // MOE_CHUNK
    stat = pl.BlockSpec((None, 1, LANES), lambda i: (i, 0, 0))
    return pl.pallas_call(
        _rank_kernel,
        grid=(c,),
        in_specs=[pl.BlockSpec((MOE_CHUNK, LANES), lambda i: (i, 0))],
        out_specs=[pl.BlockSpec((MOE_CHUNK, LANES), lambda i: (i, 0)),
                   pl.BlockSpec((SUBLANES, MOE_CHUNK), lambda i: (0, i)),
                   stat, stat],
        out_shape=[jax.ShapeDtypeStruct((n, LANES), F32),
                   jax.ShapeDtypeStruct((SUBLANES, n), F32),
                   jax.ShapeDtypeStruct((c, 1, LANES), F32),
                   jax.ShapeDtypeStruct((c, 1, LANES), F32)],
        scratch_shapes=[pltpu.VMEM((1, LANES), F32)],
        compiler_params=_params("arbitrary"),
        name="moe_rank",
    )(sel)


def _fill_forward(values, valid):
    idx = lax.cummax(jnp.where(valid, jnp.arange(values.shape[0]), -1))
    idx = jnp.where(idx < 0, jnp.argmax(valid), idx)
    return values[idx]


def _moe_plan(base, cnt, n_group_tiles):
    counts = base[-1] + cnt[-1]
    gsize = -(-counts // MOE_GROUP) * MOE_GROUP
    gend = jnp.cumsum(gsize)
    gstart = gend - gsize
    n_used = gend[-1] // MOE_GROUP
    group = jnp.minimum(jnp.arange(n_group_tiles), n_used - 1)
    group_expert = jnp.minimum(jnp.searchsorted(gend, group * MOE_GROUP, side='right'), N_EXPERTS - 1)
    experts = (group.astype(jnp.int32), group_expert.astype(jnp.int32),
               (jnp.arange(n_group_tiles) < n_used).astype(jnp.int32))
    n_chunks = base.shape[0]
    n_tiles = n_group_tiles * (MOE_GROUP // MOE_TILE)
    clip = lambda t: jnp.clip(t, 0, n_tiles - 1).astype(jnp.int32)
    as_i32 = lambda t: t.astype(jnp.int32)
    first_slot = gstart[None, :] + base
    t_first = first_slot // MOE_TILE
    t_last = (first_slot + cnt - 1) // MOE_TILE
    span = jnp.arange(MOE_SPAN)
    tiles = t_first[..., None] + span
    valid = (cnt[..., None] > 0) & (tiles <= t_last[..., None])
    td = tiles.transpose(1, 0, 2).reshape(-1)
    vd = valid.transpose(1, 0, 2).reshape(-1)
    seen = lax.cummax(jnp.where(vd, td, -1))
    prev = jnp.concatenate([jnp.full((1,), -1, seen.dtype), seen[:-1]])
    first = vd & (td != prev)
    chunk_d = jnp.broadcast_to(jnp.arange(n_chunks)[None, :, None], (N_EXPERTS, n_chunks, MOE_SPAN)).reshape(-1)
    expert_d = jnp.broadcast_to(jnp.arange(N_EXPERTS)[:, None, None], (N_EXPERTS, n_chunks, MOE_SPAN)).reshape(-1)
    n_items = N_EXPERTS * n_chunks + n_tiles
    n_valid = jnp.sum(vd)
    where = jnp.where(vd, jnp.cumsum(vd) - 1, n_items)
    take = jnp.minimum(jnp.arange(n_items), n_valid - 1)

    def compact(values):
        return jnp.zeros((n_items,), jnp.int32).at[where].set(as_i32(values), mode='drop')[take]

    live = jnp.arange(n_items) < n_valid
    item_tile, item_chunk, item_expert = clip(compact(td)), compact(chunk_d), compact(expert_d)
    item_first = (compact(first) == 1) & live
    idx = jnp.arange(n_items)
    rank_in_run = idx - lax.cummax(jnp.where(item_first, idx, 0))
    pos = rank_in_run % MOE_PACK
    opens = (pos == 0) & live
    pack = jnp.cumsum(opens) - 1
    n_packs = N_EXPERTS * n_chunks // MOE_PACK + n_tiles + 1
    n_open = jnp.sum(opens)
    keep = jnp.minimum(jnp.arange(n_packs), n_open - 1)
    pack_live = jnp.arange(n_packs) < n_open

    def per_pack(values, cond):
        where_p = jnp.where(cond & live, pack, n_packs)
        return jnp.zeros((n_packs,), jnp.int32).at[where_p].set(as_i32(values), mode='drop')

    chunks, valids = [], []
    for k in range(MOE_PACK):
        vk = (per_pack(jnp.ones_like(idx), pos == k) == 1) & pack_live
        chunks.append(_fill_forward(per_pack(item_chunk, pos == k), vk))
        valids.append(as_i32(vk))
    dispatch = (per_pack(item_tile, pos == 0)[keep], per_pack(item_expert, pos == 0)[keep],
                per_pack(rank_in_run == 0, pos == 0) * as_i32(pack_live), as_i32(pack_live),
                jnp.concatenate(chunks), jnp.concatenate(valids))
    tiles_ek = tiles.transpose(1, 2, 0).reshape(N_EXPERTS * MOE_SPAN, n_chunks)
    valid_ek = valid.transpose(1, 2, 0).reshape(N_EXPERTS * MOE_SPAN, n_chunks)
    combine = (clip(jax.vmap(_fill_forward)(tiles_ek, valid_ek)).reshape(-1), as_i32(valid_ek).reshape(-1))
    return gstart, experts, dispatch, combine


def _dispatch_kernel(tile_ref, expert_ref, first_ref, live_ref, chunk_ref, valid_ref, *refs):
    del chunk_ref
    slot_refs, x_refs = refs[:MOE_PACK], refs[MOE_PACK:2 * MOE_PACK]
    o_ref = refs[2 * MOE_PACK + 1]
    w = pl.program_id(0)
    n_packs = pl.num_programs(0)

    @pl.when(live_ref[w] == 1)
    def _():
        base = (tile_ref[w] * MOE_TILE).astype(F32)
        sub = lax.broadcasted_iota(jnp.int32, (MOE_TILE, MOE_CHUNK), 0).astype(F32)

        def rows(k):
            local = slot_refs[k][pl.ds(expert_ref[w], 1), :] - base
            onehot = jnp.where(sub == local, 1.0, 0.0).astype(BF16)
            return _dot(onehot, x_refs[k][...]).astype(BF16)

        @pl.when(first_ref[w] == 1)
        def _():
            o_ref[...] = rows(0)

        @pl.when(first_ref[w] == 0)
        def _():
            o_ref[...] += rows(0)

        for k in range(1, MOE_PACK):
            @pl.when(valid_ref[k * n_packs + w] == 1)
            def _(k=k):
                o_ref[...] += rows(k)


def _dispatch(plan, slot_t, xb, n_slots):
    n, dm = xb.shape
    n_packs = plan[0].shape[0]

    def chunk_of(k):
        return lambda w, tile, expert, first, live, chunk, valid: chunk[k * n_packs + w]

    slot_specs = [pl.BlockSpec((SUBLANES, MOE_CHUNK), lambda w, *p, f=chunk_of(k): (0, f(w, *p)))
                  for k in range(MOE_PACK)]
    x_specs = [pl.BlockSpec((MOE_CHUNK, dm), lambda w, *p, f=chunk_of(k): (f(w, *p), 0))
               for k in range(MOE_PACK)]
    grid_spec = pltpu.PrefetchScalarGridSpec(
        num_scalar_prefetch=6,
        grid=(n_packs,),
        in_specs=slot_specs + x_specs + [pl.BlockSpec(memory_space=pl.ANY)],
        out_specs=pl.BlockSpec((MOE_TILE, dm), lambda w, tile, *_: (tile[w], 0)),
    )
    return pl.pallas_call(
        _dispatch_kernel,
        grid_spec=grid_spec,
        out_shape=jax.ShapeDtypeStruct((n_slots, dm), BF16),
        input_output_aliases={6 + 2 * MOE_PACK: 0},
        compiler_params=_params("arbitrary"),
        name="moe_dispatch",
    )(*plan, *([slot_t] * MOE_PACK), *([xb] * MOE_PACK), jnp.zeros((n_slots, dm), BF16))


def _experts_kernel(tile_ref, expert_ref, live_ref, x_ref, w1_ref, w3_ref, w2_ref, o_ref, acc):
    del tile_ref, expert_ref
    j = pl.program_id(1)

    @pl.when(live_ref[pl.program_id(0)] == 1)
    def _():
        xb = x_ref[...]
        a = _dot(xb, w1_ref[...])
        gate = _dot(xb, w3_ref[...])
        y = _dot((a * jax.nn.sigmoid(a) * gate).astype(BF16), w2_ref[...])

        @pl.when(j == 0)
        def _():
            acc[...] = y

        @pl.when(j > 0)
        def _():
            acc[...] += y

        @pl.when(j == pl.num_programs(1) - 1)
        def _():
            o_ref[...] = acc[...].astype(BF16)


def _experts(plan, xs, w13, w2, tf=1792):
    n_slots, dm = xs.shape
    ff = w2.shape[1]
    nf = ff // tf

    def chunk(i, j, live):
        return j * live[i] + (nf - 1) * (1 - live[i])

    grid_spec = pltpu.PrefetchScalarGridSpec(
        num_scalar_prefetch=3,
        grid=(n_slots // MOE_GROUP, nf),
        in_specs=[pl.BlockSpec((MOE_GROUP, dm), lambda i, j, tile, te, live: (tile[i], 0)),
                  pl.BlockSpec((None, dm, tf), lambda i, j, tile, te, live: (te[i], 0, chunk(i, j, live))),
                  pl.BlockSpec((None, dm, tf), lambda i, j, tile, te, live: (te[i], 0, nf + chunk(i, j, live))),
                  pl.BlockSpec((None, tf, dm), lambda i, j, tile, te, live: (te[i], chunk(i, j, live), 0))],
        out_specs=pl.BlockSpec((MOE_GROUP, dm), lambda i, j, tile, te, live: (tile[i], 0)),
        scratch_shapes=[pltpu.VMEM((MOE_GROUP, dm), F32)],
    )
    return pl.pallas_call(
        _experts_kernel,
        grid_spec=grid_spec,
        out_shape=jax.ShapeDtypeStruct((n_slots, dm), BF16),
        compiler_params=_params("arbitrary", "arbitrary"),
        name="moe_experts_swiglu",
    )(*plan, xs, w13, w13, w2)


def _combine_kernel(tile_ref, valid_ref, slot_ref, gate_ref, *rest, n_chunks):
    n_y = N_EXPERTS * MOE_SPAN
    y_refs = rest[:n_y]
    x_ref, g_ref, b_ref, o_ref, acc = rest[n_y:]
    c = pl.program_id(0)
    acc[...] = DN_ALPHA * x_ref[...]
    col = lax.broadcasted_iota(jnp.int32, (MOE_CHUNK, MOE_TILE), 1).astype(F32)
    for e in range(N_EXPERTS):
        slot = slot_ref[:, e:e + 1]
        gate = gate_ref[:, e:e + 1]
        for k in range(MOE_SPAN):
            w = (e * MOE_SPAN + k) * n_chunks + c

            @pl.when(valid_ref[w] == 1)
            def _(w=w, slot=slot, gate=gate, y_ref=y_refs[e * MOE_SPAN + k]):
                local = slot - (tile_ref[w] * MOE_TILE).astype(F32)
                onehot = jnp.where(col == local, 1.0, 0.0).astype(BF16)
                acc[...] += gate * _dot(onehot, y_ref[...])

    o_ref[...] = _layer_norm(acc[...], g_ref[...], b_ref[...])


def _combine(plan, slot, gates, ys, x, g, b):
    n, dm = x.shape
    c = n // MOE_CHUNK
    n_y = N_EXPERTS * MOE_SPAN
    tok = lambda width: pl.BlockSpec((MOE_CHUNK, width), lambda i, *_: (i, 0))
    vec = pl.BlockSpec((1, dm), lambda i, *_: (0, 0))

    def tile_spec(ek):
        return pl.BlockSpec((MOE_TILE, dm), lambda i, tile, valid: (tile[ek * c + i], 0))

    grid_spec = pltpu.PrefetchScalarGridSpec(
        num_scalar_prefetch=2,
        grid=(c,),
        in_specs=[tok(LANES), tok(LANES)] + [tile_spec(ek) for ek in range(n_y)] + [tok(dm), vec, vec],
        out_specs=tok(dm),
        scratch_shapes=[pltpu.VMEM((MOE_CHUNK, dm), F32)],
    )
    return pl.pallas_call(
        functools.partial(_combine_kernel, n_chunks=c),
        grid_spec=grid_spec,
        out_shape=jax.ShapeDtypeStruct((n, dm), F32),
        compiler_params=_params("parallel"),
        name="moe_combine_ln",
    )(*plan, slot, gates, *([ys] * n_y), x, g.reshape(1, dm), b.reshape(1, dm))


def _moe(x, xb, gates, sel, w13, w2, g, b):
    n, dm = x.shape
    n_slots = TOP_K * n + N_EXPERTS * MOE_GROUP
    rank, rank_t, base, cnt = _rank(sel)
    as_counts = lambda t: t[:, 0, :N_EXPERTS].astype(jnp.int32)
    gstart, experts_plan, dispatch_plan, combine_plan = _moe_plan(
        as_counts(base), as_counts(cnt), n_slots // MOE_GROUP)
    gstart = gstart.astype(F32)
    slot = rank + jnp.zeros((LANES,), F32).at[:N_EXPERTS].set(gstart)[None, :]
    slot_t = rank_t + gstart[:, None]
    xs = _dispatch(dispatch_plan, slot_t, xb, n_slots)
    ys = _experts(experts_plan, xs, w13, w2)
    return _combine(combine_plan, slot, gates, ys, x, g, b)


def _trunk(x, mem, w):
    b, s, dm = x.shape
    n = b * s
    mem2 = mem.reshape(-1, dm)
    for i in range(DEPTH):
        j = i // 2
        even = i % 2 == 0
        x2 = x.reshape(n, dm)
        if even:
            u, qkv = _matmul(x2, w['e_w_in'][j], (2 * CONV_CH, QKV_EVEN), (F32, BF16), "even_in_proj",
                             glu_first=True)
            a_out = _conv_module(u.reshape(b, s, CONV_CH), w['e_dw_w'][j], w['e_dw_b'][j],
                                 w['e_cn_g'][j], w['e_cn_b'][j])
            b_out = _mixture_of_dilations(qkv.reshape(b, s, QKV_EVEN))
            w_out = w['e_w_out'][j]
            mixed, w_mix = [a_out, b_out], [w_out[:CONV_CH], w_out[CONV_CH:]]
        else:
            qt, k, vt = _odd_in_proj(x, w['o_w_in'][j], w['o_q_norm'][j], w['o_k_norm'][j])
            mixed, w_mix = [_gqa(qt, k, vt)], [w['o_w_out'][j]]
        (kv,) = _matmul(mem2, w['x_wkv'][i], (2 * dm,), (BF16,), "memory_kv_proj", tm=256)
        outs = _post_mixer(mixed, w_mix, x, (w['ln_g'][i, 0], w['ln_b'][i, 0]),
                           kv.reshape(b, -1, 2 * dm), w['x_wq'][i], w['x_wo'][i],
                           (w['ln_g'][i, 1], w['ln_b'][i, 1]),
                           router_w=None if even else w['m_router'][j])
        x2 = outs[0].reshape(n, dm)
        if even:
            x2 = _ffn(x2, w['f_w13'][j], w['f_w2'][j], w['ln_g'][i, 2], w['ln_b'][i, 2])
        else:
            xb, gates, sel = outs[1].reshape(n, dm), outs[2].reshape(n, LANES), outs[3].reshape(n, LANES)
            x2 = _moe(x2, xb, gates, sel, w['m_w13'][j], w['m_w2'][j], w['ln_g'][i, 2], w['ln_b'][i, 2])
        x = x2.reshape(b, s, dm)
    return x


_MXU_WEIGHTS = ('e_w_in', 'e_w_out', 'f_w13', 'f_w2', 'o_w_in', 'o_w_out', 'm_w13', 'm_w2',
                'x_wq', 'x_wkv', 'x_wo')


def kernel(x_prompt, x_sample, mem_prompt, mem_sample, e_w_in, e_dw_w, e_dw_b, e_cn_g, e_cn_b,
           e_w_out, f_w13, f_w2, o_w_in, o_q_norm, o_k_norm, o_w_out, m_router, m_w13, m_w2,
           x_wq, x_wkv, x_wo, ln_g, ln_b):
    w = dict(e_w_in=e_w_in, e_dw_w=e_dw_w, e_dw_b=e_dw_b, e_cn_g=e_cn_g, e_cn_b=e_cn_b,
             e_w_out=e_w_out, f_w13=f_w13, f_w2=f_w2, o_w_in=o_w_in, o_q_norm=o_q_norm,
             o_k_norm=o_k_norm, o_w_out=o_w_out, m_router=m_router, m_w13=m_w13, m_w2=m_w2,
             x_wq=x_wq, x_wkv=x_wkv, x_wo=x_wo, ln_g=ln_g, ln_b=ln_b)
    for name in _MXU_WEIGHTS:
        w[name] = w[name].astype(BF16)
    return (_trunk(x_prompt, mem_prompt, w), _trunk(x_sample, mem_sample, w))
```

```python
import functools
import math

import numpy as np
import jax
import jax.numpy as jnp
from jax import lax
from jax.experimental import pallas as pl
from jax.experimental.pallas import tpu as pltpu

F32 = jnp.float32
BF16 = jnp.bfloat16

D_MODEL = 1024
DEPTH = 2
GRID_W = 64
CONV_CH = D_MODEL // 2
CONV_WIDTH = 31
B_DILATIONS = (1, 4, 16)
B_GROUPS = len(B_DILATIONS)
B_HEADS = 4
B_HD = 128
B_WIDTH = B_HEADS * B_HD
B_HALF = 64
QKV_EVEN = 3 * B_GROUPS * B_WIDTH
C_HEADS = 8
C_KV_HEADS = 2
C_GROUP = C_HEADS // C_KV_HEADS
C_HD = 128
ROPE_THETA = 10000.0
X_HEADS = 4
X_HD = D_MODEL // X_HEADS
FF_DENSE = 2816
N_EXPERTS = 8
FF_EXPERT = 3584
DN_ALPHA = (2 * DEPTH) ** 0.25
LN_EPS = 1e-5
RMS_EPS = 1e-6
NEG = -1e30

LANES = 128
SUBLANES = 8
V7X_VMEM_LIMIT = 56 * 1024 * 1024


def _params(*semantics):
    return pltpu.CompilerParams(dimension_semantics=semantics,
                                vmem_limit_bytes=V7X_VMEM_LIMIT)


def _layer_norm(y, g, b):
    mu = jnp.mean(y, -1, keepdims=True)
    yc = y - mu
    var = jnp.mean(yc * yc, -1, keepdims=True)
    return yc * lax.rsqrt(var + LN_EPS) * g + b


def _dot(a, b):
    return jnp.dot(a, b, preferred_element_type=F32)


def _dot_nt(a, b):
    return lax.dot_general(a, b, (((1,), (1,)), ((), ())), preferred_element_type=F32)


def _matmul_kernel(x_ref, w_ref, *o_refs, splits, chunk, glu_first):
    xb = x_ref[...].astype(BF16)
    col = 0
    for idx, (o_ref, width) in enumerate(zip(o_refs, splits)):
        if glu_first and idx == 0:
            half = width // 2
            for c in range(0, half, chunk):
                a = _dot(xb, w_ref[:, col + c:col + c + chunk])
                gate = _dot(xb, w_ref[:, col + half + c:col + half + c + chunk])
                o_ref[:, c:c + chunk] = (a * jax.nn.sigmoid(gate)).astype(o_ref.dtype)
        else:
            for c in range(0, width, chunk):
                o_ref[:, c:c + chunk] = _dot(xb, w_ref[:, col + c:col + c + chunk]).astype(o_ref.dtype)
        col += width


def _matmul(x, w, splits, dtypes, name, tm=512, chunk=512, glu_first=False):
    n, k = x.shape
    widths = [s // 2 if glu_first and i == 0 else s for i, s in enumerate(splits)]
    return pl.pallas_call(
        functools.partial(_matmul_kernel, splits=splits, chunk=chunk, glu_first=glu_first),
        grid=(n // tm,),
        in_specs=[pl.BlockSpec((tm, k), lambda i: (i, 0)),
                  pl.BlockSpec(w.shape, lambda i: (0, 0))],
        out_specs=[pl.BlockSpec((tm, s), lambda i: (i, 0)) for s in widths],
        out_shape=[jax.ShapeDtypeStruct((n, s), dt) for s, dt in zip(widths, dtypes)],
        compiler_params=_params("parallel"),
        name=name,
    )(x, w)


CONV_HALO = 16
CONV_ROWS = 64


def _conv_kernel(prev_ref, cur_ref, next_ref, w_ref, b_ref, g_ref, be_ref, o_ref, hwin, *, ts):
    s = pl.program_id(1)
    last = pl.num_programs(1) - 1

    hwin[0, 0:CONV_HALO, :] = jnp.where(s > 0, prev_ref[...], 0.0)
    hwin[0, CONV_HALO:CONV_HALO + ts, :] = cur_ref[...]
    hwin[0, CONV_HALO + ts:, :] = jnp.where(s < last, next_ref[...], 0.0)
    shifted_rows = ts + 2 * CONV_HALO - SUBLANES
    for r in range(1, SUBLANES):
        hwin[r, 0:shifted_rows, :] = hwin[0, r:r + shifted_rows, :]
    pad = CONV_WIDTH // 2
    for c in range(0, ts, CONV_ROWS):
        acc = jnp.broadcast_to(b_ref[...], (CONV_ROWS, CONV_CH))
        for j in range(CONV_WIDTH):
            start = CONV_HALO + c + j - pad
            r = start % SUBLANES
            acc = acc + w_ref[j:j + 1, :] * hwin[r, start - r:start - r + CONV_ROWS, :]
        y = _layer_norm(acc, g_ref[...], be_ref[...])
        o_ref[c:c + CONV_ROWS, :] = (y * jax.nn.sigmoid(y)).astype(o_ref.dtype)


def _conv_module(u, dw_w, dw_b, cn_g, cn_b, ts=512):
    b, s, _ = u.shape
    hb = ts // CONV_HALO
    nhalo = s // CONV_HALO
    row = lambda a: a.reshape(1, CONV_CH)
    return pl.pallas_call(
        functools.partial(_conv_kernel, ts=ts),
        grid=(b, s // ts),
        in_specs=[
            pl.BlockSpec((None, CONV_HALO, CONV_CH),
                         lambda i, j: (i, jnp.maximum(j * hb - 1, 0), 0)),
            pl.BlockSpec((None, ts, CONV_CH), lambda i, j: (i, j, 0)),
            pl.BlockSpec((None, CONV_HALO, CONV_CH),
                         lambda i, j: (i, jnp.minimum((j + 1) * hb, nhalo - 1), 0)),
            pl.BlockSpec((CONV_WIDTH, CONV_CH), lambda i, j: (0, 0)),
            pl.BlockSpec((1, CONV_CH), lambda i, j: (0, 0)),
            pl.BlockSpec((1, CONV_CH), lambda i, j: (0, 0)),
            pl.BlockSpec((1, CONV_CH), lambda i, j: (0, 0)),
        ],
        out_specs=pl.BlockSpec((None, ts, CONV_CH), lambda i, j: (i, j, 0)),
        out_shape=jax.ShapeDtypeStruct((b, s, CONV_CH), BF16),
        scratch_shapes=[pltpu.VMEM((SUBLANES, ts + 2 * CONV_HALO, CONV_CH), F32)],
        compiler_params=_params("parallel", "parallel"),
        name="conv_module",
    )(u, u, u, dw_w, row(dw_b), row(cn_g), row(cn_b))


BAND_Q = 128
BAND_UNROLL = 16
DIL_TILE = 2048


def _band_block(q, k, v, slope, band_bias, key_token0, key_step, seq):
    s = _dot_nt(q, k) * (1.0 / math.sqrt(B_HD)) + slope * band_bias
    col = lax.broadcasted_iota(jnp.int32, (1, 2 * BAND_Q), 1)
    token = key_token0 + col * key_step
    in_seq = token.astype(jnp.uint32) < jnp.uint32(seq)
    s = jnp.where(in_seq, s, NEG)
    m = jnp.max(s, -1, keepdims=True)
    p = jnp.exp(s - m)
    l = jnp.sum(p, -1, keepdims=True)
    return _dot(p.astype(BF16), v) / l, m + jnp.log(l)


def _dilated_kernel(slope_ref, *refs, seq):
    q_refs = refs[0:3]
    kv_refs = refs[3:21]
    o_ref = refs[21]
    kw0, vw0, qf1, kw1, vw1, qf2, kw2, vw2, og0, og1, og2, lg0, lg1, lg2 = refs[22:]
    head = pl.program_id(1)
    t0 = pl.program_id(2) * DIL_TILE

    row = lax.broadcasted_iota(jnp.int32, (BAND_Q, 2 * BAND_Q), 0)
    col = lax.broadcasted_iota(jnp.int32, (BAND_Q, 2 * BAND_Q), 1)
    dist = jnp.abs(col - row - B_HALF)
    band_bias = jnp.where(dist <= B_HALF, -dist.astype(F32), NEG)

    def stage(win, group, which, dtype):
        halo = B_HALF * B_DILATIONS[group]
        prev_ref, cur_ref, next_ref = kv_refs[group * 6 + which * 3:group * 6 + which * 3 + 3]
        win[0:halo, :] = prev_ref[...].astype(dtype)
        win[halo:halo + DIL_TILE, :] = cur_ref[...].astype(dtype)
        win[halo + DIL_TILE:, :] = next_ref[...].astype(dtype)

    def slope_of(group):
        return slope_ref[group * B_HEADS + head] * float(B_DILATIONS[group])

    stage(kw0, 0, 0, BF16)
    stage(vw0, 0, 1, BF16)
    slope0 = slope_of(0)

    def body0(i, carry):
        i0 = pl.multiple_of(i * BAND_Q, BAND_Q)
        o, lse = _band_block(q_refs[0][pl.ds(i0, BAND_Q), :], kw0[pl.ds(i0, 2 * BAND_Q), :],
                             vw0[pl.ds(i0, 2 * BAND_Q), :], slope0, band_bias,
                             t0 + i0 - B_HALF, 1, seq)
        og0[pl.ds(i0, BAND_Q), :] = o
        lg0[pl.ds(i0, BAND_Q), :] = jnp.broadcast_to(lse, (BAND_Q, B_HD))
        return carry

    lax.fori_loop(0, DIL_TILE // BAND_Q, body0, 0, unroll=BAND_UNROLL)

    for group, qf, kw, vw, og, lg in ((1, qf1, kw1, vw1, og1, lg1), (2, qf2, kw2, vw2, og2, lg2)):
        d = B_DILATIONS[group]
        nq = DIL_TILE // d
        qf[...] = q_refs[group][...].astype(F32)
        stage(kw, group, 0, F32)
        stage(vw, group, 1, F32)
        slope = slope_of(group)

        def body(r, carry, d=d, nq=nq, qf=qf, kw=kw, vw=vw, og=og, lg=lg, slope=slope):
            qr = qf[pl.ds(r, nq, stride=d), :].astype(BF16)
            kr = kw[pl.ds(r, nq + 2 * B_HALF, stride=d), :].astype(BF16)
            vr = vw[pl.ds(r, nq + 2 * B_HALF, stride=d), :].astype(BF16)
            for i0 in range(0, nq, BAND_Q):
                o, lse = _band_block(qr[i0:i0 + BAND_Q], kr[i0:i0 + 2 * BAND_Q],
                                     vr[i0:i0 + 2 * BAND_Q], slope, band_bias,
                                     t0 + r + (i0 - B_HALF) * d, d, seq)
                og[pl.ds(r + i0 * d, BAND_Q, stride=d), :] = o
                lg[pl.ds(r + i0 * d, BAND_Q, stride=d), :] = jnp.broadcast_to(lse, (BAND_Q, B_HD))
            return carry

        lax.fori_loop(0, d, body, 0, unroll=max(1, BAND_UNROLL * BAND_Q * d // DIL_TILE))

    l0, l1, l2 = lg0[...], lg1[...], lg2[...]
    top = jnp.maximum(jnp.maximum(l0, l1), l2)
    w0, w1, w2 = jnp.exp(l0 - top), jnp.exp(l1 - top), jnp.exp(l2 - top)
    mix = (w0 * og0[...] + w1 * og1[...] + w2 * og2[...]) / (w0 + w1 + w2)
    o_ref[...] = mix.astype(o_ref.dtype)


def _mixture_of_dilations(qkv):
    b, s, _ = qkv.shape
    n = B_GROUPS * B_HEADS
    slopes = jnp.asarray(np.array([2.0 ** (-8.0 * (i + 1) / n) for i in range(n)], np.float32))
    n_tiles = s // DIL_TILE

    def column(which, group):
        return (which * B_GROUPS + group) * B_HEADS

    def tile_spec(col):
        return pl.BlockSpec((None, DIL_TILE, B_HD), lambda i, h, t: (i, t, col + h))

    def halo_specs(col, group):
        halo = B_HALF * B_DILATIONS[group]
        per_tile = DIL_TILE // halo
        last = s // halo - 1
        return [pl.BlockSpec((None, halo, B_HD),
                             lambda i, h, t: (i, jnp.maximum(t * per_tile - 1, 0), col + h)),
                tile_spec(col),
                pl.BlockSpec((None, halo, B_HD),
                             lambda i, h, t: (i, jnp.minimum((t + 1) * per_tile, last), col + h))]

    in_specs = [pl.BlockSpec(memory_space=pltpu.SMEM)]
    in_specs += [tile_spec(column(0, g)) for g in range(B_GROUPS)]
    for g in range(B_GROUPS):
        in_specs += halo_specs(column(1, g), g) + halo_specs(column(2, g), g)
    scratch = []
    for g, d in enumerate(B_DILATIONS):
        win = (DIL_TILE + 2 * B_HALF * d, B_HD)
        if g == 0:
            scratch += [pltpu.VMEM(win, BF16)] * 2
        else:
            scratch += [pltpu.VMEM((DIL_TILE, B_HD), F32), pltpu.VMEM(win, F32), pltpu.VMEM(win, F32)]
    scratch += [pltpu.VMEM((DIL_TILE, B_HD), F32)] * (2 * B_GROUPS)
    return pl.pallas_call(
        functools.partial(_dilated_kernel, seq=s),
        grid=(b, B_HEADS, n_tiles),
        in_specs=in_specs,
        out_specs=pl.BlockSpec((None, DIL_TILE, B_HD), lambda i, h, t: (i, t, h)),
        out_shape=jax.ShapeDtypeStruct((b, s, B_WIDTH), BF16),
        scratch_shapes=scratch,
        compiler_params=_params("parallel", "parallel", "parallel"),
        name="dilated_mixture_attn",
    )(slopes, *([qkv] * (1 + 3 + 6 * B_GROUPS - 1)))


def _split_bf16(a):
    hi = a.astype(BF16)
    return hi, (a - hi.astype(F32)).astype(BF16)


def _route_top2(x, wh, wl):
    xh, xl = _split_bf16(x)
    logits = _dot(xh, wh) + (_dot(xh, wl) + _dot(xl, wh))
    lane = lax.broadcasted_iota(jnp.int32, logits.shape, 1)
    logits = jnp.where(lane < N_EXPERTS, logits, -jnp.inf)
    m1 = jnp.max(logits, -1, keepdims=True)
    i1 = jnp.min(jnp.where(logits == m1, lane, LANES), -1, keepdims=True)
    rest = jnp.where(lane == i1, -jnp.inf, logits)
    m2 = jnp.max(rest, -1, keepdims=True)
    i2 = jnp.min(jnp.where(rest == m2, lane, LANES), -1, keepdims=True)
    e = jnp.exp(m2 - m1)
    g1 = 1.0 / (1.0 + e)
    g2 = e / (1.0 + e)
    gates = jnp.where(lane == i1, g1, 0.0) + jnp.where(lane == i2, g2, 0.0)
    return gates, jnp.where((lane == i1) | (lane == i2), 1.0, 0.0).astype(BF16)


def _post_mixer_kernel(*refs, n_in, route):
    ins = refs[:n_in]
    ws = refs[n_in:2 * n_in]
    n_fixed = 2 * n_in + 9
    x_ref, g0_ref, b0_ref, k_ref, v_ref, wq_ref, wo_ref, g_ref, b_ref = refs[2 * n_in:n_fixed]
    if route:
        wh_ref, wl_ref, o_ref, ob_ref, gate_ref, sel_ref = refs[n_fixed:]
    else:
        (o_ref,) = refs[n_fixed:]
    y = DN_ALPHA * x_ref[...]
    for a_ref, w_ref in zip(ins, ws):
        y = y + _dot(a_ref[...], w_ref[...])
    x = _layer_norm(y, g0_ref[...], b0_ref[...])
    q = (_dot(x.astype(BF16), wq_ref[...]) * (1.0 / math.sqrt(X_HD))).astype(BF16)
    heads = []
    for h in range(X_HEADS):
        sl = slice(h * X_HD, (h + 1) * X_HD)
        s = _dot_nt(q[:, sl], k_ref[:, sl])
        m = jnp.max(s, -1, keepdims=True)
        p = jnp.exp(s - m)
        l = jnp.sum(p, -1, keepdims=True)
        heads.append((_dot(p.astype(BF16), v_ref[:, sl]) / l).astype(BF16))
    o = jnp.concatenate(heads, -1)
    y = _layer_norm(DN_ALPHA * x + _dot(o, wo_ref[...]), g_ref[...], b_ref[...])
    o_ref[...] = y
    if route:
        ob_ref[...] = y.astype(BF16)
        gate_ref[...], sel_ref[...] = _route_top2(y, wh_ref[...], wl_ref[...])


def _post_mixer(ins, ws, x, ln0, kv, wq, wo, ln1, router_w=None, tm=1024):
    bsz, s, dm = x.shape
    m = kv.shape[1]
    rows = lambda width: pl.BlockSpec((None, tm, width), lambda i, j: (i, j, 0))
    whole = lambda a: pl.BlockSpec(a.shape, lambda i, j: (0, 0))
    vec = pl.BlockSpec((1, dm), lambda i, j: (0, 0))
    out_specs = [rows(dm)]
    out_shape = [jax.ShapeDtypeStruct((bsz, s, dm), F32)]
    route_args = []
    if router_w is not None:
        route_args = list(_split_bf16(jnp.zeros((dm, LANES), F32).at[:, :N_EXPERTS].set(router_w)))
        out_specs += [rows(dm), rows(LANES), rows(LANES)]
        out_shape += [jax.ShapeDtypeStruct((bsz, s, dm), BF16),
                      jax.ShapeDtypeStruct((bsz, s, LANES), F32),
                      jax.ShapeDtypeStruct((bsz, s, LANES), BF16)]
    as_row = lambda t: t.reshape(1, dm)
    return pl.pallas_call(
        functools.partial(_post_mixer_kernel, n_in=len(ins), route=router_w is not None),
        grid=(bsz, s // tm),
        in_specs=([rows(a.shape[-1]) for a in ins] + [whole(w) for w in ws]
                  + [rows(dm), vec, vec,
                     pl.BlockSpec((None, m, dm), lambda i, j: (i, 0, 0)),
                     pl.BlockSpec((None, m, dm), lambda i, j: (i, 0, 1)),
                     whole(wq), whole(wo), vec, vec] + [whole(w) for w in route_args]),
        out_specs=out_specs,
        out_shape=out_shape,
        compiler_params=_params("parallel", "parallel"),
        name="mixer_out_xattn_ln",
    )(*ins, *ws, x, as_row(ln0[0]), as_row(ln0[1]), kv, kv, wq, wo, as_row(ln1[0]), as_row(ln1[1]),
      *route_args)


def _ffn_kernel(x_ref, w13_ref, w2_ref, g_ref, b_ref, o_ref):
    x = x_ref[...]
    xb = x.astype(BF16)
    ff = w2_ref.shape[0]
    a = _dot(xb, w13_ref[:, :ff])
    gate = _dot(xb, w13_ref[:, ff:])
    h = (a * jax.nn.sigmoid(a) * gate).astype(BF16)
    o_ref[...] = _layer_norm(DN_ALPHA * x + _dot(h, w2_ref[...]), g_ref[...], b_ref[...])


def _ffn(x, w13, w2, g, b, tm=512):
    n, dm = x.shape
    resident = lambda w: pl.BlockSpec(w.shape, lambda i: (0, 0), pipeline_mode=pl.Buffered(1))
    return pl.pallas_call(
        _ffn_kernel,
        grid=(n // tm,),
        in_specs=[pl.BlockSpec((tm, dm), lambda i: (i, 0)),
                  resident(w13), resident(w2),
                  pl.BlockSpec((1, dm), lambda i: (0, 0)),
                  pl.BlockSpec((1, dm), lambda i: (0, 0))],
        out_specs=pl.BlockSpec((tm, dm), lambda i: (i, 0)),
        out_shape=jax.ShapeDtypeStruct((n, dm), F32),
        compiler_params=_params("parallel"),
        name="dense_swiglu_ln",
    )(x, w13, w2, g.reshape(1, dm), b.reshape(1, dm))


def _rope_tables(seq_len):
    rows = seq_len // GRID_W
    row = jnp.broadcast_to(jnp.arange(rows, dtype=F32)[:, None], (rows, GRID_W)).reshape(-1)
    col = jnp.broadcast_to(jnp.arange(GRID_W, dtype=F32)[None, :], (rows, GRID_W)).reshape(-1)
    axis_dim = C_HD // 2
    freqs = ROPE_THETA ** (-jnp.arange(0, axis_dim, 2, dtype=F32) / axis_dim)
    ang_r = row[:, None] * freqs[None, :]
    ang_c = col[:, None] * freqs[None, :]
    ang = jnp.concatenate([ang_r, ang_r, ang_c, ang_c], -1)
    cos, sin = jnp.cos(ang), jnp.sin(ang)
    first = (jnp.arange(C_HD) % axis_dim) < axis_dim // 2
    sin_up = jnp.where(first[None, :], -sin, 0.0)
    sin_dn = jnp.where(first[None, :], 0.0, sin)
    return cos, sin_up, sin_dn


def _odd_in_kernel(x_ref, w_ref, kn_ref, cos_ref, su_ref, sd_ref, cq_ref, uq_ref, dq_ref,
                   qt_ref, k_ref, vt_ref):
    xb = x_ref[...].astype(BF16)
    cos, su, sd = cos_ref[...], su_ref[...], sd_ref[...]
    cq, uq, dq = cq_ref[...], uq_ref[...], dq_ref[...]
    quarter = C_HD // 4

    def norm_rope(h, gain):
        h = h * lax.rsqrt(jnp.mean(h * h, -1, keepdims=True) + RMS_EPS) * gain
        up = pltpu.roll(h, C_HD - quarter, 1)
        dn = pltpu.roll(h, quarter, 1)
        return h * cos + up * su + dn * sd

    def norm_rope_t(ht):
        inv = lax.rsqrt(jnp.mean(ht * ht, 0, keepdims=True) + RMS_EPS)
        up = jnp.concatenate([ht[quarter:], ht[:quarter]], 0)
        dn = jnp.concatenate([ht[C_HD - quarter:], ht[:C_HD - quarter]], 0)
        return (ht * cq + up * uq + dn * dq) * inv

    nq, nk = C_HEADS * C_HD, C_KV_HEADS * C_HD
    pair = 2 * C_HD
    for i in range(0, C_HEADS, 2):
        h2 = _dot(xb, w_ref[:, i * C_HD:i * C_HD + pair])
        for j in range(2):
            ht = h2[:, j * C_HD:(j + 1) * C_HD].T
            qt_ref[(i + j) * C_HD:(i + j + 1) * C_HD, :] = norm_rope_t(ht).astype(BF16)
    for i in range(0, C_KV_HEADS, 2):
        h2 = _dot(xb, w_ref[:, nq + i * C_HD:nq + i * C_HD + pair])
        for j in range(2):
            k_ref[:, (i + j) * C_HD:(i + j + 1) * C_HD] = norm_rope(
                h2[:, j * C_HD:(j + 1) * C_HD], kn_ref[...]).astype(BF16)
        v2 = _dot(xb, w_ref[:, nq + nk + i * C_HD:nq + nk + i * C_HD + pair])
        vt_ref[i * C_HD:i * C_HD + pair, :] = v2.T.astype(BF16)


def _odd_in_proj(x, w, qn, kn, tm=512):
    b, s, dm = x.shape
    nq, nk = C_HEADS * C_HD, C_KV_HEADS * C_HD
    cos, su, sd = _rope_tables(s)
    q_scale = math.log2(math.e) / math.sqrt(C_HD)
    quarter = C_HD // 4
    cq = (cos * qn[None, :] * q_scale).T
    uq = (su * jnp.roll(qn, -quarter)[None, :] * q_scale).T
    dq = (sd * jnp.roll(qn, quarter)[None, :] * q_scale).T
    tab = pl.BlockSpec((tm, C_HD), lambda i, j: (j, 0))
    tab_t = pl.BlockSpec((C_HD, tm), lambda i, j: (0, j))
    vec = pl.BlockSpec((1, C_HD), lambda i, j: (0, 0))
    return pl.pallas_call(
        _odd_in_kernel,
        grid=(b, s // tm),
        in_specs=[pl.BlockSpec((None, tm, dm), lambda i, j: (i, j, 0)),
                  pl.BlockSpec(w.shape, lambda i, j: (0, 0)),
                  vec, tab, tab, tab, tab_t, tab_t, tab_t],
        out_specs=[pl.BlockSpec((None, nq, tm), lambda i, j: (i, 0, j)),
                   pl.BlockSpec((None, tm, nk), lambda i, j: (i, j, 0)),
                   pl.BlockSpec((None, nk, tm), lambda i, j: (i, 0, j))],
        out_shape=[jax.ShapeDtypeStruct((b, nq, s), BF16),
                   jax.ShapeDtypeStruct((b, s, nk), BF16),
                   jax.ShapeDtypeStruct((b, nk, s), BF16)],
        compiler_params=_params("parallel", "parallel"),
        name="gqa_in_proj_norm_rope",
    )(x, w, kn.reshape(1, C_HD), cos, su, sd, cq, uq, dq)


def _gqa_kernel(qt_ref, k_ref, vt_ref, o_ref, acc_sc, st_sc, *, seq, tq, tk):
    qt = jnp.concatenate([qt_ref[h * C_HD:(h + 1) * C_HD, :] for h in range(C_GROUP)], 1)
    cols = C_GROUP * tq
    n = seq // tk
    acc_sc[...] = jnp.zeros(acc_sc.shape, F32)

    def scores(kc):
        k0 = pl.multiple_of(kc * tk, tk)
        return _dot(k_ref[pl.ds(k0, tk), :], qt)

    def consume(kc, st, m_old, l_old):
        k0 = pl.multiple_of(kc * tk, tk)
        m_new = jnp.maximum(m_old, jnp.max(st, 0, keepdims=True))
        alpha = jnp.exp2(m_old - m_new)
        p = jnp.exp2(st - m_new)
        l_new = alpha * l_old + jnp.sum(p, 0, keepdims=True)
        acc_sc[...] = alpha * acc_sc[...] + _dot(vt_ref[:, pl.ds(k0, tk)], p.astype(BF16))
        return m_new, l_new

    st_a, st_b = st_sc.at[0], st_sc.at[1]
    st_a[...] = scores(0)

    def body(i, carry):
        kc = 2 * i
        st_b[...] = scores(kc + 1)
        m, l = consume(kc, st_a[...], *carry)
        st_a[...] = scores(jnp.minimum(kc + 2, n - 1))
        return consume(kc + 1, st_b[...], m, l)

    init = (jnp.full((1, cols), -jnp.inf, F32), jnp.zeros((1, cols), F32))
    _, l = lax.fori_loop(0, n // 2, body, init)
    out = acc_sc[...] / l
    for h in range(C_GROUP):
        o_ref[:, h * C_HD:(h + 1) * C_HD] = out[:, h * tq:(h + 1) * tq].T.astype(o_ref.dtype)


def _gqa(qt, k, vt, tq=512, tk=512):
    b, nq, s = qt.shape
    gw = C_GROUP * C_HD
    assert (s // tk) % 2 == 0
    return pl.pallas_call(
        functools.partial(_gqa_kernel, seq=s, tq=tq, tk=tk),
        grid=(b, C_KV_HEADS, s // tq),
        in_specs=[pl.BlockSpec((None, gw, tq), lambda i, h, j: (i, h, j)),
                  pl.BlockSpec((None, s, C_HD), lambda i, h, j: (i, 0, h)),
                  pl.BlockSpec((None, C_HD, s), lambda i, h, j: (i, h, 0))],
        out_specs=pl.BlockSpec((None, tq, gw), lambda i, h, j: (i, j, h)),
        out_shape=jax.ShapeDtypeStruct((b, s, nq), BF16),
        scratch_shapes=[pltpu.VMEM((C_HD, C_GROUP * tq), F32),
                        pltpu.VMEM((2, tk, C_GROUP * tq), F32)],
        compiler_params=_params("parallel", "parallel", "parallel"),
        name="gqa_flash",
    )(qt, k, vt)


TOP_K = 2
MOE_CHUNK = 512
MOE_TILE = 256
MOE_SPAN = MOE_CHUNK // MOE_TILE + 1
MOE_PACK = 3
MOE_GROUP = 1024
UNSELECTED = -float(2 ** 26)


def _rank_kernel(sel_ref, rank_ref, rank_t_ref, base_ref, cnt_ref, carry):
    @pl.when(pl.program_id(0) == 0)
    def _():
        carry[...] = jnp.zeros(carry.shape, F32)

    sel = sel_ref[...]
    n = sel.shape[0]
    row = lax.broadcasted_iota(jnp.int32, (n, n), 0)
    col = lax.broadcasted_iota(jnp.int32, (n, n), 1)
    earlier = jnp.where(row > col, 1.0, 0.0).astype(BF16)
    before = _dot(earlier, sel) + carry[...]
    chosen = sel.astype(F32)
    rank = jnp.where(chosen > 0.0, before, UNSELECTED)
    rank_ref[...] = rank
    rank_t_ref[...] = rank.T[:SUBLANES, :]
    cnt = jnp.sum(chosen, 0, keepdims=True)
    base_ref[...] = carry[...]
    cnt_ref[...] = cnt
    carry[...] += cnt


def _rank(sel):
    n = sel.shape[0]
    c = n // MOE_CHUNK
    stat = pl.BlockSpec((None, 1, LANES), lambda i: (i, 0, 0))
    return pl.pallas_call(
        _rank_kernel,
        grid=(c,),
        in_specs=[pl.BlockSpec((MOE_CHUNK, LANES), lambda i: (i, 0))],
        out_specs=[pl.BlockSpec((MOE_CHUNK, LANES), lambda i: (i, 0)),
                   pl.BlockSpec((SUBLANES, MOE_CHUNK), lambda i: (0, i)),
                   stat, stat],
        out_shape=[jax.ShapeDtypeStruct((n, LANES), F32),
                   jax.ShapeDtypeStruct((SUBLANES, n), F32),
                   jax.ShapeDtypeStruct((c, 1, LANES), F32),
                   jax.ShapeDtypeStruct((c, 1, LANES), F32)],
        scratch_shapes=[pltpu.VMEM((1, LANES), F32)],
        compiler_params=_params("arbitrary"),
        name="moe_rank",
    )(sel)


def _fill_forward(values, valid):
    idx = lax.cummax(jnp.where(valid, jnp.arange(values.shape[0]), -1))
    idx = jnp.where(idx < 0, jnp.argmax(valid), idx)
    return values[idx]


def _moe_plan(base, cnt, n_group_tiles):
    counts = base[-1] + cnt[-1]
    gsize = -(-counts // MOE_GROUP) * MOE_GROUP
    gend = jnp.cumsum(gsize)
    gstart = gend - gsize
    n_used = gend[-1] // MOE_GROUP
    group = jnp.minimum(jnp.arange(n_group_tiles), n_used - 1)
    group_expert = jnp.minimum(jnp.searchsorted(gend, group * MOE_GROUP, side='right'), N_EXPERTS - 1)
    experts = (group.astype(jnp.int32), group_expert.astype(jnp.int32),
               (jnp.arange(n_group_tiles) < n_used).astype(jnp.int32))
    n_chunks = base.shape[0]
    n_tiles = n_group_tiles * (MOE_GROUP // MOE_TILE)
    clip = lambda t: jnp.clip(t, 0, n_tiles - 1).astype(jnp.int32)
    as_i32 = lambda t: t.astype(jnp.int32)
    first_slot = gstart[None, :] + base
    t_first = first_slot // MOE_TILE
    t_last = (first_slot + cnt - 1) // MOE_TILE
    span = jnp.arange(MOE_SPAN)
    tiles = t_first[..., None] + span
    valid = (cnt[..., None] > 0) & (tiles <= t_last[..., None])
    td = tiles.transpose(1, 0, 2).reshape(-1)
    vd = valid.transpose(1, 0, 2).reshape(-1)
    seen = lax.cummax(jnp.where(vd, td, -1))
    prev = jnp.concatenate([jnp.full((1,), -1, seen.dtype), seen[:-1]])
    first = vd & (td != prev)
    chunk_d = jnp.broadcast_to(jnp.arange(n_chunks)[None, :, None], (N_EXPERTS, n_chunks, MOE_SPAN)).reshape(-1)
    expert_d = jnp.broadcast_to(jnp.arange(N_EXPERTS)[:, None, None], (N_EXPERTS, n_chunks, MOE_SPAN)).reshape(-1)
    n_items = N_EXPERTS * n_chunks + n_tiles
    n_valid = jnp.sum(vd)
    where = jnp.where(vd, jnp.cumsum(vd) - 1, n_items)
    take = jnp.minimum(jnp.arange(n_items), n_valid - 1)

    fields = jnp.stack([as_i32(td), as_i32(chunk_d), as_i32(expert_d), as_i32(first)], -1)
    items = jnp.zeros((n_items, 4), jnp.int32).at[where].set(fields, mode='drop')[take]
    live = jnp.arange(n_items) < n_valid
    item_tile, item_chunk, item_expert = clip(items[:, 0]), items[:, 1], items[:, 2]
    item_first = (items[:, 3] == 1) & live
    idx = jnp.arange(n_items)
    rank_in_run = idx - lax.cummax(jnp.where(item_first, idx, 0))
    pos = rank_in_run % MOE_PACK
    opens = (pos == 0) & live
    pack = jnp.cumsum(opens) - 1
    n_packs = N_EXPERTS * n_chunks // MOE_PACK + n_tiles + 1
    n_open = jnp.sum(opens)
    keep = jnp.minimum(jnp.arange(n_packs), n_open - 1)
    pack_live = jnp.arange(n_packs) < n_open

    entry = jnp.stack([item_chunk, item_tile, item_expert, as_i32(rank_in_run == 0),
                       jnp.ones_like(idx)], -1)
    table = jnp.zeros((n_packs, MOE_PACK, 5), jnp.int32).at[
        jnp.where(live, pack, n_packs), pos].set(entry, mode='drop')
    present = (table[..., 4] == 1) & pack_live[:, None]
    chunks = [_fill_forward(table[:, k, 0], present[:, k]) for k in range(MOE_PACK)]
    dispatch = (table[:, 0, 1][keep], table[:, 0, 2][keep],
                table[:, 0, 3] * as_i32(pack_live), as_i32(pack_live),
                jnp.concatenate(chunks), as_i32(present).T.reshape(-1))
    tiles_ek = tiles.transpose(1, 2, 0).reshape(N_EXPERTS * MOE_SPAN, n_chunks)
    valid_ek = valid.transpose(1, 2, 0).reshape(N_EXPERTS * MOE_SPAN, n_chunks)
    combine = (clip(jax.vmap(_fill_forward)(tiles_ek, valid_ek)).reshape(-1), as_i32(valid_ek).reshape(-1))
    return gstart, experts, dispatch, combine


def _dispatch_kernel(tile_ref, expert_ref, first_ref, live_ref, chunk_ref, valid_ref, *refs):
    del chunk_ref
    slot_refs, x_refs = refs[:MOE_PACK], refs[MOE_PACK:2 * MOE_PACK]
    o_ref = refs[2 * MOE_PACK + 1]
    w = pl.program_id(0)
    n_packs = pl.num_programs(0)

    @pl.when(live_ref[w] == 1)
    def _():
        base = (tile_ref[w] * MOE_TILE).astype(F32)
        sub = lax.broadcasted_iota(jnp.int32, (MOE_TILE, MOE_CHUNK), 0).astype(F32)

        def rows(k):
            local = slot_refs[k][pl.ds(expert_ref[w], 1), :] - base
            onehot = jnp.where(sub == local, 1.0, 0.0).astype(BF16)
            return _dot(onehot, x_refs[k][...]).astype(BF16)

        @pl.when(first_ref[w] == 1)
        def _():
            o_ref[...] = rows(0)

        @pl.when(first_ref[w] == 0)
        def _():
            o_ref[...] += rows(0)

        for k in range(1, MOE_PACK):
            @pl.when(valid_ref[k * n_packs + w] == 1)
            def _(k=k):
                o_ref[...] += rows(k)


def _dispatch(plan, slot_t, xb, n_slots):
    n, dm = xb.shape
    n_packs = plan[0].shape[0]

    def chunk_of(k):
        return lambda w, tile, expert, first, live, chunk, valid: chunk[k * n_packs + w]

    slot_specs = [pl.BlockSpec((SUBLANES, MOE_CHUNK), lambda w, *p, f=chunk_of(k): (0, f(w, *p)))
                  for k in range(MOE_PACK)]
    x_specs = [pl.BlockSpec((MOE_CHUNK, dm), lambda w, *p, f=chunk_of(k): (f(w, *p), 0))
               for k in range(MOE_PACK)]
    grid_spec = pltpu.PrefetchScalarGridSpec(
        num_scalar_prefetch=6,
        grid=(n_packs,),
        in_specs=slot_specs + x_specs + [pl.BlockSpec(memory_space=pl.ANY)],
        out_specs=pl.BlockSpec((MOE_TILE, dm), lambda w, tile, *_: (tile[w], 0)),
    )
    return pl.pallas_call(
        _dispatch_kernel,
        grid_spec=grid_spec,
        out_shape=jax.ShapeDtypeStruct((n_slots, dm), BF16),
        input_output_aliases={6 + 2 * MOE_PACK: 0},
        compiler_params=_params("arbitrary"),
        name="moe_dispatch",
    )(*plan, *([slot_t] * MOE_PACK), *([xb] * MOE_PACK), jnp.zeros((n_slots, dm), BF16))


def _experts_kernel(tile_ref, expert_ref, live_ref, x_ref, w1_ref, w3_ref, w2_ref, o_ref, acc):
    del tile_ref, expert_ref
    j = pl.program_id(1)

    @pl.when(live_ref[pl.program_id(0)] == 1)
    def _():
        xb = x_ref[...]
        a = _dot(xb, w1_ref[...])
        gate = _dot(xb, w3_ref[...])
        y = _dot((a * jax.nn.sigmoid(a) * gate).astype(BF16), w2_ref[...])

        @pl.when(j == 0)
        def _():
            acc[...] = y

        @pl.when(j > 0)
        def _():
            acc[...] += y

        @pl.when(j == pl.num_programs(1) - 1)
        def _():
            o_ref[...] = acc[...].astype(BF16)


def _experts(plan, xs, w13, w2, tf=1792):
    n_slots, dm = xs.shape
    ff = w2.shape[1]
    nf = ff // tf

    def chunk(i, j, live):
        return j * live[i] + (nf - 1) * (1 - live[i])

    grid_spec = pltpu.PrefetchScalarGridSpec(
        num_scalar_prefetch=3,
        grid=(n_slots // MOE_GROUP, nf),
        in_specs=[pl.BlockSpec((MOE_GROUP, dm), lambda i, j, tile, te, live: (tile[i], 0)),
                  pl.BlockSpec((None, dm, tf), lambda i, j, tile, te, live: (te[i], 0, chunk(i, j, live))),
                  pl.BlockSpec((None, dm, tf), lambda i, j, tile, te, live: (te[i], 0, nf + chunk(i, j, live))),
                  pl.BlockSpec((None, tf, dm), lambda i, j, tile, te, live: (te[i], chunk(i, j, live), 0))],
        out_specs=pl.BlockSpec((MOE_GROUP, dm), lambda i, j, tile, te, live: (tile[i], 0)),
        scratch_shapes=[pltpu.VMEM((MOE_GROUP, dm), F32)],
    )
    return pl.pallas_call(
        _experts_kernel,
        grid_spec=grid_spec,
        out_shape=jax.ShapeDtypeStruct((n_slots, dm), BF16),
        compiler_params=_params("arbitrary", "arbitrary"),
        name="moe_experts_swiglu",
    )(*plan, xs, w13, w13, w2)


def _combine_kernel(tile_ref, valid_ref, slot_ref, gate_ref, *rest, n_chunks):
    n_y = N_EXPERTS * MOE_SPAN
    y_refs = rest[:n_y]
    x_ref, g_ref, b_ref, o_ref, acc = rest[n_y:]
    c = pl.program_id(0)
    acc[...] = DN_ALPHA * x_ref[...]
    col = lax.broadcasted_iota(jnp.int32, (MOE_CHUNK, MOE_TILE), 1).astype(F32)
    for e in range(N_EXPERTS):
        slot = slot_ref[:, e:e + 1]
        gate = gate_ref[:, e:e + 1]
        for k in range(MOE_SPAN):
            w = (e * MOE_SPAN + k) * n_chunks + c

            @pl.when(valid_ref[w] == 1)
            def _(w=w, slot=slot, gate=gate, y_ref=y_refs[e * MOE_SPAN + k]):
                local = slot - (tile_ref[w] * MOE_TILE).astype(F32)
                onehot = jnp.where(col == local, 1.0, 0.0).astype(BF16)
                acc[...] += gate * _dot(onehot, y_ref[...])

    o_ref[...] = _layer_norm(acc[...], g_ref[...], b_ref[...])


def _combine(plan, slot, gates, ys, x, g, b):
    n, dm = x.shape
    c = n // MOE_CHUNK
    n_y = N_EXPERTS * MOE_SPAN
    tok = lambda width: pl.BlockSpec((MOE_CHUNK, width), lambda i, *_: (i, 0))
    vec = pl.BlockSpec((1, dm), lambda i, *_: (0, 0))

    def tile_spec(ek):
        return pl.BlockSpec((MOE_TILE, dm), lambda i, tile, valid: (tile[ek * c + i], 0))

    grid_spec = pltpu.PrefetchScalarGridSpec(
        num_scalar_prefetch=2,
        grid=(c,),
        in_specs=[tok(LANES), tok(LANES)] + [tile_spec(ek) for ek in range(n_y)] + [tok(dm), vec, vec],
        out_specs=tok(dm),
        scratch_shapes=[pltpu.VMEM((MOE_CHUNK, dm), F32)],
    )
    return pl.pallas_call(
        functools.partial(_combine_kernel, n_chunks=c),
        grid_spec=grid_spec,
        out_shape=jax.ShapeDtypeStruct((n, dm), F32),
        compiler_params=_params("parallel"),
        name="moe_combine_ln",
    )(*plan, slot, gates, *([ys] * n_y), x, g.reshape(1, dm), b.reshape(1, dm))


def _moe(x, xb, gates, sel, w13, w2, g, b):
    n, dm = x.shape
    n_slots = TOP_K * n + N_EXPERTS * MOE_GROUP
    rank, rank_t, base, cnt = _rank(sel)
    as_counts = lambda t: t[:, 0, :N_EXPERTS].astype(jnp.int32)
    gstart, experts_plan, dispatch_plan, combine_plan = _moe_plan(
        as_counts(base), as_counts(cnt), n_slots // MOE_GROUP)
    gstart = gstart.astype(F32)
    slot = rank + jnp.zeros((LANES,), F32).at[:N_EXPERTS].set(gstart)[None, :]
    slot_t = rank_t + gstart[:, None]
    xs = _dispatch(dispatch_plan, slot_t, xb, n_slots)
    ys = _experts(experts_plan, xs, w13, w2)
    return _combine(combine_plan, slot, gates, ys, x, g, b)


def _trunk(x, mem, w):
    b, s, dm = x.shape
    n = b * s
    mem2 = mem.reshape(-1, dm)
    for i in range(DEPTH):
        j = i // 2
        even = i % 2 == 0
        x2 = x.reshape(n, dm)
        if even:
            u, qkv = _matmul(x2, w['e_w_in'][j], (2 * CONV_CH, QKV_EVEN), (F32, BF16), "even_in_proj",
                             glu_first=True)
            a_out = _conv_module(u.reshape(b, s, CONV_CH), w['e_dw_w'][j], w['e_dw_b'][j],
                                 w['e_cn_g'][j], w['e_cn_b'][j])
            b_out = _mixture_of_dilations(qkv.reshape(b, s, QKV_EVEN))
            w_out = w['e_w_out'][j]
            mixed, w_mix = [a_out, b_out], [w_out[:CONV_CH], w_out[CONV_CH:]]
        else:
            qt, k, vt = _odd_in_proj(x, w['o_w_in'][j], w['o_q_norm'][j], w['o_k_norm'][j])
            mixed, w_mix = [_gqa(qt, k, vt)], [w['o_w_out'][j]]
        (kv,) = _matmul(mem2, w['x_wkv'][i], (2 * dm,), (BF16,), "memory_kv_proj", tm=256)
        outs = _post_mixer(mixed, w_mix, x, (w['ln_g'][i, 0], w['ln_b'][i, 0]),
                           kv.reshape(b, -1, 2 * dm), w['x_wq'][i], w['x_wo'][i],
                           (w['ln_g'][i, 1], w['ln_b'][i, 1]),
                           router_w=None if even else w['m_router'][j])
        x2 = outs[0].reshape(n, dm)
        if even:
            x2 = _ffn(x2, w['f_w13'][j], w['f_w2'][j], w['ln_g'][i, 2], w['ln_b'][i, 2])
        else:
            xb, gates, sel = outs[1].reshape(n, dm), outs[2].reshape(n, LANES), outs[3].reshape(n, LANES)
            x2 = _moe(x2, xb, gates, sel, w['m_w13'][j], w['m_w2'][j], w['ln_g'][i, 2], w['ln_b'][i, 2])
        x = x2.reshape(b, s, dm)
    return x


_MXU_WEIGHTS = ('e_w_in', 'e_w_out', 'f_w13', 'f_w2', 'o_w_in', 'o_w_out', 'm_w13', 'm_w2',
                'x_wq', 'x_wkv', 'x_wo')


def kernel(x_prompt, x_sample, mem_prompt, mem_sample, e_w_in, e_dw_w, e_dw_b, e_cn_g, e_cn_b,
           e_w_out, f_w13, f_w2, o_w_in, o_q_norm, o_k_norm, o_w_out, m_router, m_w13, m_w2,
           x_wq, x_wkv, x_wo, ln_g, ln_b):
    w = dict(e_w_in=e_w_in, e_dw_w=e_dw_w, e_dw_b=e_dw_b, e_cn_g=e_cn_g, e_cn_b=e_cn_b,
             e_w_out=e_w_out, f_w13=f_w13, f_w2=f_w2, o_w_in=o_w_in, o_q_norm=o_q_norm,
             o_k_norm=o_k_norm, o_w_out=o_w_out, m_router=m_router, m_w13=m_w13, m_w2=m_w2,
             x_wq=x_wq, x_wkv=x_wkv, x_wo=x_wo, ln_g=ln_g, ln_b=ln_b)
    for name in _MXU_WEIGHTS:
        w[name] = w[name].astype(BF16)
    return (_trunk(x_prompt, mem_prompt, w), _trunk(x_sample, mem_sample, w))
```

```python
import functools
import math

import numpy as np
import jax
import jax.numpy as jnp
from jax import lax
from jax.experimental import pallas as pl
from jax.experimental.pallas import tpu as pltpu

F32 = jnp.float32
BF16 = jnp.bfloat16

D_MODEL = 1024
DEPTH = 2
GRID_W = 64
CONV_CH = D_MODEL // 2
CONV_WIDTH = 31
B_DILATIONS = (1, 4, 16)
B_GROUPS = len(B_DILATIONS)
B_HEADS = 4
B_HD = 128
B_WIDTH = B_HEADS * B_HD
B_HALF = 64
QKV_EVEN = 3 * B_GROUPS * B_WIDTH
C_HEADS = 8
C_KV_HEADS = 2
C_GROUP = C_HEADS // C_KV_HEADS
C_HD = 128
ROPE_THETA = 10000.0
X_HEADS = 4
X_HD = D_MODEL // X_HEADS
FF_DENSE = 2816
N_EXPERTS = 8
FF_EXPERT = 3584
DN_ALPHA = (2 * DEPTH) ** 0.25
LN_EPS = 1e-5
RMS_EPS = 1e-6
NEG = -1e30

LANES = 128
SUBLANES = 8
V7X_VMEM_LIMIT = 56 * 1024 * 1024


def _params(*semantics):
    return pltpu.CompilerParams(dimension_semantics=semantics,
                                vmem_limit_bytes=V7X_VMEM_LIMIT)


def _layer_norm(y, g, b):
    mu = jnp.mean(y, -1, keepdims=True)
    yc = y - mu
    var = jnp.mean(yc * yc, -1, keepdims=True)
    return yc * lax.rsqrt(var + LN_EPS) * g + b


def _dot(a, b):
    return jnp.dot(a, b, preferred_element_type=F32)


def _dot_nt(a, b):
    return lax.dot_general(a, b, (((1,), (1,)), ((), ())), preferred_element_type=F32)


def _matmul_kernel(x_ref, w_ref, *o_refs, splits, chunk, glu_first):
    xb = x_ref[...].astype(BF16)
    col = 0
    for idx, (o_ref, width) in enumerate(zip(o_refs, splits)):
        if glu_first and idx == 0:
            half = width // 2
            for c in range(0, half, chunk):
                a = _dot(xb, w_ref[:, col + c:col + c + chunk])
                gate = _dot(xb, w_ref[:, col + half + c:col + half + c + chunk])
                o_ref[:, c:c + chunk] = (a * jax.nn.sigmoid(gate)).astype(o_ref.dtype)
        else:
            for c in range(0, width, chunk):
                o_ref[:, c:c + chunk] = _dot(xb, w_ref[:, col + c:col + c + chunk]).astype(o_ref.dtype)
        col += width


def _matmul(x, w, splits, dtypes, name, tm=512, chunk=512, glu_first=False):
    n, k = x.shape
    widths = [s // 2 if glu_first and i == 0 else s for i, s in enumerate(splits)]
    return pl.pallas_call(
        functools.partial(_matmul_kernel, splits=splits, chunk=chunk, glu_first=glu_first),
        grid=(n // tm,),
        in_specs=[pl.BlockSpec((tm, k), lambda i: (i, 0)),
                  pl.BlockSpec(w.shape, lambda i: (0, 0))],
        out_specs=[pl.BlockSpec((tm, s), lambda i: (i, 0)) for s in widths],
        out_shape=[jax.ShapeDtypeStruct((n, s), dt) for s, dt in zip(widths, dtypes)],
        compiler_params=_params("parallel"),
        name=name,
    )(x, w)


CONV_HALO = 16
CONV_ROWS = 64


def _conv_kernel(prev_ref, cur_ref, next_ref, w_ref, b_ref, g_ref, be_ref, o_ref, hwin, *, ts):
    s = pl.program_id(1)
    last = pl.num_programs(1) - 1

    hwin[0, 0:CONV_HALO, :] = jnp.where(s > 0, prev_ref[...], 0.0)
    hwin[0, CONV_HALO:CONV_HALO + ts, :] = cur_ref[...]
    hwin[0, CONV_HALO + ts:, :] = jnp.where(s < last, next_ref[...], 0.0)
    shifted_rows = ts + 2 * CONV_HALO - SUBLANES
    for r in range(1, SUBLANES):
        hwin[r, 0:shifted_rows, :] = hwin[0, r:r + shifted_rows, :]
    pad = CONV_WIDTH // 2
    for c in range(0, ts, CONV_ROWS):
        acc = jnp.broadcast_to(b_ref[...], (CONV_ROWS, CONV_CH))
        for j in range(CONV_WIDTH):
            start = CONV_HALO + c + j - pad
            r = start % SUBLANES
            acc = acc + w_ref[j:j + 1, :] * hwin[r, start - r:start - r + CONV_ROWS, :]
        y = _layer_norm(acc, g_ref[...], be_ref[...])
        o_ref[c:c + CONV_ROWS, :] = (y * jax.nn.sigmoid(y)).astype(o_ref.dtype)


def _conv_module(u, dw_w, dw_b, cn_g, cn_b, ts=512):
    b, s, _ = u.shape
    hb = ts // CONV_HALO
    nhalo = s // CONV_HALO
    row = lambda a: a.reshape(1, CONV_CH)
    return pl.pallas_call(
        functools.partial(_conv_kernel, ts=ts),
        grid=(b, s // ts),
        in_specs=[
            pl.BlockSpec((None, CONV_HALO, CONV_CH),
                         lambda i, j: (i, jnp.maximum(j * hb - 1, 0), 0)),
            pl.BlockSpec((None, ts, CONV_CH), lambda i, j: (i, j, 0)),
            pl.BlockSpec((None, CONV_HALO, CONV_CH),
                         lambda i, j: (i, jnp.minimum((j + 1) * hb, nhalo - 1), 0)),
            pl.BlockSpec((CONV_WIDTH, CONV_CH), lambda i, j: (0, 0)),
            pl.BlockSpec((1, CONV_CH), lambda i, j: (0, 0)),
            pl.BlockSpec((1, CONV_CH), lambda i, j: (0, 0)),
            pl.BlockSpec((1, CONV_CH), lambda i, j: (0, 0)),
        ],
        out_specs=pl.BlockSpec((None, ts, CONV_CH), lambda i, j: (i, j, 0)),
        out_shape=jax.ShapeDtypeStruct((b, s, CONV_CH), BF16),
        scratch_shapes=[pltpu.VMEM((SUBLANES, ts + 2 * CONV_HALO, CONV_CH), F32)],
        compiler_params=_params("parallel", "parallel"),
        name="conv_module",
    )(u, u, u, dw_w, row(dw_b), row(cn_g), row(cn_b))


BAND_Q = 128
BAND_UNROLL = 16
DIL_TILE = 2048


def _band_block(q, k, v, slope, band_bias, key_token0, key_step, seq):
    s = _dot_nt(q, k) * (1.0 / math.sqrt(B_HD)) + slope * band_bias
    col = lax.broadcasted_iota(jnp.int32, (1, 2 * BAND_Q), 1)
    token = key_token0 + col * key_step
    in_seq = token.astype(jnp.uint32) < jnp.uint32(seq)
    s = jnp.where(in_seq, s, NEG)
    m = jnp.max(s, -1, keepdims=True)
    p = jnp.exp(s - m)
    l = jnp.sum(p, -1, keepdims=True)
    return _dot(p.astype(BF16), v) / l, m + jnp.log(l)


def _dilated_kernel(slope_ref, *refs, seq):
    q_refs = refs[0:3]
    kv_refs = refs[3:21]
    o_ref = refs[21]
    kw0, vw0, qf1, kw1, vw1, qf2, kw2, vw2, og0, og1, og2, lg0, lg1, lg2 = refs[22:]
    head = pl.program_id(1)
    t0 = pl.program_id(2) * DIL_TILE

    row = lax.broadcasted_iota(jnp.int32, (BAND_Q, 2 * BAND_Q), 0)
    col = lax.broadcasted_iota(jnp.int32, (BAND_Q, 2 * BAND_Q), 1)
    dist = jnp.abs(col - row - B_HALF)
    band_bias = jnp.where(dist <= B_HALF, -dist.astype(F32), NEG)

    def stage(win, group, which, dtype):
        halo = B_HALF * B_DILATIONS[group]
        prev_ref, cur_ref, next_ref = kv_refs[group * 6 + which * 3:group * 6 + which * 3 + 3]
        win[0:halo, :] = prev_ref[...].astype(dtype)
        win[halo:halo + DIL_TILE, :] = cur_ref[...].astype(dtype)
        win[halo + DIL_TILE:, :] = next_ref[...].astype(dtype)

    def slope_of(group):
        return slope_ref[group * B_HEADS + head] * float(B_DILATIONS[group])

    stage(kw0, 0, 0, BF16)
    stage(vw0, 0, 1, BF16)
    slope0 = slope_of(0)

    def body0(i, carry):
        i0 = pl.multiple_of(i * BAND_Q, BAND_Q)
        o, lse = _band_block(q_refs[0][pl.ds(i0, BAND_Q), :], kw0[pl.ds(i0, 2 * BAND_Q), :],
                             vw0[pl.ds(i0, 2 * BAND_Q), :], slope0, band_bias,
                             t0 + i0 - B_HALF, 1, seq)
        og0[pl.ds(i0, BAND_Q), :] = o
        lg0[pl.ds(i0, BAND_Q), :] = jnp.broadcast_to(lse, (BAND_Q, B_HD))
        return carry

    lax.fori_loop(0, DIL_TILE // BAND_Q, body0, 0, unroll=BAND_UNROLL)

    for group, qf, kw, vw, og, lg in ((1, qf1, kw1, vw1, og1, lg1), (2, qf2, kw2, vw2, og2, lg2)):
        d = B_DILATIONS[group]
        nq = DIL_TILE // d
        qf[...] = q_refs[group][...].astype(F32)
        stage(kw, group, 0, F32)
        stage(vw, group, 1, F32)
        slope = slope_of(group)

        def body(r, carry, d=d, nq=nq, qf=qf, kw=kw, vw=vw, og=og, lg=lg, slope=slope):
            qr = qf[pl.ds(r, nq, stride=d), :].astype(BF16)
            kr = kw[pl.ds(r, nq + 2 * B_HALF, stride=d), :].astype(BF16)
            vr = vw[pl.ds(r, nq + 2 * B_HALF, stride=d), :].astype(BF16)
            for i0 in range(0, nq, BAND_Q):
                o, lse = _band_block(qr[i0:i0 + BAND_Q], kr[i0:i0 + 2 * BAND_Q],
                                     vr[i0:i0 + 2 * BAND_Q], slope, band_bias,
                                     t0 + r + (i0 - B_HALF) * d, d, seq)
                og[pl.ds(r + i0 * d, BAND_Q, stride=d), :] = o
                lg[pl.ds(r + i0 * d, BAND_Q, stride=d), :] = jnp.broadcast_to(lse, (BAND_Q, B_HD))
            return carry

        lax.fori_loop(0, d, body, 0, unroll=max(1, BAND_UNROLL * BAND_Q * d // DIL_TILE))

    l0, l1, l2 = lg0[...], lg1[...], lg2[...]
    top = jnp.maximum(jnp.maximum(l0, l1), l2)
    w0, w1, w2 = jnp.exp(l0 - top), jnp.exp(l1 - top), jnp.exp(l2 - top)
    mix = (w0 * og0[...] + w1 * og1[...] + w2 * og2[...]) / (w0 + w1 + w2)
    o_ref[...] = mix.astype(o_ref.dtype)


def _mixture_of_dilations(qkv):
    b, s, _ = qkv.shape
    n = B_GROUPS * B_HEADS
    slopes = jnp.asarray(np.array([2.0 ** (-8.0 * (i + 1) / n) for i in range(n)], np.float32))
    n_tiles = s // DIL_TILE

    def column(which, group):
        return (which * B_GROUPS + group) * B_HEADS

    def tile_spec(col):
        return pl.BlockSpec((None, DIL_TILE, B_HD), lambda i, h, t: (i, t, col + h))

    def halo_specs(col, group):
        halo = B_HALF * B_DILATIONS[group]
        per_tile = DIL_TILE // halo
        last = s // halo - 1
        return [pl.BlockSpec((None, halo, B_HD),
                             lambda i, h, t: (i, jnp.maximum(t * per_tile - 1, 0), col + h)),
                tile_spec(col),
                pl.BlockSpec((None, halo, B_HD),
                             lambda i, h, t: (i, jnp.minimum((t + 1) * per_tile, last), col + h))]

    in_specs = [pl.BlockSpec(memory_space=pltpu.SMEM)]
    in_specs += [tile_spec(column(0, g)) for g in range(B_GROUPS)]
    for g in range(B_GROUPS):
        in_specs += halo_specs(column(1, g), g) + halo_specs(column(2, g), g)
    scratch = []
    for g, d in enumerate(B_DILATIONS):
        win = (DIL_TILE + 2 * B_HALF * d, B_HD)
        if g == 0:
            scratch += [pltpu.VMEM(win, BF16)] * 2
        else:
            scratch += [pltpu.VMEM((DIL_TILE, B_HD), F32), pltpu.VMEM(win, F32), pltpu.VMEM(win, F32)]
    scratch += [pltpu.VMEM((DIL_TILE, B_HD), F32)] * (2 * B_GROUPS)
    return pl.pallas_call(
        functools.partial(_dilated_kernel, seq=s),
        grid=(b, B_HEADS, n_tiles),
        in_specs=in_specs,
        out_specs=pl.BlockSpec((None, DIL_TILE, B_HD), lambda i, h, t: (i, t, h)),
        out_shape=jax.ShapeDtypeStruct((b, s, B_WIDTH), BF16),
        scratch_shapes=scratch,
        compiler_params=_params("parallel", "parallel", "parallel"),
        name="dilated_mixture_attn",
    )(slopes, *([qkv] * (1 + 3 + 6 * B_GROUPS - 1)))


def _split_bf16(a):
    hi = a.astype(BF16)
    return hi, (a - hi.astype(F32)).astype(BF16)


def _route_top2(x, wh, wl):
    xh, xl = _split_bf16(x)
    logits = _dot(xh, wh) + (_dot(xh, wl) + _dot(xl, wh))
    lane = lax.broadcasted_iota(jnp.int32, logits.shape, 1)
    logits = jnp.where(lane < N_EXPERTS, logits, -jnp.inf)
    m1 = jnp.max(logits, -1, keepdims=True)
    i1 = jnp.min(jnp.where(logits == m1, lane, LANES), -1, keepdims=True)
    rest = jnp.where(lane == i1, -jnp.inf, logits)
    m2 = jnp.max(rest, -1, keepdims=True)
    i2 = jnp.min(jnp.where(rest == m2, lane, LANES), -1, keepdims=True)
    e = jnp.exp(m2 - m1)
    g1 = 1.0 / (1.0 + e)
    g2 = e / (1.0 + e)
    gates = jnp.where(lane == i1, g1, 0.0) + jnp.where(lane == i2, g2, 0.0)
    return gates, jnp.where((lane == i1) | (lane == i2), 1.0, 0.0).astype(BF16)


def _post_mixer_kernel(*refs, n_in, route):
    ins = refs[:n_in]
    ws = refs[n_in:2 * n_in]
    n_fixed = 2 * n_in + 9
    x_ref, g0_ref, b0_ref, k_ref, v_ref, wq_ref, wo_ref, g_ref, b_ref = refs[2 * n_in:n_fixed]
    if route:
        o_ref, ob_ref = refs[n_fixed:]
    else:
        (o_ref,) = refs[n_fixed:]
    y = DN_ALPHA * x_ref[...]
    for a_ref, w_ref in zip(ins, ws):
        y = y + _dot(a_ref[...], w_ref[...])
    x = _layer_norm(y, g0_ref[...], b0_ref[...])
    q = (_dot(x.astype(BF16), wq_ref[...]) * (1.0 / math.sqrt(X_HD))).astype(BF16)
    heads = []
    for h in range(X_HEADS):
        sl = slice(h * X_HD, (h + 1) * X_HD)
        s = _dot_nt(q[:, sl], k_ref[:, sl])
        m = jnp.max(s, -1, keepdims=True)
        p = jnp.exp(s - m)
        l = jnp.sum(p, -1, keepdims=True)
        heads.append((_dot(p.astype(BF16), v_ref[:, sl]) / l).astype(BF16))
    o = jnp.concatenate(heads, -1)
    y = _layer_norm(DN_ALPHA * x + _dot(o, wo_ref[...]), g_ref[...], b_ref[...])
    o_ref[...] = y
    if route:
        ob_ref[...] = y.astype(BF16)


def _router_kernel(x_ref, wh_ref, wl_ref, gate_ref, sel_ref):
    gate_ref[...], sel_ref[...] = _route_top2(x_ref[...], wh_ref[...], wl_ref[...])


def _router(x, w, tm=512):
    n, dm = x.shape
    wh, wl = _split_bf16(jnp.zeros((dm, LANES), F32).at[:, :N_EXPERTS].set(w))
    return pl.pallas_call(
        _router_kernel,
        grid=(n // tm,),
        in_specs=[pl.BlockSpec((tm, dm), lambda i: (i, 0)),
                  pl.BlockSpec((dm, LANES), lambda i: (0, 0)),
                  pl.BlockSpec((dm, LANES), lambda i: (0, 0))],
        out_specs=[pl.BlockSpec((tm, LANES), lambda i: (i, 0))] * 2,
        out_shape=[jax.ShapeDtypeStruct((n, LANES), F32), jax.ShapeDtypeStruct((n, LANES), BF16)],
        compiler_params=_params("parallel"),
        name="moe_router_top2",
    )(x, wh, wl)


def _post_mixer(ins, ws, x, ln0, kv, wq, wo, ln1, router_w=None, tm=1024):
    bsz, s, dm = x.shape
    m = kv.shape[1]
    rows = lambda width: pl.BlockSpec((None, tm, width), lambda i, j: (i, j, 0))
    whole = lambda a: pl.BlockSpec(a.shape, lambda i, j: (0, 0))
    vec = pl.BlockSpec((1, dm), lambda i, j: (0, 0))
    out_specs = [rows(dm)]
    out_shape = [jax.ShapeDtypeStruct((bsz, s, dm), F32)]
    route_args = []
    if router_w is not None:
        out_specs += [rows(dm)]
        out_shape += [jax.ShapeDtypeStruct((bsz, s, dm), BF16)]
    as_row = lambda t: t.reshape(1, dm)
    return pl.pallas_call(
        functools.partial(_post_mixer_kernel, n_in=len(ins), route=router_w is not None),
        grid=(bsz, s // tm),
        in_specs=([rows(a.shape[-1]) for a in ins] + [whole(w) for w in ws]
                  + [rows(dm), vec, vec,
                     pl.BlockSpec((None, m, dm), lambda i, j: (i, 0, 0)),
                     pl.BlockSpec((None, m, dm), lambda i, j: (i, 0, 1)),
                     whole(wq), whole(wo), vec, vec] + [whole(w) for w in route_args]),
        out_specs=out_specs,
        out_shape=out_shape,
        compiler_params=_params("parallel", "parallel"),
        name="mixer_out_xattn_ln",
    )(*ins, *ws, x, as_row(ln0[0]), as_row(ln0[1]), kv, kv, wq, wo, as_row(ln1[0]), as_row(ln1[1]),
      *route_args)


def _ffn_kernel(x_ref, w13_ref, w2_ref, g_ref, b_ref, o_ref):
    x = x_ref[...]
    xb = x.astype(BF16)
    ff = w2_ref.shape[0]
    a = _dot(xb, w13_ref[:, :ff])
    gate = _dot(xb, w13_ref[:, ff:])
    h = (a * jax.nn.sigmoid(a) * gate).astype(BF16)
    o_ref[...] = _layer_norm(DN_ALPHA * x + _dot(h, w2_ref[...]), g_ref[...], b_ref[...])


def _ffn(x, w13, w2, g, b, tm=512):
    n, dm = x.shape
    resident = lambda w: pl.BlockSpec(w.shape, lambda i: (0, 0), pipeline_mode=pl.Buffered(1))
    return pl.pallas_call(
        _ffn_kernel,
        grid=(n // tm,),
        in_specs=[pl.BlockSpec((tm, dm), lambda i: (i, 0)),
                  resident(w13), resident(w2),
                  pl.BlockSpec((1, dm), lambda i: (0, 0)),
                  pl.BlockSpec((1, dm), lambda i: (0, 0))],
        out_specs=pl.BlockSpec((tm, dm), lambda i: (i, 0)),
        out_shape=jax.ShapeDtypeStruct((n, dm), F32),
        compiler_params=_params("parallel"),
        name="dense_swiglu_ln",
    )(x, w13, w2, g.reshape(1, dm), b.reshape(1, dm))


def _rope_tables(seq_len):
    rows = seq_len // GRID_W
    row = jnp.broadcast_to(jnp.arange(rows, dtype=F32)[:, None], (rows, GRID_W)).reshape(-1)
    col = jnp.broadcast_to(jnp.arange(GRID_W, dtype=F32)[None, :], (rows, GRID_W)).reshape(-1)
    axis_dim = C_HD // 2
    freqs = ROPE_THETA ** (-jnp.arange(0, axis_dim, 2, dtype=F32) / axis_dim)
    ang_r = row[:, None] * freqs[None, :]
    ang_c = col[:, None] * freqs[None, :]
    ang = jnp.concatenate([ang_r, ang_r, ang_c, ang_c], -1)
    cos, sin = jnp.cos(ang), jnp.sin(ang)
    first = (jnp.arange(C_HD) % axis_dim) < axis_dim // 2
    sin_up = jnp.where(first[None, :], -sin, 0.0)
    sin_dn = jnp.where(first[None, :], 0.0, sin)
    return cos, sin_up, sin_dn


def _odd_in_kernel(x_ref, w_ref, kn_ref, cos_ref, su_ref, sd_ref, cq_ref, uq_ref, dq_ref,
                   qt_ref, k_ref, vt_ref):
    xb = x_ref[...].astype(BF16)
    cos, su, sd = cos_ref[...], su_ref[...], sd_ref[...]
    cq, uq, dq = cq_ref[...], uq_ref[...], dq_ref[...]
    quarter = C_HD // 4

    def norm_rope(h, gain):
        h = h * lax.rsqrt(jnp.mean(h * h, -1, keepdims=True) + RMS_EPS) * gain
        up = pltpu.roll(h, C_HD - quarter, 1)
        dn = pltpu.roll(h, quarter, 1)
        return h * cos + up * su + dn * sd

    def norm_rope_t(ht):
        inv = lax.rsqrt(jnp.mean(ht * ht, 0, keepdims=True) + RMS_EPS)
        up = jnp.concatenate([ht[quarter:], ht[:quarter]], 0)
        dn = jnp.concatenate([ht[C_HD - quarter:], ht[:C_HD - quarter]], 0)
        return (ht * cq + up * uq + dn * dq) * inv

    nq, nk = C_HEADS * C_HD, C_KV_HEADS * C_HD
    pair = 2 * C_HD
    for i in range(0, C_HEADS, 2):
        h2 = _dot(xb, w_ref[:, i * C_HD:i * C_HD + pair])
        for j in range(2):
            ht = h2[:, j * C_HD:(j + 1) * C_HD].T
            qt_ref[(i + j) * C_HD:(i + j + 1) * C_HD, :] = norm_rope_t(ht).astype(BF16)
    for i in range(0, C_KV_HEADS, 2):
        h2 = _dot(xb, w_ref[:, nq + i * C_HD:nq + i * C_HD + pair])
        for j in range(2):
            k_ref[:, (i + j) * C_HD:(i + j + 1) * C_HD] = norm_rope(
                h2[:, j * C_HD:(j + 1) * C_HD], kn_ref[...]).astype(BF16)
        v2 = _dot(xb, w_ref[:, nq + nk + i * C_HD:nq + nk + i * C_HD + pair])
        vt_ref[i * C_HD:i * C_HD + pair, :] = v2.T.astype(BF16)


def _odd_in_proj(x, w, qn, kn, tm=512):
    b, s, dm = x.shape
    nq, nk = C_HEADS * C_HD, C_KV_HEADS * C_HD
    cos, su, sd = _rope_tables(s)
    q_scale = math.log2(math.e) / math.sqrt(C_HD)
    quarter = C_HD // 4
    cq = (cos * qn[None, :] * q_scale).T
    uq = (su * jnp.roll(qn, -quarter)[None, :] * q_scale).T
    dq = (sd * jnp.roll(qn, quarter)[None, :] * q_scale).T
    tab = pl.BlockSpec((tm, C_HD), lambda i, j: (j, 0))
    tab_t = pl.BlockSpec((C_HD, tm), lambda i, j: (0, j))
    vec = pl.BlockSpec((1, C_HD), lambda i, j: (0, 0))
    return pl.pallas_call(
        _odd_in_kernel,
        grid=(b, s // tm),
        in_specs=[pl.BlockSpec((None, tm, dm), lambda i, j: (i, j, 0)),
                  pl.BlockSpec(w.shape, lambda i, j: (0, 0)),
                  vec, tab, tab, tab, tab_t, tab_t, tab_t],
        out_specs=[pl.BlockSpec((None, nq, tm), lambda i, j: (i, 0, j)),
                   pl.BlockSpec((None, tm, nk), lambda i, j: (i, j, 0)),
                   pl.BlockSpec((None, nk, tm), lambda i, j: (i, 0, j))],
        out_shape=[jax.ShapeDtypeStruct((b, nq, s), BF16),
                   jax.ShapeDtypeStruct((b, s, nk), BF16),
                   jax.ShapeDtypeStruct((b, nk, s), BF16)],
        compiler_params=_params("parallel", "parallel"),
        name="gqa_in_proj_norm_rope",
    )(x, w, kn.reshape(1, C_HD), cos, su, sd, cq, uq, dq)


def _gqa_kernel(qt_ref, k_ref, vt_ref, o_ref, acc_sc, st_sc, *, seq, tq, tk):
    qt = jnp.concatenate([qt_ref[h * C_HD:(h + 1) * C_HD, :] for h in range(C_GROUP)], 1)
    cols = C_GROUP * tq
    n = seq // tk
    acc_sc[...] = jnp.zeros(acc_sc.shape, F32)

    def scores(kc):
        k0 = pl.multiple_of(kc * tk, tk)
        return _dot(k_ref[pl.ds(k0, tk), :], qt)

    def consume(kc, st, m_old, l_old):
        k0 = pl.multiple_of(kc * tk, tk)
        m_new = jnp.maximum(m_old, jnp.max(st, 0, keepdims=True))
        alpha = jnp.exp2(m_old - m_new)
        p = jnp.exp2(st - m_new)
        l_new = alpha * l_old + jnp.sum(p, 0, keepdims=True)
        acc_sc[...] = alpha * acc_sc[...] + _dot(vt_ref[:, pl.ds(k0, tk)], p.astype(BF16))
        return m_new, l_new

    st_a, st_b = st_sc.at[0], st_sc.at[1]
    st_a[...] = scores(0)

    def body(i, carry):
        kc = 2 * i
        st_b[...] = scores(kc + 1)
        m, l = consume(kc, st_a[...], *carry)
        st_a[...] = scores(jnp.minimum(kc + 2, n - 1))
        return consume(kc + 1, st_b[...], m, l)

    init = (jnp.full((1, cols), -jnp.inf, F32), jnp.zeros((1, cols), F32))
    _, l = lax.fori_loop(0, n // 2, body, init)
    out = acc_sc[...] / l
    for h in range(C_GROUP):
        o_ref[:, h * C_HD:(h + 1) * C_HD] = out[:, h * tq:(h + 1) * tq].T.astype(o_ref.dtype)


def _gqa(qt, k, vt, tq=512, tk=512):
    b, nq, s = qt.shape
    gw = C_GROUP * C_HD
    assert (s // tk) % 2 == 0
    return pl.pallas_call(
        functools.partial(_gqa_kernel, seq=s, tq=tq, tk=tk),
        grid=(b, C_KV_HEADS, s // tq),
        in_specs=[pl.BlockSpec((None, gw, tq), lambda i, h, j: (i, h, j)),
                  pl.BlockSpec((None, s, C_HD), lambda i, h, j: (i, 0, h)),
                  pl.BlockSpec((None, C_HD, s), lambda i, h, j: (i, h, 0))],
        out_specs=pl.BlockSpec((None, tq, gw), lambda i, h, j: (i, j, h)),
        out_shape=jax.ShapeDtypeStruct((b, s, nq), BF16),
        scratch_shapes=[pltpu.VMEM((C_HD, C_GROUP * tq), F32),
                        pltpu.VMEM((2, tk, C_GROUP * tq), F32)],
        compiler_params=_params("parallel", "parallel", "parallel"),
        name="gqa_flash",
    )(qt, k, vt)


TOP_K = 2
MOE_CHUNK = 512
MOE_TILE = 256
MOE_SPAN = MOE_CHUNK // MOE_TILE + 1
MOE_PACK = 3
MOE_GROUP = 1024
UNSELECTED = -float(2 ** 26)


def _rank_kernel(sel_ref, rank_ref, rank_t_ref, base_ref, cnt_ref, carry):
    @pl.when(pl.program_id(0) == 0)
    def _():
        carry[...] = jnp.zeros(carry.shape, F32)

    sel = sel_ref[...]
    n = sel.shape[0]
    row = lax.broadcasted_iota(jnp.int32, (n, n), 0)
    col = lax.broadcasted_iota(jnp.int32, (n, n), 1)
    earlier = jnp.where(row > col, 1.0, 0.0).astype(BF16)
    before = _dot(earlier, sel) + carry[...]
    chosen = sel.astype(F32)
    rank = jnp.where(chosen > 0.0, before, UNSELECTED)
    rank_ref[...] = rank
    rank_t_ref[...] = rank.T[:SUBLANES, :]
    cnt = jnp.sum(chosen, 0, keepdims=True)
    base_ref[...] = carry[...]
    cnt_ref[...] = cnt
    carry[...] += cnt


def _rank(sel):
    n = sel.shape[0]
    c = n // MOE_CHUNK
    stat = pl.BlockSpec((None, 1, LANES), lambda i: (i, 0, 0))
    return pl.pallas_call(
        _rank_kernel,
        grid=(c,),
        in_specs=[pl.BlockSpec((MOE_CHUNK, LANES), lambda i: (i, 0))],
        out_specs=[pl.BlockSpec((MOE_CHUNK, LANES), lambda i: (i, 0)),
                   pl.BlockSpec((SUBLANES, MOE_CHUNK), lambda i: (0, i)),
                   stat, stat],
        out_shape=[jax.ShapeDtypeStruct((n, LANES), F32),
                   jax.ShapeDtypeStruct((SUBLANES, n), F32),
                   jax.ShapeDtypeStruct((c, 1, LANES), F32),
                   jax.ShapeDtypeStruct((c, 1, LANES), F32)],
        scratch_shapes=[pltpu.VMEM((1, LANES), F32)],
        compiler_params=_params("arbitrary"),
        name="moe_rank",
    )(sel)


def _fill_forward(values, valid):
    idx = lax.cummax(jnp.where(valid, jnp.arange(values.shape[0]), -1))
    idx = jnp.where(idx < 0, jnp.argmax(valid), idx)
    return values[idx]


def _moe_plan(base, cnt, n_group_tiles):
    counts = base[-1] + cnt[-1]
    gsize = -(-counts // MOE_GROUP) * MOE_GROUP
    gend = jnp.cumsum(gsize)
    gstart = gend - gsize
    n_used = gend[-1] // MOE_GROUP
    group = jnp.minimum(jnp.arange(n_group_tiles), n_used - 1)
    group_expert = jnp.minimum(jnp.searchsorted(gend, group * MOE_GROUP, side='right'), N_EXPERTS - 1)
    experts = (group.astype(jnp.int32), group_expert.astype(jnp.int32),
               (jnp.arange(n_group_tiles) < n_used).astype(jnp.int32))
    n_chunks = base.shape[0]
    n_tiles = n_group_tiles * (MOE_GROUP // MOE_TILE)
    clip = lambda t: jnp.clip(t, 0, n_tiles - 1).astype(jnp.int32)
    as_i32 = lambda t: t.astype(jnp.int32)
    first_slot = gstart[None, :] + base
    t_first = first_slot // MOE_TILE
    t_last = (first_slot + cnt - 1) // MOE_TILE
    span = jnp.arange(MOE_SPAN)
    tiles = t_first[..., None] + span
    valid = (cnt[..., None] > 0) & (tiles <= t_last[..., None])
    td = tiles.transpose(1, 0, 2).reshape(-1)
    vd = valid.transpose(1, 0, 2).reshape(-1)
    seen = lax.cummax(jnp.where(vd, td, -1))
    prev = jnp.concatenate([jnp.full((1,), -1, seen.dtype), seen[:-1]])
    first = vd & (td != prev)
    chunk_d = jnp.broadcast_to(jnp.arange(n_chunks)[None, :, None], (N_EXPERTS, n_chunks, MOE_SPAN)).reshape(-1)
    expert_d = jnp.broadcast_to(jnp.arange(N_EXPERTS)[:, None, None], (N_EXPERTS, n_chunks, MOE_SPAN)).reshape(-1)
    n_items = N_EXPERTS * n_chunks + n_tiles
    n_valid = jnp.sum(vd)
    where = jnp.where(vd, jnp.cumsum(vd) - 1, n_items)
    take = jnp.minimum(jnp.arange(n_items), n_valid - 1)

    fields = jnp.stack([as_i32(td), as_i32(chunk_d), as_i32(expert_d), as_i32(first)], -1)
    items = jnp.zeros((n_items, 4), jnp.int32).at[where].set(fields, mode='drop')[take]
    live = jnp.arange(n_items) < n_valid
    item_tile, item_chunk, item_expert = clip(items[:, 0]), items[:, 1], items[:, 2]
    item_first = (items[:, 3] == 1) & live
    idx = jnp.arange(n_items)
    rank_in_run = idx - lax.cummax(jnp.where(item_first, idx, 0))
    pos = rank_in_run % MOE_PACK
    opens = (pos == 0) & live
    pack = jnp.cumsum(opens) - 1
    n_packs = N_EXPERTS * n_chunks // MOE_PACK + n_tiles + 1
    n_open = jnp.sum(opens)
    keep = jnp.minimum(jnp.arange(n_packs), n_open - 1)
    pack_live = jnp.arange(n_packs) < n_open

    entry = jnp.stack([item_chunk, item_tile, item_expert, as_i32(rank_in_run == 0),
                       jnp.ones_like(idx)], -1)
    table = jnp.zeros((n_packs, MOE_PACK, 5), jnp.int32).at[
        jnp.where(live, pack, n_packs), pos].set(entry, mode='drop')
    present = (table[..., 4] == 1) & pack_live[:, None]
    chunks = [_fill_forward(table[:, k, 0], present[:, k]) for k in range(MOE_PACK)]
    dispatch = (table[:, 0, 1][keep], table[:, 0, 2][keep],
                table[:, 0, 3] * as_i32(pack_live), as_i32(pack_live),
                jnp.concatenate(chunks), as_i32(present).T.reshape(-1))
    tiles_ek = tiles.transpose(1, 2, 0).reshape(N_EXPERTS * MOE_SPAN, n_chunks)
    valid_ek = valid.transpose(1, 2, 0).reshape(N_EXPERTS * MOE_SPAN, n_chunks)
    combine = (clip(jax.vmap(_fill_forward)(tiles_ek, valid_ek)).reshape(-1), as_i32(valid_ek).reshape(-1))
    return gstart, experts, dispatch, combine


def _dispatch_kernel(tile_ref, expert_ref, first_ref, live_ref, chunk_ref, valid_ref, *refs):
    del chunk_ref
    slot_refs, x_refs = refs[:MOE_PACK], refs[MOE_PACK:2 * MOE_PACK]
    o_ref = refs[2 * MOE_PACK + 1]
    w = pl.program_id(0)
    n_packs = pl.num_programs(0)

    @pl.when(live_ref[w] == 1)
    def _():
        base = (tile_ref[w] * MOE_TILE).astype(F32)
        sub = lax.broadcasted_iota(jnp.int32, (MOE_TILE, MOE_CHUNK), 0).astype(F32)

        def rows(k):
            local = slot_refs[k][pl.ds(expert_ref[w], 1), :] - base
            onehot = jnp.where(sub == local, 1.0, 0.0).astype(BF16)
            return _dot(onehot, x_refs[k][...]).astype(BF16)

        @pl.when(first_ref[w] == 1)
        def _():
            o_ref[...] = rows(0)

        @pl.when(first_ref[w] == 0)
        def _():
            o_ref[...] += rows(0)

        for k in range(1, MOE_PACK):
            @pl.when(valid_ref[k * n_packs + w] == 1)
            def _(k=k):
                o_ref[...] += rows(k)


def _dispatch(plan, slot_t, xb, n_slots):
    n, dm = xb.shape
    n_packs = plan[0].shape[0]

    def chunk_of(k):
        return lambda w, tile, expert, first, live, chunk, valid: chunk[k * n_packs + w]

    slot_specs = [pl.BlockSpec((SUBLANES, MOE_CHUNK), lambda w, *p, f=chunk_of(k): (0, f(w, *p)))
                  for k in range(MOE_PACK)]
    x_specs = [pl.BlockSpec((MOE_CHUNK, dm), lambda w, *p, f=chunk_of(k): (f(w, *p), 0))
               for k in range(MOE_PACK)]
    grid_spec = pltpu.PrefetchScalarGridSpec(
        num_scalar_prefetch=6,
        grid=(n_packs,),
        in_specs=slot_specs + x_specs + [pl.BlockSpec(memory_space=pl.ANY)],
        out_specs=pl.BlockSpec((MOE_TILE, dm), lambda w, tile, *_: (tile[w], 0)),
    )
    return pl.pallas_call(
        _dispatch_kernel,
        grid_spec=grid_spec,
        out_shape=jax.ShapeDtypeStruct((n_slots, dm), BF16),
        input_output_aliases={6 + 2 * MOE_PACK: 0},
        compiler_params=_params("arbitrary"),
        name="moe_dispatch",
    )(*plan, *([slot_t] * MOE_PACK), *([xb] * MOE_PACK), jnp.zeros((n_slots, dm), BF16))


def _experts_kernel(tile_ref, expert_ref, live_ref, x_ref, w1_ref, w3_ref, w2_ref, o_ref, acc):
    del tile_ref, expert_ref
    j = pl.program_id(1)

    @pl.when(live_ref[pl.program_id(0)] == 1)
    def _():
        xb = x_ref[...]
        a = _dot(xb, w1_ref[...])
        gate = _dot(xb, w3_ref[...])
        y = _dot((a * jax.nn.sigmoid(a) * gate).astype(BF16), w2_ref[...])

        @pl.when(j == 0)
        def _():
            acc[...] = y

        @pl.when(j > 0)
        def _():
            acc[...] += y

        @pl.when(j == pl.num_programs(1) - 1)
        def _():
            o_ref[...] = acc[...].astype(BF16)


def _experts(plan, xs, w13, w2, tf=1792):
    n_slots, dm = xs.shape
    ff = w2.shape[1]
    nf = ff // tf

    def chunk(i, j, live):
        return j * live[i] + (nf - 1) * (1 - live[i])

    grid_spec = pltpu.PrefetchScalarGridSpec(
        num_scalar_prefetch=3,
        grid=(n_slots // MOE_GROUP, nf),
        in_specs=[pl.BlockSpec((MOE_GROUP, dm), lambda i, j, tile, te, live: (tile[i], 0)),
                  pl.BlockSpec((None, dm, tf), lambda i, j, tile, te, live: (te[i], 0, chunk(i, j, live))),
                  pl.BlockSpec((None, dm, tf), lambda i, j, tile, te, live: (te[i], 0, nf + chunk(i, j, live))),
                  pl.BlockSpec((None, tf, dm), lambda i, j, tile, te, live: (te[i], chunk(i, j, live), 0))],
        out_specs=pl.BlockSpec((MOE_GROUP, dm), lambda i, j, tile, te, live: (tile[i], 0)),
        scratch_shapes=[pltpu.VMEM((MOE_GROUP, dm), F32)],
    )
    return pl.pallas_call(
        _experts_kernel,
        grid_spec=grid_spec,
        out_shape=jax.ShapeDtypeStruct((n_slots, dm), BF16),
        compiler_params=_params("arbitrary", "arbitrary"),
        name="moe_experts_swiglu",
    )(*plan, xs, w13, w13, w2)


def _combine_kernel(tile_ref, valid_ref, slot_ref, gate_ref, *rest, n_chunks):
    n_y = N_EXPERTS * MOE_SPAN
    y_refs = rest[:n_y]
    x_ref, g_ref, b_ref, o_ref, acc = rest[n_y:]
    c = pl.program_id(0)
    acc[...] = DN_ALPHA * x_ref[...]
    col = lax.broadcasted_iota(jnp.int32, (MOE_CHUNK, MOE_TILE), 1).astype(F32)
    for e in range(N_EXPERTS):
        slot = slot_ref[:, e:e + 1]
        gate = gate_ref[:, e:e + 1]
        for k in range(MOE_SPAN):
            w = (e * MOE_SPAN + k) * n_chunks + c

            @pl.when(valid_ref[w] == 1)
            def _(w=w, slot=slot, gate=gate, y_ref=y_refs[e * MOE_SPAN + k]):
                local = slot - (tile_ref[w] * MOE_TILE).astype(F32)
                onehot = jnp.where(col == local, 1.0, 0.0).astype(BF16)
                acc[...] += gate * _dot(onehot, y_ref[...])

    o_ref[...] = _layer_norm(acc[...], g_ref[...], b_ref[...])


def _combine(plan, slot, gates, ys, x, g, b):
    n, dm = x.shape
    c = n // MOE_CHUNK
    n_y = N_EXPERTS * MOE_SPAN
    tok = lambda width: pl.BlockSpec((MOE_CHUNK, width), lambda i, *_: (i, 0))
    vec = pl.BlockSpec((1, dm), lambda i, *_: (0, 0))

    def tile_spec(ek):
        return pl.BlockSpec((MOE_TILE, dm), lambda i, tile, valid: (tile[ek * c + i], 0))

    grid_spec = pltpu.PrefetchScalarGridSpec(
        num_scalar_prefetch=2,
        grid=(c,),
        in_specs=[tok(LANES), tok(LANES)] + [tile_spec(ek) for ek in range(n_y)] + [tok(dm), vec, vec],
        out_specs=tok(dm),
        scratch_shapes=[pltpu.VMEM((MOE_CHUNK, dm), F32)],
    )
    return pl.pallas_call(
        functools.partial(_combine_kernel, n_chunks=c),
        grid_spec=grid_spec,
        out_shape=jax.ShapeDtypeStruct((n, dm), F32),
        compiler_params=_params("parallel"),
        name="moe_combine_ln",
    )(*plan, slot, gates, *([ys] * n_y), x, g.reshape(1, dm), b.reshape(1, dm))


def _moe(x, xb, gates, sel, w13, w2, g, b):
    n, dm = x.shape
    n_slots = TOP_K * n + N_EXPERTS * MOE_GROUP
    rank, rank_t, base, cnt = _rank(sel)
    as_counts = lambda t: t[:, 0, :N_EXPERTS].astype(jnp.int32)
    gstart, experts_plan, dispatch_plan, combine_plan = _moe_plan(
        as_counts(base), as_counts(cnt), n_slots // MOE_GROUP)
    gstart = gstart.astype(F32)
    slot = rank + jnp.zeros((LANES,), F32).at[:N_EXPERTS].set(gstart)[None, :]
    slot_t = rank_t + gstart[:, None]
    xs = _dispatch(dispatch_plan, slot_t, xb, n_slots)
    ys = _experts(experts_plan, xs, w13, w2)
    return _combine(combine_plan, slot, gates, ys, x, g, b)


def _trunk(x, mem, w):
    b, s, dm = x.shape
    n = b * s
    mem2 = mem.reshape(-1, dm)
    for i in range(DEPTH):
        j = i // 2
        even = i % 2 == 0
        x2 = x.reshape(n, dm)
        if even:
            u, qkv = _matmul(x2, w['e_w_in'][j], (2 * CONV_CH, QKV_EVEN), (F32, BF16), "even_in_proj",
                             glu_first=True)
            a_out = _conv_module(u.reshape(b, s, CONV_CH), w['e_dw_w'][j], w['e_dw_b'][j],
                                 w['e_cn_g'][j], w['e_cn_b'][j])
            b_out = _mixture_of_dilations(qkv.reshape(b, s, QKV_EVEN))
            w_out = w['e_w_out'][j]
            mixed, w_mix = [a_out, b_out], [w_out[:CONV_CH], w_out[CONV_CH:]]
        else:
            qt, k, vt = _odd_in_proj(x, w['o_w_in'][j], w['o_q_norm'][j], w['o_k_norm'][j])
            mixed, w_mix = [_gqa(qt, k, vt)], [w['o_w_out'][j]]
        (kv,) = _matmul(mem2, w['x_wkv'][i], (2 * dm,), (BF16,), "memory_kv_proj", tm=256)
        outs = _post_mixer(mixed, w_mix, x, (w['ln_g'][i, 0], w['ln_b'][i, 0]),
                           kv.reshape(b, -1, 2 * dm), w['x_wq'][i], w['x_wo'][i],
                           (w['ln_g'][i, 1], w['ln_b'][i, 1]),
                           router_w=None if even else w['m_router'][j])
        x2 = outs[0].reshape(n, dm)
        if even:
            x2 = _ffn(x2, w['f_w13'][j], w['f_w2'][j], w['ln_g'][i, 2], w['ln_b'][i, 2])
        else:
            xb = outs[1].reshape(n, dm)
            gates, sel = _router(x2, w['m_router'][j])
            x2 = _moe(x2, xb, gates, sel, w['m_w13'][j], w['m_w2'][j], w['ln_g'][i, 2], w['ln_b'][i, 2])
        x = x2.reshape(b, s, dm)
    return x


_MXU_WEIGHTS = ('e_w_in', 'e_w_out', 'f_w13', 'f_w2', 'o_w_in', 'o_w_out', 'm_w13', 'm_w2',
                'x_wq', 'x_wkv', 'x_wo')


def kernel(x_prompt, x_sample, mem_prompt, mem_sample, e_w_in, e_dw_w, e_dw_b, e_cn_g, e_cn_b,
           e_w_out, f_w13, f_w2, o_w_in, o_q_norm, o_k_norm, o_w_out, m_router, m_w13, m_w2,
           x_wq, x_wkv, x_wo, ln_g, ln_b):
    w = dict(e_w_in=e_w_in, e_dw_w=e_dw_w, e_dw_b=e_dw_b, e_cn_g=e_cn_g, e_cn_b=e_cn_b,
             e_w_out=e_w_out, f_w13=f_w13, f_w2=f_w2, o_w_in=o_w_in, o_q_norm=o_q_norm,
             o_k_norm=o_k_norm, o_w_out=o_w_out, m_router=m_router, m_w13=m_w13, m_w2=m_w2,
             x_wq=x_wq, x_wkv=x_wkv, x_wo=x_wo, ln_g=ln_g, ln_b=ln_b)
    for name in _MXU_WEIGHTS:
        w[name] = w[name].astype(BF16)
    return (_trunk(x_prompt, mem_prompt, w), _trunk(x_sample, mem_sample, w))
```
